```python
import math
import jax
import jax.numpy as jnp
from jax import lax
import numpy as np

D_MODEL = 1024
BATCH = 2
SEQ = 8192
DEPTH = 2

PLE_DIM = 256
EPS = 1e-6

N_ATTN_HEADS = 8
N_BUCKETS = 32
MAX_DISTANCE = 128

SC_WIDTH = D_MODEL // 2
CONV_WIDTH = 3
SWA_HEADS = N_ATTN_HEADS
SWA_KV_HEADS = 2
SWA_HEAD_DIM = (D_MODEL // 2) // SWA_HEADS
SWA_WINDOW = 128
SWA_BLOCK = 128
SWA_Q_WIDTH = SWA_HEADS * SWA_HEAD_DIM
SWA_KV_WIDTH = SWA_KV_HEADS * SWA_HEAD_DIM
D_IN_EVEN = 3 * SC_WIDTH + SWA_Q_WIDTH + 2 * SWA_KV_WIDTH

MOBA_HEADS = N_ATTN_HEADS
MOBA_HEAD_DIM = D_MODEL // MOBA_HEADS
MOBA_BLOCK = 256
MOBA_TOPK = 3
MOBA_Q_CHUNK = 32

D_FF = ((8 * D_MODEL // 3 + 127) // 128) * 128
N_EXPERTS = 8
TOP_K = 2
D_FF_EXPERT = 7 * D_MODEL // 2

N_EVEN = (DEPTH + 1) // 2
N_ODD = DEPTH // 2

kernel_name = "hybrid_conv_swa_moba_moe_trunk"


def rms_norm(x, g):
    xf = x.astype(jnp.float32)
    y = xf * lax.rsqrt(jnp.mean(xf * xf, axis=-1, keepdims=True) + EPS)
    return (y * g.astype(jnp.float32)).astype(x.dtype)


def t5_bucket(dist):
    d = jnp.maximum(dist, 0).astype(jnp.int32)
    max_exact = N_BUCKETS // 2
    scaled = jnp.log(jnp.maximum(d, max_exact).astype(jnp.float32) / max_exact) / math.log(MAX_DISTANCE / max_exact)
    large = jnp.minimum(max_exact + (scaled * (N_BUCKETS - max_exact)).astype(jnp.int32), N_BUCKETS - 1)
    return jnp.where(d < max_exact, d, large)


def short_conv_mixer(b_gate, c_gate, u, conv_w):
    cu = c_gate * u
    y = lax.conv_general_dilated(
        cu, conv_w[:, None, :].astype(cu.dtype), window_strides=(1,),
        padding=[(CONV_WIDTH - 1, 0)], dimension_numbers=("NWC", "WIO", "NWC"),
        feature_group_count=cu.shape[-1])
    return b_gate * y


def sliding_window_attention(q, k, v, sinks, rel_bias):
    bsz, seq, n_h, hd = q.shape
    n_kv = k.shape[2]
    grp = n_h // n_kv
    nb = seq // SWA_BLOCK
    blk = SWA_BLOCK
    qb = q.reshape(bsz, nb, blk, n_kv, grp, hd)

    def band(t):
        tb = t.reshape(bsz, nb, blk, n_kv, hd)
        prev = jnp.concatenate([jnp.zeros_like(tb[:, :1]), tb[:, :-1]], axis=1)
        return jnp.concatenate([prev, tb], axis=2)

    kw, vw = band(k), band(v)
    logits = jnp.einsum("bnqkgd,bnjkd->bnkgqj", qb, kw,
                        preferred_element_type=jnp.float32) * (hd ** -0.5)
    qi = jnp.arange(blk)[:, None]
    kj = jnp.arange(2 * blk)[None, :]
    diff = qi - kj + blk
    kpos = jnp.arange(nb)[:, None, None] * blk - blk + kj[None]
    valid = (diff >= 0)[None] & (diff < SWA_WINDOW)[None] & (kpos >= 0)
    bias = rel_bias[t5_bucket(diff)].astype(jnp.float32)
    bias = bias.transpose(2, 0, 1).reshape(n_kv, grp, blk, 2 * blk)
    logits = jnp.where(valid[None, :, None, None], logits + bias, -jnp.inf)
    sink = jnp.broadcast_to(sinks.astype(jnp.float32).reshape(1, 1, n_kv, grp, 1, 1),
                            logits.shape[:-1] + (1,))
    probs = jax.nn.softmax(jnp.concatenate([logits, sink], axis=-1), axis=-1)[..., :-1]
    out = jnp.einsum("bnkgqj,bnjkd->bnqkgd", probs.astype(v.dtype), vw)
    return out.reshape(bsz, seq, n_h * hd)


def moba_attention(q, k, v, rel_bias):
    bsz, seq, n_h, hd = q.shape
    bs = MOBA_BLOCK
    s_pad = ((seq + bs - 1) // bs) * bs
    pad = [(0, 0), (0, s_pad - seq), (0, 0), (0, 0)]
    q, k, v = jnp.pad(q, pad), jnp.pad(k, pad), jnp.pad(v, pad)
    nblk = s_pad // bs
    topk = min(MOBA_TOPK, nblk)
    scale = hd ** -0.5

    kb = k.reshape(bsz, nblk, bs, n_h, hd)
    kmean = jnp.mean(kb.astype(jnp.float32), axis=2)
    gate = jnp.einsum("bshd,bnhd->bhsn", q.astype(jnp.float32), kmean)
    qblk = jnp.arange(s_pad) // bs
    past = jnp.arange(nblk)[None, :] < qblk[:, None]
    gate = jnp.where(past[None, None], gate, -jnp.inf)
    _, sel = lax.top_k(gate, topk)
    sel_valid = sel < qblk[None, None, :, None]

    kbh = kb.transpose(0, 3, 1, 2, 4)
    vbh = v.reshape(bsz, nblk, bs, n_h, hd).transpose(0, 3, 1, 2, 4)
    bi = jnp.arange(bsz)[:, None, None, None]
    hi = jnp.arange(n_h)[None, :, None, None]
    hi5 = jnp.arange(n_h)[None, :, None, None, None]
    qc_len = MOBA_Q_CHUNK
    n_chunks = s_pad // qc_len

    def chunk(c):
        start = c * qc_len
        qc = lax.dynamic_slice_in_dim(q, start, qc_len, axis=1)
        selc = lax.dynamic_slice_in_dim(sel, start, qc_len, axis=2)
        validc = lax.dynamic_slice_in_dim(sel_valid, start, qc_len, axis=2)
        kg = kbh[bi, hi, selc]
        vg = vbh[bi, hi, selc]
        qpos = start + jnp.arange(qc_len)
        kpos_sel = selc[..., None] * bs + jnp.arange(bs)
        d_sel = qpos[None, None, :, None, None] - kpos_sel
        l_sel = jnp.einsum("bqhd,bhqtjd->bhqtj", qc, kg,
                           preferred_element_type=jnp.float32) * scale
        l_sel = l_sel + rel_bias[t5_bucket(d_sel), hi5].astype(jnp.float32)
        l_sel = jnp.where(validc[..., None], l_sel, -jnp.inf)
        l_sel = l_sel.reshape(bsz, n_h, qc_len, topk * bs)
        own = start // bs
        ko = lax.dynamic_slice_in_dim(k, own * bs, bs, axis=1)
        vo = lax.dynamic_slice_in_dim(v, own * bs, bs, axis=1)
        d_own = qpos[:, None] - (own * bs + jnp.arange(bs))[None, :]
        l_own = jnp.einsum("bqhd,bjhd->bhqj", qc, ko,
                           preferred_element_type=jnp.float32) * scale
        l_own = l_own + rel_bias[t5_bucket(d_own)].astype(jnp.float32).transpose(2, 0, 1)[None]
        l_own = jnp.where((d_own >= 0)[None, None], l_own, -jnp.inf)
        probs = jax.nn.softmax(jnp.concatenate([l_sel, l_own], axis=-1), axis=-1).astype(v.dtype)
        p_sel, p_own = probs[..., : topk * bs], probs[..., topk * bs:]
        out = jnp.einsum("bhqj,bhqjd->bqhd", p_sel, vg.reshape(bsz, n_h, qc_len, topk * bs, hd))
        return out + jnp.einsum("bhqj,bjhd->bqhd", p_own, vo)

    outs = lax.map(chunk, jnp.arange(n_chunks))
    out = outs.transpose(1, 0, 2, 3, 4).reshape(bsz, s_pad, n_h * hd)
    return out[:, :seq]


def swiglu(x, w_gate, w_up, w_down):
    return (jax.nn.silu(x @ w_gate) * (x @ w_up)) @ w_down


def moe_swiglu(x, w_router, w_gate, w_up, w_down):
    logits = (x @ w_router).astype(jnp.float32)
    top_vals, top_idx = lax.top_k(logits, TOP_K)
    top_w = jax.nn.softmax(top_vals, axis=-1)
    gates = jnp.sum(jax.nn.one_hot(top_idx, N_EXPERTS, dtype=jnp.float32) * top_w[..., None], axis=-2)
    out = jnp.zeros_like(x)
    for e in range(N_EXPERTS):
        out = out + gates[..., e:e + 1].astype(x.dtype) * swiglu(x, w_gate[e], w_up[e], w_down[e])
    return out


def per_layer_embedding(h, p_i, g, w_gate, w_proj):
    return h + jax.nn.sigmoid(rms_norm(h, g) @ w_gate) * (p_i @ w_proj)


def setup_inputs(seed: int = 0) -> dict:
    key = jax.random.key(seed)
    ks = iter(jax.random.split(key, 40))
    f32 = jnp.float32

    def dense(shape, fan_in):
        return jax.random.normal(next(ks), shape, f32) * (fan_in ** -0.5)

    def gain(shape):
        return 1.0 + 0.05 * jax.random.normal(next(ks), shape, f32)

    return {
        "x": jax.random.normal(next(ks), (BATCH, SEQ, D_MODEL), f32),
        "p": jax.random.normal(next(ks), (DEPTH, BATCH, SEQ, PLE_DIM), f32),
        "rel_bias": 0.5 * jax.random.normal(next(ks), (N_BUCKETS, N_ATTN_HEADS), f32),
        "final_norm": gain((D_MODEL,)),
        "e_norm_mix": gain((N_EVEN, D_MODEL)),
        "e_w_in": dense((N_EVEN, D_MODEL, D_IN_EVEN), D_MODEL),
        "e_conv_w": dense((N_EVEN, CONV_WIDTH, SC_WIDTH), CONV_WIDTH),
        "e_sinks": 0.5 * jax.random.normal(next(ks), (N_EVEN, SWA_HEADS), f32),
        "e_w_out": dense((N_EVEN, SC_WIDTH + SWA_Q_WIDTH, D_MODEL), SC_WIDTH + SWA_Q_WIDTH),
        "e_norm_ffn": gain((N_EVEN, D_MODEL)),
        "e_ffn_gate": dense((N_EVEN, D_MODEL, D_FF), D_MODEL),
        "e_ffn_up": dense((N_EVEN, D_MODEL, D_FF), D_MODEL),
        "e_ffn_down": dense((N_EVEN, D_FF, D_MODEL), D_FF),
        "e_norm_ple": gain((N_EVEN, D_MODEL)),
        "e_ple_gate": dense((N_EVEN, D_MODEL, D_MODEL), D_MODEL),
        "e_ple_proj": dense((N_EVEN, PLE_DIM, D_MODEL), PLE_DIM),
        "o_norm_mix": gain((N_ODD, D_MODEL)),
        "o_w_qkv": dense((N_ODD, D_MODEL, 3 * D_MODEL), D_MODEL),
        "o_w_o": dense((N_ODD, D_MODEL, D_MODEL), D_MODEL),
        "o_norm_ffn": gain((N_ODD, D_MODEL)),
        "o_router": dense((N_ODD, D_MODEL, N_EXPERTS), D_MODEL),
        "o_exp_gate": dense((N_ODD, N_EXPERTS, D_MODEL, D_FF_EXPERT), D_MODEL),
        "o_exp_up": dense((N_ODD, N_EXPERTS, D_MODEL, D_FF_EXPERT), D_MODEL),
        "o_exp_down": dense((N_ODD, N_EXPERTS, D_FF_EXPERT, D_MODEL), D_FF_EXPERT),
        "o_norm_ple": gain((N_ODD, D_MODEL)),
        "o_ple_gate": dense((N_ODD, D_MODEL, D_MODEL), D_MODEL),
        "o_ple_proj": dense((N_ODD, PLE_DIM, D_MODEL), PLE_DIM),
    }


def reference(x, p, rel_bias, final_norm,
              e_norm_mix, e_w_in, e_conv_w, e_sinks, e_w_out,
              e_norm_ffn, e_ffn_gate, e_ffn_up, e_ffn_down,
              e_norm_ple, e_ple_gate, e_ple_proj,
              o_norm_mix, o_w_qkv, o_w_o, o_norm_ffn, o_router,
              o_exp_gate, o_exp_up, o_exp_down,
              o_norm_ple, o_ple_gate, o_ple_proj):
    bsz, seq, _ = x.shape
    splits_even = [SC_WIDTH, 2 * SC_WIDTH, 3 * SC_WIDTH, 3 * SC_WIDTH + SWA_Q_WIDTH,
                   3 * SC_WIDTH + SWA_Q_WIDTH + SWA_KV_WIDTH]
    h = x
    for i in range(DEPTH):
        j = i // 2
        if i % 2 == 0:
            z = rms_norm(h, e_norm_mix[j]) @ e_w_in[j]
            b_gate, c_gate, u, q, k, v = jnp.split(z, splits_even, axis=-1)
            a_out = short_conv_mixer(b_gate, c_gate, u, e_conv_w[j])
            b_out = sliding_window_attention(
                q.reshape(bsz, seq, SWA_HEADS, SWA_HEAD_DIM),
                k.reshape(bsz, seq, SWA_KV_HEADS, SWA_HEAD_DIM),
                v.reshape(bsz, seq, SWA_KV_HEADS, SWA_HEAD_DIM),
                e_sinks[j], rel_bias)
            h = h + jnp.concatenate([a_out, b_out], axis=-1) @ e_w_out[j]
            h = h + swiglu(rms_norm(h, e_norm_ffn[j]), e_ffn_gate[j], e_ffn_up[j], e_ffn_down[j])
            h = per_layer_embedding(h, p[i], e_norm_ple[j], e_ple_gate[j], e_ple_proj[j])
        else:
            qkv = rms_norm(h, o_norm_mix[j]) @ o_w_qkv[j]
            q, k, v = jnp.split(qkv, 3, axis=-1)
            shp = (bsz, seq, MOBA_HEADS, MOBA_HEAD_DIM)
            c_out = moba_attention(q.reshape(shp), k.reshape(shp), v.reshape(shp), rel_bias)
            h = h + c_out @ o_w_o[j]
            h = h + moe_swiglu(rms_norm(h, o_norm_ffn[j]), o_router[j],
                               o_exp_gate[j], o_exp_up[j], o_exp_down[j])
            h = per_layer_embedding(h, p[i], o_norm_ple[j], o_ple_gate[j], o_ple_proj[j])
    return rms_norm(h, final_norm)
```

```python
import functools
import math

import jax
import jax.numpy as jnp
import numpy as np
from jax import lax
from jax.experimental import pallas as pl
from jax.experimental.pallas import tpu as pltpu

F32 = jnp.float32
BF16 = jnp.bfloat16

EPS = 1e-6
N_HEADS = 8
N_BUCKETS = 32
MAX_DISTANCE = 128

SC_WIDTH = 512
CONV_WIDTH = 3
SWA_KV_HEADS = 2
SWA_HEAD_DIM = 64
SWA_BLOCK = 128
SWA_Q_WIDTH = N_HEADS * SWA_HEAD_DIM
SWA_KV_WIDTH = SWA_KV_HEADS * SWA_HEAD_DIM

MOBA_HEAD_DIM = 128
MOBA_BLOCK = 256
MOBA_TOPK = 3

N_EXPERTS = 8

VMEM_LIMIT_BYTES = 56 * 1024 * 1024
NEG_INF = float("-inf")


def _params(*semantics):
    return pltpu.CompilerParams(dimension_semantics=semantics,
                                vmem_limit_bytes=VMEM_LIMIT_BYTES)


def _resident(shape):
    zeros = (0,) * len(shape)
    return pl.BlockSpec(shape, lambda *_: zeros, pipeline_mode=pl.Buffered(1))


def _rms(x, g):
    return x * lax.rsqrt(jnp.mean(x * x, axis=-1, keepdims=True) + EPS) * g


def _dot(a, b):
    return jnp.dot(a, b, preferred_element_type=F32)


def _dot_nt(a, b):
    return lax.dot_general(a, b, (((1,), (1,)), ((), ())), preferred_element_type=F32)


def _t5_bucket_np(dist):
    d = np.maximum(dist, 0).astype(np.int32)
    max_exact = N_BUCKETS // 2
    scaled = (np.log(np.maximum(d, max_exact).astype(np.float32) / np.float32(max_exact))
              / np.float32(math.log(MAX_DISTANCE / max_exact)))
    large = np.minimum(max_exact + (scaled * (N_BUCKETS - max_exact)).astype(np.int32),
                       N_BUCKETS - 1)
    return np.where(d < max_exact, d, large).astype(np.int32)


def _bias_table_kernel(rb_ref, bkt_ref, out_ref):
    h = pl.program_id(0)
    bkt = bkt_ref[...]
    acc = jnp.zeros(bkt.shape, F32)
    for b in range(N_BUCKETS):
        acc = jnp.where(bkt == b, rb_ref[b, h], acc)
    out_ref[...] = acc


def _bias_table(rel_bias, bucket):
    rows, cols = bucket.shape
    return pl.pallas_call(
        _bias_table_kernel,
        grid=(N_HEADS,),
        in_specs=[pl.BlockSpec(memory_space=pltpu.SMEM),
                  pl.BlockSpec((rows, cols), lambda h: (0, 0))],
        out_specs=pl.BlockSpec((None, rows, cols), lambda h: (h, 0, 0)),
        out_shape=jax.ShapeDtypeStruct((N_HEADS, rows, cols), F32),
        compiler_params=_params("arbitrary"),
        name="bias_table",
    )(rel_bias, jnp.asarray(bucket))


def _even_proj_kernel(x_ref, g_ref, w_ref, cw_ref, a_ref, qkv_ref, cu_scr, *, tm, tiles_per_seq):
    i = pl.program_id(0)
    xn = _rms(x_ref[...], g_ref[...]).astype(BF16)
    b_gate = _dot(xn, w_ref[:, 0:SC_WIDTH])
    c_gate = _dot(xn, w_ref[:, SC_WIDTH:2 * SC_WIDTH])
    u = _dot(xn, w_ref[:, 2 * SC_WIDTH:3 * SC_WIDTH])
    cu = c_gate * u

    @pl.when(i % tiles_per_seq == 0)
    def _():
        cu_scr[0:8, :] = jnp.zeros((8, SC_WIDTH), F32)

    @pl.when(i % tiles_per_seq != 0)
    def _():
        cu_scr[0:8, :] = cu_scr[tm:tm + 8, :]

    cu_scr[8:8 + tm, :] = cu
    y = (cw_ref[0:1, :] * cu_scr[6:6 + tm, :] + cw_ref[1:2, :] * cu_scr[7:7 + tm, :]
         + cw_ref[2:3, :] * cu)
    a_ref[...] = (b_gate * y).astype(BF16)
    qkv_ref[...] = _dot(xn, w_ref[:, 3 * SC_WIDTH:]).astype(BF16)


def _even_proj(x, g, w, conv_w, seq, tm=512):
    n, d = x.shape
    d_in = w.shape[1]
    d_qkv = d_in - 3 * SC_WIDTH
    return pl.pallas_call(
        functools.partial(_even_proj_kernel, tm=tm, tiles_per_seq=seq // tm),
        grid=(n // tm,),
        in_specs=[pl.BlockSpec((tm, d), lambda i: (i, 0)),
                  _resident((1, d)), _resident((d, d_in)), _resident((CONV_WIDTH, SC_WIDTH))],
        out_specs=[pl.BlockSpec((tm, SC_WIDTH), lambda i: (i, 0)),
                   pl.BlockSpec((tm, d_qkv), lambda i: (i, 0))],
        out_shape=[jax.ShapeDtypeStruct((n, SC_WIDTH), BF16),
                   jax.ShapeDtypeStruct((n, d_qkv), BF16)],
        scratch_shapes=[pltpu.VMEM((tm + 8, SC_WIDTH), F32)],
        compiler_params=_params("arbitrary"),
        name="even_proj_conv",
    )(x, g, w, conv_w)


def _swa_kernel(sink_ref, cur_ref, prev_ref, tab_ref, out_ref):
    i = pl.program_id(1)
    blk = SWA_BLOCK
    hd = SWA_HEAD_DIM
    grp = N_HEADS // SWA_KV_HEADS
    qi = lax.broadcasted_iota(jnp.int32, (blk, 2 * blk), 0)
    kj = lax.broadcasted_iota(jnp.int32, (blk, 2 * blk), 1)
    diff = qi - kj + blk
    valid = (diff >= 0) & (diff < blk) & ((kj >= blk) | (i > 0))
    scale = hd ** -0.5
    outs = []
    for g in range(SWA_KV_HEADS):
        k_cat = jnp.concatenate([prev_ref[:, g * hd:(g + 1) * hd],
                                 cur_ref[:, SWA_Q_WIDTH + g * hd:SWA_Q_WIDTH + (g + 1) * hd]], axis=0)
        v_cat = jnp.concatenate(
            [prev_ref[:, SWA_KV_WIDTH + g * hd:SWA_KV_WIDTH + (g + 1) * hd],
             cur_ref[:, SWA_Q_WIDTH + SWA_KV_WIDTH + g * hd:SWA_Q_WIDTH + SWA_KV_WIDTH + (g + 1) * hd]],
            axis=0)
        for hh in range(grp):
            h = g * grp + hh
            q = cur_ref[:, h * hd:(h + 1) * hd]
            s = _dot_nt(q, k_cat) * scale + tab_ref[h]
            s = jnp.where(valid, s, NEG_INF)
            sink = sink_ref[h]
            m = jnp.maximum(jnp.max(s, axis=-1, keepdims=True), sink)
            p = jnp.exp(s - m)
            denom = jnp.sum(p, axis=-1, keepdims=True) + jnp.exp(sink - m)
            o = _dot(p.astype(BF16), v_cat)
            outs.append(o / denom)
    out_ref[...] = jnp.concatenate(outs, axis=-1).astype(BF16)


def _swa(qkv, sinks, tab, bsz, seq):
    n = qkv.shape[0]
    nb = seq // SWA_BLOCK
    kv_col_block = SWA_Q_WIDTH // (2 * SWA_KV_WIDTH)
    return pl.pallas_call(
        _swa_kernel,
        grid=(bsz, nb),
        in_specs=[pl.BlockSpec(memory_space=pltpu.SMEM),
                  pl.BlockSpec((SWA_BLOCK, qkv.shape[1]), lambda b, i: (b * nb + i, 0)),
                  pl.BlockSpec((SWA_BLOCK, 2 * SWA_KV_WIDTH),
                               lambda b, i: (b * nb + jnp.maximum(i - 1, 0), kv_col_block)),
                  _resident((N_HEADS, SWA_BLOCK, 2 * SWA_BLOCK))],
        out_specs=pl.BlockSpec((SWA_BLOCK, SWA_Q_WIDTH), lambda b, i: (b * nb + i, 0)),
        out_shape=jax.ShapeDtypeStruct((n, SWA_Q_WIDTH), BF16),
        compiler_params=_params("arbitrary", "arbitrary"),
        name="swa",
    )(sinks, qkv, qkv, tab)


def _ple(h, p, g, w_gate, w_proj):
    gate = jax.nn.sigmoid(_dot(_rms(h, g).astype(BF16), w_gate))
    return h + gate * _dot(p.astype(BF16), w_proj)


def _even_ffn_kernel(h_ref, a_ref, b_ref, p_ref, wo_ref, gf_ref, wg_ref, wu_ref, wd_ref,
                     gp_ref, wpg_ref, wpp_ref, out_ref, *, tf):
    h1 = (h_ref[...] + _dot(a_ref[...], wo_ref[0:SC_WIDTH, :])
          + _dot(b_ref[...], wo_ref[SC_WIDTH:, :]))
    xn = _rms(h1, gf_ref[...]).astype(BF16)
    d_ff = wg_ref.shape[1]
    acc = jnp.zeros(h1.shape, F32)
    for c in range(d_ff // tf):
        gate = _dot(xn, wg_ref[:, c * tf:(c + 1) * tf])
        up = _dot(xn, wu_ref[:, c * tf:(c + 1) * tf])
        hid = (gate * jax.nn.sigmoid(gate) * up).astype(BF16)
        acc = acc + _dot(hid, wd_ref[c * tf:(c + 1) * tf, :])
    h2 = h1 + acc
    out_ref[...] = _ple(h2, p_ref[...], gp_ref[...], wpg_ref[...], wpp_ref[...])


def _even_ffn(h, a, b, p, wo, gf, wg, wu, wd, gp, wpg, wpp, tm=512, tf=256):
    n, d = h.shape
    row = lambda w: pl.BlockSpec((tm, w), lambda i: (i, 0))
    return pl.pallas_call(
        functools.partial(_even_ffn_kernel, tf=tf),
        grid=(n // tm,),
        in_specs=[row(d), row(a.shape[1]), row(b.shape[1]), row(p.shape[1]),
                  _resident(wo.shape), _resident(gf.shape), _resident(wg.shape),
                  _resident(wu.shape), _resident(wd.shape), _resident(gp.shape),
                  _resident(wpg.shape), _resident(wpp.shape)],
        out_specs=row(d),
        out_shape=jax.ShapeDtypeStruct((n, d), F32),
        compiler_params=_params("arbitrary"),
        name="even_out_ffn_ple",
    )(h, a, b, p, wo, gf, wg, wu, wd, gp, wpg, wpp)


def _odd_proj_kernel(h_ref, g_ref, w_ref, qkv_ref, kmean_ref, *, tm):
    d = h_ref.shape[1]
    xn = _rms(h_ref[...], g_ref[...]).astype(BF16)
    qkv_ref[:, 0:d] = _dot(xn, w_ref[:, 0:d]).astype(BF16)
    k = _dot(xn, w_ref[:, d:2 * d])
    qkv_ref[:, d:2 * d] = k.astype(BF16)
    qkv_ref[:, 2 * d:3 * d] = _dot(xn, w_ref[:, 2 * d:3 * d]).astype(BF16)
    for r in range(tm // MOBA_BLOCK):
        kmean_ref[r] = jnp.mean(k[r * MOBA_BLOCK:(r + 1) * MOBA_BLOCK, :], axis=0, keepdims=True)


def _odd_proj(h, g, w, tm=512):
    n, d = h.shape
    bpt = tm // MOBA_BLOCK
    return pl.pallas_call(
        functools.partial(_odd_proj_kernel, tm=tm),
        grid=(n // tm,),
        in_specs=[pl.BlockSpec((tm, d), lambda i: (i, 0)), _resident(g.shape), _resident(w.shape)],
        out_specs=[pl.BlockSpec((tm, 3 * d), lambda i: (i, 0)),
                   pl.BlockSpec((bpt, 1, d), lambda i: (i, 0, 0))],
        out_shape=[jax.ShapeDtypeStruct((n, 3 * d), BF16),
                   jax.ShapeDtypeStruct((n // MOBA_BLOCK, 1, d), F32)],
        compiler_params=_params("arbitrary"),
        name="odd_qkv",
    )(h, g, w)


def _moba_kernel(rb_ref, q_ref, k_ref, v_ref, km_ref, tab_ref, out_ref, *, nblk):
    h = pl.program_id(1)
    qi = pl.program_id(2)
    bs = MOBA_BLOCK
    scale = MOBA_HEAD_DIM ** -0.5
    q = q_ref[...]

    gate = lax.dot_general(q.astype(F32), km_ref[...], (((1,), (1,)), ((), ())),
                           preferred_element_type=F32, precision=lax.Precision.HIGHEST)
    blk = lax.broadcasted_iota(jnp.int32, gate.shape, 1)
    past = blk < qi
    g = jnp.where(past, gate, NEG_INF)
    sel = jnp.zeros(gate.shape, jnp.bool_)
    for _ in range(MOBA_TOPK):
        top = jnp.max(g, axis=-1, keepdims=True)
        idx = jnp.min(jnp.where(g == top, blk, nblk), axis=-1, keepdims=True)
        hit = blk == idx
        sel = sel | hit
        g = jnp.where(hit, NEG_INF, g)
    sel_bias = jnp.where(sel & past, 0.0, NEG_INF)

    row = lax.broadcasted_iota(jnp.int32, (bs, bs), 0)
    col = lax.broadcasted_iota(jnp.int32, (bs, bs), 1)
    own = pl.multiple_of(qi * bs, bs)
    s = _dot_nt(q, k_ref[pl.ds(own, bs), :]) * scale + tab_ref[:, bs:2 * bs]
    s = jnp.where(row >= col, s, NEG_INF)
    m0 = jnp.max(s, axis=-1, keepdims=True)
    p = jnp.exp(s - m0)
    l0 = jnp.sum(p, axis=-1, keepdims=True)
    acc0 = _dot(p.astype(BF16), v_ref[pl.ds(own, bs), :])
    far_bias = rb_ref[N_BUCKETS - 1, h]

    def body(j, carry):
        m, l, acc = carry
        start = pl.multiple_of(j * bs, bs)
        s = _dot_nt(q, k_ref[pl.ds(start, bs), :]) * scale
        bias = jnp.where(j == qi - 1, tab_ref[:, 0:bs], far_bias)
        row_sel = jnp.sum(jnp.where(blk == j, sel_bias, 0.0), axis=-1, keepdims=True)
        s = s + bias + row_sel
        m_new = jnp.maximum(m, jnp.max(s, axis=-1, keepdims=True))
        alpha = jnp.exp(m - m_new)
        p = jnp.exp(s - m_new)
        l = alpha * l + jnp.sum(p, axis=-1, keepdims=True)
        acc = alpha * acc + _dot(p.astype(BF16), v_ref[pl.ds(start, bs), :])
        return m_new, l, acc

    _, l, acc = lax.fori_loop(0, qi, body, (m0, l0, acc0))
    out_ref[...] = (acc / l).astype(BF16)


def _moba(qkv, kmean, tab, rel_bias, bsz, seq):
    n = qkv.shape[0]
    d = qkv.shape[1] // 3
    nblk = seq // MOBA_BLOCK
    hd = MOBA_HEAD_DIM
    return pl.pallas_call(
        functools.partial(_moba_kernel, nblk=nblk),
        grid=(bsz, N_HEADS, nblk),
        in_specs=[pl.BlockSpec(memory_space=pltpu.SMEM),
                  pl.BlockSpec((MOBA_BLOCK, hd), lambda b, h, i: (b * nblk + i, h)),
                  pl.BlockSpec((seq, hd), lambda b, h, i: (b, N_HEADS + h)),
                  pl.BlockSpec((seq, hd), lambda b, h, i: (b, 2 * N_HEADS + h)),
                  pl.BlockSpec((None, nblk, hd), lambda b, h, i: (b, 0, h)),
                  pl.BlockSpec((None, MOBA_BLOCK, 2 * MOBA_BLOCK), lambda b, h, i: (h, 0, 0))],
        out_specs=pl.BlockSpec((MOBA_BLOCK, hd), lambda b, h, i: (b * nblk + i, h)),
        out_shape=jax.ShapeDtypeStruct((n, d), BF16),
        compiler_params=_params("arbitrary", "arbitrary", "arbitrary"),
        name="moba",
    )(rel_bias, qkv, qkv, qkv, kmean.reshape(bsz, nblk, d), tab)


def _odd_router_kernel(h_ref, c_ref, wo_ref, g_ref, wr_ref, h_out_ref, xn_ref, gates_ref):
    h1 = h_ref[...] + _dot(c_ref[...], wo_ref[...])
    h_out_ref[...] = h1
    xn = _rms(h1, g_ref[...])
    xn_ref[...] = xn.astype(BF16)
    logits = jnp.dot(xn, wr_ref[...], preferred_element_type=F32, precision=lax.Precision.HIGHEST)
    lane = lax.broadcasted_iota(jnp.int32, logits.shape, 1)
    v1 = jnp.max(logits, axis=-1, keepdims=True)
    i1 = jnp.min(jnp.where(logits == v1, lane, N_EXPERTS), axis=-1, keepdims=True)
    rest = jnp.where(lane == i1, NEG_INF, logits)
    v2 = jnp.max(rest, axis=-1, keepdims=True)
    i2 = jnp.min(jnp.where(rest == v2, lane, N_EXPERTS), axis=-1, keepdims=True)
    e2 = jnp.exp(v2 - v1)
    w1 = 1.0 / (1.0 + e2)
    w2 = e2 / (1.0 + e2)
    gates_ref[...] = jnp.where(lane == i1, w1, 0.0) + jnp.where(lane == i2, w2, 0.0)


def _odd_router(h, c, wo, g, wr, tm=512):
    n, d = h.shape
    row = lambda w: pl.BlockSpec((tm, w), lambda i: (i, 0))
    return pl.pallas_call(
        _odd_router_kernel,
        grid=(n // tm,),
        in_specs=[row(d), row(d), _resident(wo.shape), _resident(g.shape), _resident(wr.shape)],
        out_specs=[row(d), row(d), row(N_EXPERTS)],
        out_shape=[jax.ShapeDtypeStruct((n, d), F32), jax.ShapeDtypeStruct((n, d), BF16),
                   jax.ShapeDtypeStruct((n, N_EXPERTS), F32)],
        compiler_params=_params("arbitrary"),
        name="odd_out_router",
    )(h, c, wo, g, wr)


def _moe_dense_kernel(xn_ref, gates_ref, h_ref, wg_ref, wu_ref, wd_ref, out_ref, acc_ref):
    e = pl.program_id(1)
    f = pl.program_id(2)

    @pl.when((e == 0) & (f == 0))
    def _():
        acc_ref[...] = jnp.zeros(acc_ref.shape, F32)

    xn = xn_ref[...]
    gate = _dot(xn, wg_ref[...])
    up = _dot(xn, wu_ref[...])
    hid = (gate * jax.nn.sigmoid(gate) * up).astype(BF16)
    lane = lax.broadcasted_iota(jnp.int32, gates_ref.shape, 1)
    w = jnp.sum(jnp.where(lane == e, gates_ref[...], 0.0), axis=-1, keepdims=True)
    acc_ref[...] += w * _dot(hid, wd_ref[...])

    @pl.when((e == pl.num_programs(1) - 1) & (f == pl.num_programs(2) - 1))
    def _():
        out_ref[...] = h_ref[...] + acc_ref[...]


def _moe_dense(xn, gates, h, wg, wu, wd, tm=1024, tf=512):
    n, d = h.shape
    n_e, _, d_ff = wg.shape
    row = lambda w: pl.BlockSpec((tm, w), lambda i, e, f: (i, 0))
    return pl.pallas_call(
        _moe_dense_kernel,
        grid=(n // tm, n_e, d_ff // tf),
        in_specs=[row(d), row(N_EXPERTS), row(d),
                  pl.BlockSpec((None, d, tf), lambda i, e, f: (e, 0, f)),
                  pl.BlockSpec((None, d, tf), lambda i, e, f: (e, 0, f)),
                  pl.BlockSpec((None, tf, d), lambda i, e, f: (e, f, 0))],
        out_specs=row(d),
        out_shape=jax.ShapeDtypeStruct((n, d), F32),
        scratch_shapes=[pltpu.VMEM((tm, d), F32)],
        compiler_params=_params("arbitrary", "arbitrary", "arbitrary"),
        name="moe_experts",
    )(xn, gates, h, wg, wu, wd)


def _ple_final_kernel(h_ref, p_ref, gp_ref, wpg_ref, wpp_ref, gfin_ref, out_ref):
    h = _ple(h_ref[...], p_ref[...], gp_ref[...], wpg_ref[...], wpp_ref[...])
    out_ref[...] = _rms(h, gfin_ref[...])


def _ple_final(h, p, gp, wpg, wpp, gfin, tm=512):
    n, d = h.shape
    row = lambda w: pl.BlockSpec((tm, w), lambda i: (i, 0))
    return pl.pallas_call(
        _ple_final_kernel,
        grid=(n // tm,),
        in_specs=[row(d), row(p.shape[1]), _resident(gp.shape), _resident(wpg.shape),
                  _resident(wpp.shape), _resident(gfin.shape)],
        out_specs=row(d),
        out_shape=jax.ShapeDtypeStruct((n, d), F32),
        compiler_params=_params("arbitrary"),
        name="ple_final_norm",
    )(h, p, gp, wpg, wpp, gfin)


def kernel(x, p, rel_bias, final_norm, e_norm_mix, e_w_in, e_conv_w, e_sinks, e_w_out, e_norm_ffn, e_ffn_gate, e_ffn_up, e_ffn_down, e_norm_ple, e_ple_gate, e_ple_proj, o_norm_mix, o_w_qkv, o_w_o, o_norm_ffn, o_router, o_exp_gate, o_exp_up, o_exp_down, o_norm_ple, o_ple_gate, o_ple_proj):
    bsz, seq, d = x.shape
    n = bsz * seq
    bf = lambda w: w.astype(BF16)
    h = x.reshape(n, d)
    p2 = p.reshape(p.shape[0], n, p.shape[-1])

    qi = np.arange(SWA_BLOCK)[:, None]
    kj = np.arange(2 * SWA_BLOCK)[None, :]
    swa_tab = _bias_table(rel_bias, _t5_bucket_np(qi - kj + SWA_BLOCK))
    qi = np.arange(MOBA_BLOCK)[:, None]
    kj = np.arange(2 * MOBA_BLOCK)[None, :]
    moba_tab = _bias_table(rel_bias, _t5_bucket_np(qi - kj + MOBA_BLOCK))

    a_out, qkv0 = _even_proj(h, e_norm_mix[0:1], bf(e_w_in[0]), e_conv_w[0], seq)
    b_out = _swa(qkv0, e_sinks[0], swa_tab, bsz, seq)
    h = _even_ffn(h, a_out, b_out, p2[0], bf(e_w_out[0]), e_norm_ffn[0:1],
                  bf(e_ffn_gate[0]), bf(e_ffn_up[0]), bf(e_ffn_down[0]),
                  e_norm_ple[0:1], bf(e_ple_gate[0]), bf(e_ple_proj[0]))

    qkv1, kmean = _odd_proj(h, o_norm_mix[0:1], bf(o_w_qkv[0]))
    c_out = _moba(qkv1, kmean, moba_tab, rel_bias, bsz, seq)
    h, xn, gates = _odd_router(h, c_out, bf(o_w_o[0]), o_norm_ffn[0:1], o_router[0])
    h = _moe_dense(xn, gates, h, bf(o_exp_gate[0]), bf(o_exp_up[0]), bf(o_exp_down[0]))
    out = _ple_final(h, p2[1], o_norm_ple[0:1], bf(o_ple_gate[0]), bf(o_ple_proj[0]),
                     final_norm.reshape(1, d))
    return out.reshape(bsz, seq, d)
```

```python
import functools
import math

import jax
import jax.numpy as jnp
import numpy as np
from jax import lax
from jax.experimental import pallas as pl
from jax.experimental.pallas import tpu as pltpu

F32 = jnp.float32
BF16 = jnp.bfloat16

EPS = 1e-6
N_HEADS = 8
N_BUCKETS = 32
MAX_DISTANCE = 128

SC_WIDTH = 512
CONV_WIDTH = 3
SWA_KV_HEADS = 2
SWA_HEAD_DIM = 64
SWA_BLOCK = 128
SWA_Q_WIDTH = N_HEADS * SWA_HEAD_DIM
SWA_KV_WIDTH = SWA_KV_HEADS * SWA_HEAD_DIM

MOBA_HEAD_DIM = 128
MOBA_BLOCK = 256
MOBA_TOPK = 3
MOBA_FAR_GROUP = 4

N_EXPERTS = 8

VMEM_LIMIT_BYTES = 56 * 1024 * 1024
NEG_INF = float("-inf")
LOG2E = math.log2(math.e)


def _params(*semantics):
    return pltpu.CompilerParams(dimension_semantics=semantics,
                                vmem_limit_bytes=VMEM_LIMIT_BYTES)


def _resident(shape):
    zeros = (0,) * len(shape)
    return pl.BlockSpec(shape, lambda *_: zeros, pipeline_mode=pl.Buffered(1))


def _rms(x, g):
    return x * lax.rsqrt(jnp.mean(x * x, axis=-1, keepdims=True) + EPS) * g


def _dot(a, b):
    return jnp.dot(a, b, preferred_element_type=F32)


def _dot_nt(a, b):
    return lax.dot_general(a, b, (((1,), (1,)), ((), ())), preferred_element_type=F32)


def _t5_bucket_np(dist):
    d = np.maximum(dist, 0).astype(np.int32)
    max_exact = N_BUCKETS // 2
    scaled = (np.log(np.maximum(d, max_exact).astype(np.float32) / np.float32(max_exact))
              / np.float32(math.log(MAX_DISTANCE / max_exact)))
    large = np.minimum(max_exact + (scaled * (N_BUCKETS - max_exact)).astype(np.int32),
                       N_BUCKETS - 1)
    return np.where(d < max_exact, d, large).astype(np.int32)


def _bias_table_kernel(rb_ref, bkt_ref, out_ref, *, mult):
    h = pl.program_id(0)
    bkt = bkt_ref[...]
    acc = jnp.full(bkt.shape, NEG_INF, F32)
    for b in range(N_BUCKETS):
        acc = jnp.where(bkt == b, rb_ref[b, h], acc)
    out_ref[...] = acc * mult


def _bias_table(rel_bias, bucket, mult=1.0):
    rows, cols = bucket.shape
    return pl.pallas_call(
        functools.partial(_bias_table_kernel, mult=mult),
        grid=(N_HEADS,),
        in_specs=[pl.BlockSpec(memory_space=pltpu.SMEM),
                  pl.BlockSpec((rows, cols), lambda h: (0, 0))],
        out_specs=pl.BlockSpec((None, rows, cols), lambda h: (h, 0, 0)),
        out_shape=jax.ShapeDtypeStruct((N_HEADS, rows, cols), F32),
        compiler_params=_params("arbitrary"),
        name="bias_table",
    )(rel_bias, jnp.asarray(bucket))


def _even_proj_kernel(x_ref, g_ref, w_ref, cw_ref, a_ref, qkv_ref, cu_scr, *, tm, tiles_per_seq):
    i = pl.program_id(0)
    xn = _rms(x_ref[...], g_ref[...]).astype(BF16)
    b_gate = _dot(xn, w_ref[:, 0:SC_WIDTH])
    c_gate = _dot(xn, w_ref[:, SC_WIDTH:2 * SC_WIDTH])
    u = _dot(xn, w_ref[:, 2 * SC_WIDTH:3 * SC_WIDTH])
    cu = c_gate * u

    @pl.when(i % tiles_per_seq == 0)
    def _():
        cu_scr[0:8, :] = jnp.zeros((8, SC_WIDTH), F32)

    @pl.when(i % tiles_per_seq != 0)
    def _():
        cu_scr[0:8, :] = cu_scr[tm:tm + 8, :]

    cu_scr[8:8 + tm, :] = cu
    y = (cw_ref[0:1, :] * cu_scr[6:6 + tm, :] + cw_ref[1:2, :] * cu_scr[7:7 + tm, :]
         + cw_ref[2:3, :] * cu)
    a_ref[...] = (b_gate * y).astype(BF16)
    qkv_ref[...] = _dot(xn, w_ref[:, 3 * SC_WIDTH:]).astype(BF16)


def _even_proj(x, g, w, conv_w, seq, tm=512):
    n, d = x.shape
    d_in = w.shape[1]
    d_qkv = d_in - 3 * SC_WIDTH
    return pl.pallas_call(
        functools.partial(_even_proj_kernel, tm=tm, tiles_per_seq=seq // tm),
        grid=(n // tm,),
        in_specs=[pl.BlockSpec((tm, d), lambda i: (i, 0)),
                  _resident((1, d)), _resident((d, d_in)), _resident((CONV_WIDTH, SC_WIDTH))],
        out_specs=[pl.BlockSpec((tm, SC_WIDTH), lambda i: (i, 0)),
                   pl.BlockSpec((tm, d_qkv), lambda i: (i, 0))],
        out_shape=[jax.ShapeDtypeStruct((n, SC_WIDTH), BF16),
                   jax.ShapeDtypeStruct((n, d_qkv), BF16)],
        scratch_shapes=[pltpu.VMEM((tm + 8, SC_WIDTH), F32)],
        compiler_params=_params("arbitrary"),
        name="even_proj_conv",
    )(x, g, w, conv_w)


def _swa_kernel(sink_ref, cur_ref, prev_ref, tab_ref, out_ref):
    i = pl.program_id(1)
    blk = SWA_BLOCK
    hd = SWA_HEAD_DIM
    grp = N_HEADS // SWA_KV_HEADS
    qi = lax.broadcasted_iota(jnp.int32, (blk, 2 * blk), 0)
    kj = lax.broadcasted_iota(jnp.int32, (blk, 2 * blk), 1)
    diff = qi - kj + blk
    valid = (diff >= 0) & (diff < blk) & ((kj >= blk) | (i > 0))
    scale = hd ** -0.5
    outs = []
    for g in range(SWA_KV_HEADS):
        k_cat = jnp.concatenate([prev_ref[:, g * hd:(g + 1) * hd],
                                 cur_ref[:, SWA_Q_WIDTH + g * hd:SWA_Q_WIDTH + (g + 1) * hd]], axis=0)
        v_cat = jnp.concatenate(
            [prev_ref[:, SWA_KV_WIDTH + g * hd:SWA_KV_WIDTH + (g + 1) * hd],
             cur_ref[:, SWA_Q_WIDTH + SWA_KV_WIDTH + g * hd:SWA_Q_WIDTH + SWA_KV_WIDTH + (g + 1) * hd]],
            axis=0)
        for hh in range(grp):
            h = g * grp + hh
            q = cur_ref[:, h * hd:(h + 1) * hd]
            s = _dot_nt(q, k_cat) * scale + tab_ref[h]
            s = jnp.where(valid, s, NEG_INF)
            sink = sink_ref[h]
            m = jnp.maximum(jnp.max(s, axis=-1, keepdims=True), sink)
            p = jnp.exp(s - m)
            denom = jnp.sum(p, axis=-1, keepdims=True) + jnp.exp(sink - m)
            o = _dot(p.astype(BF16), v_cat)
            outs.append(o / denom)
    out_ref[...] = jnp.concatenate(outs, axis=-1).astype(BF16)


def _swa(qkv, sinks, tab, bsz, seq):
    n = qkv.shape[0]
    nb = seq // SWA_BLOCK
    kv_col_block = SWA_Q_WIDTH // (2 * SWA_KV_WIDTH)
    return pl.pallas_call(
        _swa_kernel,
        grid=(bsz, nb),
        in_specs=[pl.BlockSpec(memory_space=pltpu.SMEM),
                  pl.BlockSpec((SWA_BLOCK, qkv.shape[1]), lambda b, i: (b * nb + i, 0)),
                  pl.BlockSpec((SWA_BLOCK, 2 * SWA_KV_WIDTH),
                               lambda b, i: (b * nb + jnp.maximum(i - 1, 0), kv_col_block)),
                  _resident((N_HEADS, SWA_BLOCK, 2 * SWA_BLOCK))],
        out_specs=pl.BlockSpec((SWA_BLOCK, SWA_Q_WIDTH), lambda b, i: (b * nb + i, 0)),
        out_shape=jax.ShapeDtypeStruct((n, SWA_Q_WIDTH), BF16),
        compiler_params=_params("arbitrary", "arbitrary"),
        name="swa",
    )(sinks, qkv, qkv, tab)


def _ple(h, p, g, w_gate, w_proj):
    gate = jax.nn.sigmoid(_dot(_rms(h, g).astype(BF16), w_gate))
    return h + gate * _dot(p.astype(BF16), w_proj)


def _even_ffn_kernel(h_ref, a_ref, b_ref, p_ref, wo_ref, gf_ref, wg_ref, wu_ref, wd_ref,
                     gp_ref, wpg_ref, wpp_ref, out_ref, *, tf):
    h1 = (h_ref[...] + _dot(a_ref[...], wo_ref[0:SC_WIDTH, :])
          + _dot(b_ref[...], wo_ref[SC_WIDTH:, :]))
    xn = _rms(h1, gf_ref[...]).astype(BF16)
    d_ff = wg_ref.shape[1]
    acc = jnp.zeros(h1.shape, F32)
    for c in range(d_ff // tf):
        gate = _dot(xn, wg_ref[:, c * tf:(c + 1) * tf])
        up = _dot(xn, wu_ref[:, c * tf:(c + 1) * tf])
        hid = (gate * jax.nn.sigmoid(gate) * up).astype(BF16)
        acc = acc + _dot(hid, wd_ref[c * tf:(c + 1) * tf, :])
    h2 = h1 + acc
    out_ref[...] = _ple(h2, p_ref[...], gp_ref[...], wpg_ref[...], wpp_ref[...])


def _even_ffn(h, a, b, p, wo, gf, wg, wu, wd, gp, wpg, wpp, tm=512, tf=256):
    n, d = h.shape
    row = lambda w: pl.BlockSpec((tm, w), lambda i: (i, 0))
    return pl.pallas_call(
        functools.partial(_even_ffn_kernel, tf=tf),
        grid=(n // tm,),
        in_specs=[row(d), row(a.shape[1]), row(b.shape[1]), row(p.shape[1]),
                  _resident(wo.shape), _resident(gf.shape), _resident(wg.shape),
                  _resident(wu.shape), _resident(wd.shape), _resident(gp.shape),
                  _resident(wpg.shape), _resident(wpp.shape)],
        out_specs=row(d),
        out_shape=jax.ShapeDtypeStruct((n, d), F32),
        compiler_params=_params("arbitrary"),
        name="even_out_ffn_ple",
    )(h, a, b, p, wo, gf, wg, wu, wd, gp, wpg, wpp)


def _odd_proj_kernel(h_ref, g_ref, wq_ref, wk_ref, wvt_ref, q_ref, k_ref, vt_ref, kmean_ref, *, tm):
    xn = _rms(h_ref[...], g_ref[...]).astype(BF16)
    q_ref[...] = (_dot(xn, wq_ref[...]) * (MOBA_HEAD_DIM ** -0.5 * LOG2E)).astype(BF16)
    k = _dot(xn, wk_ref[...])
    k_ref[...] = k.astype(BF16)
    vt_ref[...] = _dot_nt(wvt_ref[...], xn).astype(BF16)
    for r in range(tm // MOBA_BLOCK):
        kmean_ref[r] = jnp.mean(k[r * MOBA_BLOCK:(r + 1) * MOBA_BLOCK, :], axis=0, keepdims=True)


def _odd_proj(h, g, wq, wk, wvt, tm=512):
    n, d = h.shape
    bpt = tm // MOBA_BLOCK
    row = pl.BlockSpec((tm, d), lambda i: (i, 0))
    return pl.pallas_call(
        functools.partial(_odd_proj_kernel, tm=tm),
        grid=(n // tm,),
        in_specs=[row, _resident(g.shape), _resident(wq.shape), _resident(wk.shape),
                  _resident(wvt.shape)],
        out_specs=[row, row, pl.BlockSpec((d, tm), lambda i: (0, i)),
                   pl.BlockSpec((bpt, 1, d), lambda i: (i, 0, 0))],
        out_shape=[jax.ShapeDtypeStruct((n, d), BF16), jax.ShapeDtypeStruct((n, d), BF16),
                   jax.ShapeDtypeStruct((d, n), BF16),
                   jax.ShapeDtypeStruct((n // MOBA_BLOCK, 1, d), F32)],
        compiler_params=_params("arbitrary"),
        name="odd_qkv",
    )(h, g, wq, wk, wvt)


def _moba_kernel(rb_ref, q_ref, k_ref, vt_ref, km_ref, tab_ref, out_ref,
                 sel_scr, adj_scr, m_scr, l_scr, acc_scr, *, nblk, hg):
    head0 = pl.program_id(1) * hg
    qi = pl.program_id(2)
    bs = MOBA_BLOCK
    hd = MOBA_HEAD_DIM
    heads = [slice(hh * hd, (hh + 1) * hd) for hh in range(hg)]

    blk = lax.broadcasted_iota(jnp.int32, (nblk, bs), 0)
    past = blk < qi
    for hh in range(hg):
        gate = lax.dot_general(km_ref[:, heads[hh]], q_ref[:, heads[hh]].astype(F32),
                               (((1,), (1,)), ((), ())), preferred_element_type=F32,
                               precision=lax.Precision.HIGHEST)
        g = jnp.where(past, gate, NEG_INF)
        sel = jnp.zeros(gate.shape, jnp.bool_)
        for _ in range(MOBA_TOPK):
            top = jnp.max(g, axis=0, keepdims=True)
            idx = jnp.min(jnp.where(g == top, blk, nblk), axis=0, keepdims=True)
            hit = blk == idx
            sel = sel | hit
            g = jnp.where(hit, NEG_INF, g)
        far_bias = rb_ref[N_BUCKETS - 1, head0 + hh] * LOG2E
        sel_scr[hh] = jnp.where(sel & (blk < qi - 1), far_bias, NEG_INF)
        adj_scr[hh] = jnp.max(jnp.where(sel & (blk == qi - 1), 0.0, NEG_INF), axis=0, keepdims=True)

    def visit(hh, start, tables, query_biases, first):
        nb = len(tables)
        s = _dot_nt(k_ref[pl.ds(start, nb * bs), heads[hh]], q_ref[:, heads[hh]])
        parts = []
        for r in range(nb):
            s_r = s[r * bs:(r + 1) * bs]
            if tables[r] is not None:
                s_r = s_r + tables[r]
            if query_biases[r] is not None:
                s_r = s_r + query_biases[r]
            parts.append(s_r)
        s = jnp.concatenate(parts, axis=0)
        s_max = jnp.max(s, axis=0, keepdims=True)
        m_new = s_max if first else jnp.maximum(m_scr[hh], s_max)
        p = jnp.exp2(s - m_new)
        l_new = jnp.sum(p, axis=0, keepdims=True)
        acc_new = _dot(vt_ref[heads[hh], pl.ds(start, nb * bs)], p.astype(BF16))
        if not first:
            alpha = jnp.exp2(m_scr[hh] - m_new)
            l_new = alpha * l_scr[hh] + l_new
            acc_new = alpha * acc_scr[hh] + acc_new
        m_scr[hh] = m_new
        l_scr[hh] = l_new
        acc_scr[hh] = acc_new

    @pl.when(qi == 0)
    def _():
        for hh in range(hg):
            visit(hh, 0, [tab_ref[hh, :, bs:2 * bs]], [None], first=True)

    @pl.when(qi > 0)
    def _():
        start = pl.multiple_of((qi - 1) * bs, bs)
        for hh in range(hg):
            visit(hh, start, [tab_ref[hh, :, 0:bs], tab_ref[hh, :, bs:2 * bs]],
                  [adj_scr[hh], None], first=True)

    def far_body(g, carry):
        first_blk = g * MOBA_FAR_GROUP
        start = pl.multiple_of(first_blk * bs, MOBA_FAR_GROUP * bs)
        for hh in range(hg):
            visit(hh, start, [None] * MOBA_FAR_GROUP,
                  [sel_scr[hh, pl.ds(first_blk + r, 1), :] for r in range(MOBA_FAR_GROUP)],
                  first=False)
        return carry

    lax.fori_loop(0, (qi - 1 + MOBA_FAR_GROUP - 1) // MOBA_FAR_GROUP, far_body, 0)

    for hh in range(hg):
        out_ref[:, heads[hh]] = (acc_scr[hh] / l_scr[hh]).T.astype(BF16)


def _moba(q, k, vt, kmean, tab, rel_bias, bsz, seq, hg=4):
    n, d = q.shape
    nblk = seq // MOBA_BLOCK
    assert nblk % MOBA_FAR_GROUP == 0, "far-block groups must not run past the sequence"
    bs = MOBA_BLOCK
    w = hg * MOBA_HEAD_DIM
    return pl.pallas_call(
        functools.partial(_moba_kernel, nblk=nblk, hg=hg),
        grid=(bsz, N_HEADS // hg, nblk),
        in_specs=[pl.BlockSpec(memory_space=pltpu.SMEM),
                  pl.BlockSpec((bs, w), lambda b, g, i: (b * nblk + i, g)),
                  pl.BlockSpec((seq, w), lambda b, g, i: (b, g)),
                  pl.BlockSpec((w, seq), lambda b, g, i: (g, b)),
                  pl.BlockSpec((None, nblk, w), lambda b, g, i: (b, 0, g)),
                  pl.BlockSpec((hg, bs, 2 * bs), lambda b, g, i: (g, 0, 0))],
        out_specs=pl.BlockSpec((bs, w), lambda b, g, i: (b * nblk + i, g)),
        out_shape=jax.ShapeDtypeStruct((n, d), BF16),
        scratch_shapes=[pltpu.VMEM((hg, nblk, bs), F32), pltpu.VMEM((hg, 1, bs), F32),
                        pltpu.VMEM((hg, 1, bs), F32), pltpu.VMEM((hg, 1, bs), F32),
                        pltpu.VMEM((hg, MOBA_HEAD_DIM, bs), F32)],
        compiler_params=_params("arbitrary", "arbitrary", "arbitrary"),
        name="moba",
    )(rel_bias, q, k, vt, kmean.reshape(bsz, nblk, d), tab)


def _odd_router_kernel(h_ref, c_ref, wo_ref, g_ref, wr_ref, h_out_ref, xn_ref, gates_ref):
    h1 = h_ref[...] + _dot(c_ref[...], wo_ref[...])
    h_out_ref[...] = h1
    xn = _rms(h1, g_ref[...])
    xn_ref[...] = xn.astype(BF16)
    logits = jnp.dot(xn, wr_ref[...], preferred_element_type=F32, precision=lax.Precision.HIGHEST)
    lane = lax.broadcasted_iota(jnp.int32, logits.shape, 1)
    v1 = jnp.max(logits, axis=-1, keepdims=True)
    i1 = jnp.min(jnp.where(logits == v1, lane, N_EXPERTS), axis=-1, keepdims=True)
    rest = jnp.where(lane == i1, NEG_INF, logits)
    v2 = jnp.max(rest, axis=-1, keepdims=True)
    i2 = jnp.min(jnp.where(rest == v2, lane, N_EXPERTS), axis=-1, keepdims=True)
    e2 = jnp.exp(v2 - v1)
    w1 = 1.0 / (1.0 + e2)
    w2 = e2 / (1.0 + e2)
    gates_ref[...] = jnp.where(lane == i1, w1, 0.0) + jnp.where(lane == i2, w2, 0.0)


def _odd_router(h, c, wo, g, wr, tm=512):
    n, d = h.shape
    row = lambda w: pl.BlockSpec((tm, w), lambda i: (i, 0))
    return pl.pallas_call(
        _odd_router_kernel,
        grid=(n // tm,),
        in_specs=[row(d), row(d), _resident(wo.shape), _resident(g.shape), _resident(wr.shape)],
        out_specs=[row(d), row(d), row(N_EXPERTS)],
        out_shape=[jax.ShapeDtypeStruct((n, d), F32), jax.ShapeDtypeStruct((n, d), BF16),
                   jax.ShapeDtypeStruct((n, N_EXPERTS), F32)],
        compiler_params=_params("arbitrary"),
        name="odd_out_router",
    )(h, c, wo, g, wr)


def _moe_dense_kernel(xn_ref, gates_ref, h_ref, wg_ref, wu_ref, wd_ref, out_ref, acc_ref):
    e = pl.program_id(1)
    f = pl.program_id(2)

    @pl.when((e == 0) & (f == 0))
    def _():
        acc_ref[...] = jnp.zeros(acc_ref.shape, F32)

    xn = xn_ref[...]
    gate = _dot(xn, wg_ref[...])
    up = _dot(xn, wu_ref[...])
    hid = (gate * jax.nn.sigmoid(gate) * up).astype(BF16)
    lane = lax.broadcasted_iota(jnp.int32, gates_ref.shape, 1)
    w = jnp.sum(jnp.where(lane == e, gates_ref[...], 0.0), axis=-1, keepdims=True)
    acc_ref[...] += w * _dot(hid, wd_ref[...])

    @pl.when((e == pl.num_programs(1) - 1) & (f == pl.num_programs(2) - 1))
    def _():
        out_ref[...] = h_ref[...] + acc_ref[...]


def _moe_dense(xn, gates, h, wg, wu, wd, tm=1024, tf=512):
    n, d = h.shape
    n_e, _, d_ff = wg.shape
    row = lambda w: pl.BlockSpec((tm, w), lambda i, e, f: (i, 0))
    return pl.pallas_call(
        _moe_dense_kernel,
        grid=(n // tm, n_e, d_ff // tf),
        in_specs=[row(d), row(N_EXPERTS), row(d),
                  pl.BlockSpec((None, d, tf), lambda i, e, f: (e, 0, f)),
                  pl.BlockSpec((None, d, tf), lambda i, e, f: (e, 0, f)),
                  pl.BlockSpec((None, tf, d), lambda i, e, f: (e, f, 0))],
        out_specs=row(d),
        out_shape=jax.ShapeDtypeStruct((n, d), F32),
        scratch_shapes=[pltpu.VMEM((tm, d), F32)],
        compiler_params=_params("arbitrary", "arbitrary", "arbitrary"),
        name="moe_experts",
    )(xn, gates, h, wg, wu, wd)


def _ple_final_kernel(h_ref, p_ref, gp_ref, wpg_ref, wpp_ref, gfin_ref, out_ref):
    h = _ple(h_ref[...], p_ref[...], gp_ref[...], wpg_ref[...], wpp_ref[...])
    out_ref[...] = _rms(h, gfin_ref[...])


def _ple_final(h, p, gp, wpg, wpp, gfin, tm=512):
    n, d = h.shape
    row = lambda w: pl.BlockSpec((tm, w), lambda i: (i, 0))
    return pl.pallas_call(
        _ple_final_kernel,
        grid=(n // tm,),
        in_specs=[row(d), row(p.shape[1]), _resident(gp.shape), _resident(wpg.shape),
                  _resident(wpp.shape), _resident(gfin.shape)],
        out_specs=row(d),
        out_shape=jax.ShapeDtypeStruct((n, d), F32),
        compiler_params=_params("arbitrary"),
        name="ple_final_norm",
    )(h, p, gp, wpg, wpp, gfin)


def kernel(x, p, rel_bias, final_norm, e_norm_mix, e_w_in, e_conv_w, e_sinks, e_w_out, e_norm_ffn, e_ffn_gate, e_ffn_up, e_ffn_down, e_norm_ple, e_ple_gate, e_ple_proj, o_norm_mix, o_w_qkv, o_w_o, o_norm_ffn, o_router, o_exp_gate, o_exp_up, o_exp_down, o_norm_ple, o_ple_gate, o_ple_proj):
    bsz, seq, d = x.shape
    n = bsz * seq
    bf = lambda w: w.astype(BF16)
    h = x.reshape(n, d)
    p2 = p.reshape(p.shape[0], n, p.shape[-1])

    qi = np.arange(SWA_BLOCK)[:, None]
    kj = np.arange(2 * SWA_BLOCK)[None, :]
    swa_tab = _bias_table(rel_bias, _t5_bucket_np(qi - kj + SWA_BLOCK))
    kk = np.arange(MOBA_BLOCK)[:, None]
    qq = np.arange(MOBA_BLOCK)[None, :]
    own_bucket = np.where(qq >= kk, _t5_bucket_np(qq - kk), -1)
    moba_bucket = np.concatenate([_t5_bucket_np(qq - kk + MOBA_BLOCK), own_bucket], axis=1)
    moba_tab = _bias_table(rel_bias, moba_bucket, mult=LOG2E)

    a_out, qkv0 = _even_proj(h, e_norm_mix[0:1], bf(e_w_in[0]), e_conv_w[0], seq)
    b_out = _swa(qkv0, e_sinks[0], swa_tab, bsz, seq)
    h = _even_ffn(h, a_out, b_out, p2[0], bf(e_w_out[0]), e_norm_ffn[0:1],
                  bf(e_ffn_gate[0]), bf(e_ffn_up[0]), bf(e_ffn_down[0]),
                  e_norm_ple[0:1], bf(e_ple_gate[0]), bf(e_ple_proj[0]))

    w_qkv = o_w_qkv[0]
    q1, k1, vt1, kmean = _odd_proj(h, o_norm_mix[0:1], bf(w_qkv[:, :d]), bf(w_qkv[:, d:2 * d]),
                                   bf(w_qkv[:, 2 * d:].T))
    c_out = _moba(q1, k1, vt1, kmean, moba_tab, rel_bias, bsz, seq)
    h, xn, gates = _odd_router(h, c_out, bf(o_w_o[0]), o_norm_ffn[0:1], o_router[0])
    h = _moe_dense(xn, gates, h, bf(o_exp_gate[0]), bf(o_exp_up[0]), bf(o_exp_down[0]))
    out = _ple_final(h, p2[1], o_norm_ple[0:1], bf(o_ple_gate[0]), bf(o_ple_proj[0]),
                     final_norm.reshape(1, d))
    return out.reshape(bsz, seq, d)
```

```python
import functools
import math

import jax
import jax.numpy as jnp
import numpy as np
from jax import lax
from jax.experimental import pallas as pl
from jax.experimental.pallas import tpu as pltpu

F32 = jnp.float32
BF16 = jnp.bfloat16

EPS = 1e-6
N_HEADS = 8
N_BUCKETS = 32
MAX_DISTANCE = 128

SC_WIDTH = 512
CONV_WIDTH = 3
SWA_KV_HEADS = 2
SWA_HEAD_DIM = 64
SWA_BLOCK = 128
SWA_Q_WIDTH = N_HEADS * SWA_HEAD_DIM
SWA_KV_WIDTH = SWA_KV_HEADS * SWA_HEAD_DIM

MOBA_HEAD_DIM = 128
MOBA_BLOCK = 256
MOBA_TOPK = 3
MOBA_FAR_GROUP = 4

N_EXPERTS = 8
MOE_TOKEN_TILE = 512
MOE_SLOT_TILE = 512
MOE_CHUNK = 128
MOE_ALIGN = 16

VMEM_LIMIT_BYTES = 56 * 1024 * 1024
NEG_INF = float("-inf")
LOG2E = math.log2(math.e)


def _params(*semantics):
    return pltpu.CompilerParams(dimension_semantics=semantics,
                                vmem_limit_bytes=VMEM_LIMIT_BYTES)


def _resident(shape):
    zeros = (0,) * len(shape)
    return pl.BlockSpec(shape, lambda *_: zeros, pipeline_mode=pl.Buffered(1))


def _rms(x, g):
    return x * lax.rsqrt(jnp.mean(x * x, axis=-1, keepdims=True) + EPS) * g


def _dot(a, b):
    return jnp.dot(a, b, preferred_element_type=F32)


def _dot_nt(a, b):
    return lax.dot_general(a, b, (((1,), (1,)), ((), ())), preferred_element_type=F32)


def _t5_bucket_np(dist):
    d = np.maximum(dist, 0).astype(np.int32)
    max_exact = N_BUCKETS // 2
    scaled = (np.log(np.maximum(d, max_exact).astype(np.float32) / np.float32(max_exact))
              / np.float32(math.log(MAX_DISTANCE / max_exact)))
    large = np.minimum(max_exact + (scaled * (N_BUCKETS - max_exact)).astype(np.int32),
                       N_BUCKETS - 1)
    return np.where(d < max_exact, d, large).astype(np.int32)


def _bias_table_kernel(rb_ref, bkt_ref, out_ref, *, mult):
    h = pl.program_id(0)
    bkt = bkt_ref[...]
    acc = jnp.full(bkt.shape, NEG_INF, F32)
    for b in range(N_BUCKETS):
        acc = jnp.where(bkt == b, rb_ref[b, h], acc)
    out_ref[...] = acc * mult


def _bias_table(rel_bias, bucket, mult=1.0):
    rows, cols = bucket.shape
    return pl.pallas_call(
        functools.partial(_bias_table_kernel, mult=mult),
        grid=(N_HEADS,),
        in_specs=[pl.BlockSpec(memory_space=pltpu.SMEM),
                  pl.BlockSpec((rows, cols), lambda h: (0, 0))],
        out_specs=pl.BlockSpec((None, rows, cols), lambda h: (h, 0, 0)),
        out_shape=jax.ShapeDtypeStruct((N_HEADS, rows, cols), F32),
        compiler_params=_params("arbitrary"),
        name="bias_table",
    )(rel_bias, jnp.asarray(bucket))


def _even_proj_kernel(x_ref, g_ref, w_ref, cw_ref, a_ref, qkv_ref, cu_scr, *, tm, tiles_per_seq):
    i = pl.program_id(0)
    xn = _rms(x_ref[...], g_ref[...]).astype(BF16)
    b_gate = _dot(xn, w_ref[:, 0:SC_WIDTH])
    c_gate = _dot(xn, w_ref[:, SC_WIDTH:2 * SC_WIDTH])
    u = _dot(xn, w_ref[:, 2 * SC_WIDTH:3 * SC_WIDTH])
    cu = c_gate * u

    @pl.when(i % tiles_per_seq == 0)
    def _():
        cu_scr[0:8, :] = jnp.zeros((8, SC_WIDTH), F32)

    @pl.when(i % tiles_per_seq != 0)
    def _():
        cu_scr[0:8, :] = cu_scr[tm:tm + 8, :]

    cu_scr[8:8 + tm, :] = cu
    y = (cw_ref[0:1, :] * cu_scr[6:6 + tm, :] + cw_ref[1:2, :] * cu_scr[7:7 + tm, :]
         + cw_ref[2:3, :] * cu)
    a_ref[...] = (b_gate * y).astype(BF16)
    qkv_ref[...] = _dot(xn, w_ref[:, 3 * SC_WIDTH:]).astype(BF16)


def _even_proj(x, g, w, conv_w, seq, tm=512):
    n, d = x.shape
    d_in = w.shape[1]
    d_qkv = d_in - 3 * SC_WIDTH
    return pl.pallas_call(
        functools.partial(_even_proj_kernel, tm=tm, tiles_per_seq=seq // tm),
        grid=(n // tm,),
        in_specs=[pl.BlockSpec((tm, d), lambda i: (i, 0)),
                  _resident((1, d)), _resident((d, d_in)), _resident((CONV_WIDTH, SC_WIDTH))],
        out_specs=[pl.BlockSpec((tm, SC_WIDTH), lambda i: (i, 0)),
                   pl.BlockSpec((tm, d_qkv), lambda i: (i, 0))],
        out_shape=[jax.ShapeDtypeStruct((n, SC_WIDTH), BF16),
                   jax.ShapeDtypeStruct((n, d_qkv), BF16)],
        scratch_shapes=[pltpu.VMEM((tm + 8, SC_WIDTH), F32)],
        compiler_params=_params("arbitrary"),
        name="even_proj_conv",
    )(x, g, w, conv_w)


def _swa_kernel(sink_ref, cur_ref, prev_ref, tab_ref, out_ref):
    i = pl.program_id(1)
    blk = SWA_BLOCK
    hd = SWA_HEAD_DIM
    grp = N_HEADS // SWA_KV_HEADS
    qi = lax.broadcasted_iota(jnp.int32, (blk, 2 * blk), 0)
    kj = lax.broadcasted_iota(jnp.int32, (blk, 2 * blk), 1)
    diff = qi - kj + blk
    valid = (diff >= 0) & (diff < blk) & ((kj >= blk) | (i > 0))
    scale = hd ** -0.5
    outs = []
    for g in range(SWA_KV_HEADS):
        k_cat = jnp.concatenate([prev_ref[:, g * hd:(g + 1) * hd],
                                 cur_ref[:, SWA_Q_WIDTH + g * hd:SWA_Q_WIDTH + (g + 1) * hd]], axis=0)
        v_cat = jnp.concatenate(
            [prev_ref[:, SWA_KV_WIDTH + g * hd:SWA_KV_WIDTH + (g + 1) * hd],
             cur_ref[:, SWA_Q_WIDTH + SWA_KV_WIDTH + g * hd:SWA_Q_WIDTH + SWA_KV_WIDTH + (g + 1) * hd]],
            axis=0)
        for hh in range(grp):
            h = g * grp + hh
            q = cur_ref[:, h * hd:(h + 1) * hd]
            s = _dot_nt(q, k_cat) * scale + tab_ref[h]
            s = jnp.where(valid, s, NEG_INF)
            sink = sink_ref[h]
            m = jnp.maximum(jnp.max(s, axis=-1, keepdims=True), sink)
            p = jnp.exp(s - m)
            denom = jnp.sum(p, axis=-1, keepdims=True) + jnp.exp(sink - m)
            o = _dot(p.astype(BF16), v_cat)
            outs.append(o / denom)
    out_ref[...] = jnp.concatenate(outs, axis=-1).astype(BF16)


def _swa(qkv, sinks, tab, bsz, seq):
    n = qkv.shape[0]
    nb = seq // SWA_BLOCK
    kv_col_block = SWA_Q_WIDTH // (2 * SWA_KV_WIDTH)
    return pl.pallas_call(
        _swa_kernel,
        grid=(bsz, nb),
        in_specs=[pl.BlockSpec(memory_space=pltpu.SMEM),
                  pl.BlockSpec((SWA_BLOCK, qkv.shape[1]), lambda b, i: (b * nb + i, 0)),
                  pl.BlockSpec((SWA_BLOCK, 2 * SWA_KV_WIDTH),
                               lambda b, i: (b * nb + jnp.maximum(i - 1, 0), kv_col_block)),
                  _resident((N_HEADS, SWA_BLOCK, 2 * SWA_BLOCK))],
        out_specs=pl.BlockSpec((SWA_BLOCK, SWA_Q_WIDTH), lambda b, i: (b * nb + i, 0)),
        out_shape=jax.ShapeDtypeStruct((n, SWA_Q_WIDTH), BF16),
        compiler_params=_params("arbitrary", "arbitrary"),
        name="swa",
    )(sinks, qkv, qkv, tab)


def _ple(h, p, g, w_gate, w_proj):
    gate = jax.nn.sigmoid(_dot(_rms(h, g).astype(BF16), w_gate))
    return h + gate * _dot(p.astype(BF16), w_proj)


def _even_ffn_kernel(h_ref, a_ref, b_ref, p_ref, wo_ref, gf_ref, wg_ref, wu_ref, wd_ref,
                     gp_ref, wpg_ref, wpp_ref, out_ref, *, tf):
    h1 = (h_ref[...] + _dot(a_ref[...], wo_ref[0:SC_WIDTH, :])
          + _dot(b_ref[...], wo_ref[SC_WIDTH:, :]))
    xn = _rms(h1, gf_ref[...]).astype(BF16)
    d_ff = wg_ref.shape[1]
    acc = jnp.zeros(h1.shape, F32)
    for c in range(d_ff // tf):
        gate = _dot(xn, wg_ref[:, c * tf:(c + 1) * tf])
        up = _dot(xn, wu_ref[:, c * tf:(c + 1) * tf])
        hid = (gate * jax.nn.sigmoid(gate) * up).astype(BF16)
        acc = acc + _dot(hid, wd_ref[c * tf:(c + 1) * tf, :])
    h2 = h1 + acc
    out_ref[...] = _ple(h2, p_ref[...], gp_ref[...], wpg_ref[...], wpp_ref[...])


def _even_ffn(h, a, b, p, wo, gf, wg, wu, wd, gp, wpg, wpp, tm=512, tf=256):
    n, d = h.shape
    row = lambda w: pl.BlockSpec((tm, w), lambda i: (i, 0))
    return pl.pallas_call(
        functools.partial(_even_ffn_kernel, tf=tf),
        grid=(n // tm,),
        in_specs=[row(d), row(a.shape[1]), row(b.shape[1]), row(p.shape[1]),
                  _resident(wo.shape), _resident(gf.shape), _resident(wg.shape),
                  _resident(wu.shape), _resident(wd.shape), _resident(gp.shape),
                  _resident(wpg.shape), _resident(wpp.shape)],
        out_specs=row(d),
        out_shape=jax.ShapeDtypeStruct((n, d), F32),
        compiler_params=_params("arbitrary"),
        name="even_out_ffn_ple",
    )(h, a, b, p, wo, gf, wg, wu, wd, gp, wpg, wpp)


def _odd_proj_kernel(h_ref, g_ref, wq_ref, wk_ref, wvt_ref, q_ref, k_ref, vt_ref, kmean_ref, *, tm):
    xn = _rms(h_ref[...], g_ref[...]).astype(BF16)
    q_ref[...] = (_dot(xn, wq_ref[...]) * (MOBA_HEAD_DIM ** -0.5 * LOG2E)).astype(BF16)
    k = _dot(xn, wk_ref[...])
    k_ref[...] = k.astype(BF16)
    vt_ref[...] = _dot_nt(wvt_ref[...], xn).astype(BF16)
    for r in range(tm // MOBA_BLOCK):
        kmean_ref[r] = jnp.mean(k[r * MOBA_BLOCK:(r + 1) * MOBA_BLOCK, :], axis=0, keepdims=True)


def _odd_proj(h, g, wq, wk, wvt, tm=512):
    n, d = h.shape
    bpt = tm // MOBA_BLOCK
    row = pl.BlockSpec((tm, d), lambda i: (i, 0))
    return pl.pallas_call(
        functools.partial(_odd_proj_kernel, tm=tm),
        grid=(n // tm,),
        in_specs=[row, _resident(g.shape), _resident(wq.shape), _resident(wk.shape),
                  _resident(wvt.shape)],
        out_specs=[row, row, pl.BlockSpec((d, tm), lambda i: (0, i)),
                   pl.BlockSpec((bpt, 1, d), lambda i: (i, 0, 0))],
        out_shape=[jax.ShapeDtypeStruct((n, d), BF16), jax.ShapeDtypeStruct((n, d), BF16),
                   jax.ShapeDtypeStruct((d, n), BF16),
                   jax.ShapeDtypeStruct((n // MOBA_BLOCK, 1, d), F32)],
        compiler_params=_params("arbitrary"),
        name="odd_qkv",
    )(h, g, wq, wk, wvt)


def _moba_kernel(rb_ref, q_ref, k_ref, vt_ref, km_ref, tab_ref, out_ref,
                 sel_scr, adj_scr, m_scr, l_scr, acc_scr, *, nblk, hg):
    head0 = pl.program_id(1) * hg
    qi = pl.program_id(2)
    bs = MOBA_BLOCK
    hd = MOBA_HEAD_DIM
    heads = [slice(hh * hd, (hh + 1) * hd) for hh in range(hg)]

    blk = lax.broadcasted_iota(jnp.int32, (nblk, bs), 0)
    past = blk < qi
    for hh in range(hg):
        gate = lax.dot_general(km_ref[:, heads[hh]], q_ref[:, heads[hh]].astype(F32),
                               (((1,), (1,)), ((), ())), preferred_element_type=F32,
                               precision=lax.Precision.HIGHEST)
        g = jnp.where(past, gate, NEG_INF)
        sel = jnp.zeros(gate.shape, jnp.bool_)
        for _ in range(MOBA_TOPK):
            top = jnp.max(g, axis=0, keepdims=True)
            idx = jnp.min(jnp.where(g == top, blk, nblk), axis=0, keepdims=True)
            hit = blk == idx
            sel = sel | hit
            g = jnp.where(hit, NEG_INF, g)
        far_bias = rb_ref[N_BUCKETS - 1, head0 + hh] * LOG2E
        sel_scr[hh] = jnp.where(sel & (blk < qi - 1), far_bias, NEG_INF)
        adj_scr[hh] = jnp.max(jnp.where(sel & (blk == qi - 1), 0.0, NEG_INF), axis=0, keepdims=True)

    def visit(hh, start, tables, query_biases, first):
        nb = len(tables)
        s = _dot_nt(k_ref[pl.ds(start, nb * bs), heads[hh]], q_ref[:, heads[hh]])
        parts = []
        for r in range(nb):
            s_r = s[r * bs:(r + 1) * bs]
            if tables[r] is not None:
                s_r = s_r + tables[r]
            if query_biases[r] is not None:
                s_r = s_r + query_biases[r]
            parts.append(s_r)
        s = jnp.concatenate(parts, axis=0)
        s_max = jnp.max(s, axis=0, keepdims=True)
        m_new = s_max if first else jnp.maximum(m_scr[hh], s_max)
        p = jnp.exp2(s - m_new)
        l_new = jnp.sum(p, axis=0, keepdims=True)
        acc_new = _dot(vt_ref[heads[hh], pl.ds(start, nb * bs)], p.astype(BF16))
        if not first:
            alpha = jnp.exp2(m_scr[hh] - m_new)
            l_new = alpha * l_scr[hh] + l_new
            acc_new = alpha * acc_scr[hh] + acc_new
        m_scr[hh] = m_new
        l_scr[hh] = l_new
        acc_scr[hh] = acc_new

    @pl.when(qi == 0)
    def _():
        for hh in range(hg):
            visit(hh, 0, [tab_ref[hh, :, bs:2 * bs]], [None], first=True)

    @pl.when(qi > 0)
    def _():
        start = pl.multiple_of((qi - 1) * bs, bs)
        for hh in range(hg):
            visit(hh, start, [tab_ref[hh, :, 0:bs], tab_ref[hh, :, bs:2 * bs]],
                  [adj_scr[hh], None], first=True)

    def far_body(g, carry):
        first_blk = g * MOBA_FAR_GROUP
        start = pl.multiple_of(first_blk * bs, MOBA_FAR_GROUP * bs)
        for hh in range(hg):
            visit(hh, start, [None] * MOBA_FAR_GROUP,
                  [sel_scr[hh, pl.ds(first_blk + r, 1), :] for r in range(MOBA_FAR_GROUP)],
                  first=False)
        return carry

    lax.fori_loop(0, (qi - 1 + MOBA_FAR_GROUP - 1) // MOBA_FAR_GROUP, far_body, 0)

    for hh in range(hg):
        out_ref[:, heads[hh]] = (acc_scr[hh] / l_scr[hh]).T.astype(BF16)


def _moba(q, k, vt, kmean, tab, rel_bias, bsz, seq, hg=4):
    n, d = q.shape
    nblk = seq // MOBA_BLOCK
    assert nblk % MOBA_FAR_GROUP == 0, "far-block groups must not run past the sequence"
    bs = MOBA_BLOCK
    w = hg * MOBA_HEAD_DIM
    return pl.pallas_call(
        functools.partial(_moba_kernel, nblk=nblk, hg=hg),
        grid=(bsz, N_HEADS // hg, nblk),
        in_specs=[pl.BlockSpec(memory_space=pltpu.SMEM),
                  pl.BlockSpec((bs, w), lambda b, g, i: (b * nblk + i, g)),
                  pl.BlockSpec((seq, w), lambda b, g, i: (b, g)),
                  pl.BlockSpec((w, seq), lambda b, g, i: (g, b)),
                  pl.BlockSpec((None, nblk, w), lambda b, g, i: (b, 0, g)),
                  pl.BlockSpec((hg, bs, 2 * bs), lambda b, g, i: (g, 0, 0))],
        out_specs=pl.BlockSpec((bs, w), lambda b, g, i: (b * nblk + i, g)),
        out_shape=jax.ShapeDtypeStruct((n, d), BF16),
        scratch_shapes=[pltpu.VMEM((hg, nblk, bs), F32), pltpu.VMEM((hg, 1, bs), F32),
                        pltpu.VMEM((hg, 1, bs), F32), pltpu.VMEM((hg, 1, bs), F32),
                        pltpu.VMEM((hg, MOBA_HEAD_DIM, bs), F32)],
        compiler_params=_params("arbitrary", "arbitrary", "arbitrary"),
        name="moba",
    )(rel_bias, q, k, vt, kmean.reshape(bsz, nblk, d), tab)


def _odd_router_kernel(h_ref, c_ref, wo_ref, g_ref, wr_ref, h_out_ref, xn_ref, gates_ref, sel_ref,
                       cnt_ref):
    h1 = h_ref[...] + _dot(c_ref[...], wo_ref[...])
    h_out_ref[...] = h1
    xn = _rms(h1, g_ref[...])
    xn_ref[...] = xn.astype(BF16)
    logits = jnp.dot(xn, wr_ref[...], preferred_element_type=F32, precision=lax.Precision.HIGHEST)
    lane = lax.broadcasted_iota(jnp.int32, logits.shape, 1)
    v1 = jnp.max(logits, axis=-1, keepdims=True)
    i1 = jnp.min(jnp.where(logits == v1, lane, N_EXPERTS), axis=-1, keepdims=True)
    rest = jnp.where(lane == i1, NEG_INF, logits)
    v2 = jnp.max(rest, axis=-1, keepdims=True)
    i2 = jnp.min(jnp.where(rest == v2, lane, N_EXPERTS), axis=-1, keepdims=True)
    e2 = jnp.exp(v2 - v1)
    w1 = 1.0 / (1.0 + e2)
    w2 = e2 / (1.0 + e2)
    gates_ref[...] = jnp.where(lane == i1, w1, 0.0) + jnp.where(lane == i2, w2, 0.0)
    sel = jnp.where((lane == i1) | (lane == i2), 1.0, 0.0)
    sel_ref[...] = sel
    cnt_ref[0] = jnp.sum(sel, axis=0, keepdims=True)


def _odd_router(h, c, wo, g, wr, tm=MOE_TOKEN_TILE):
    n, d = h.shape
    row = lambda w: pl.BlockSpec((tm, w), lambda i: (i, 0))
    return pl.pallas_call(
        _odd_router_kernel,
        grid=(n // tm,),
        in_specs=[row(d), row(d), _resident(wo.shape), _resident(g.shape), _resident(wr.shape)],
        out_specs=[row(d), row(d), row(N_EXPERTS), row(N_EXPERTS),
                   pl.BlockSpec((1, 1, N_EXPERTS), lambda i: (i, 0, 0))],
        out_shape=[jax.ShapeDtypeStruct((n, d), F32), jax.ShapeDtypeStruct((n, d), BF16),
                   jax.ShapeDtypeStruct((n, N_EXPERTS), F32),
                   jax.ShapeDtypeStruct((n, N_EXPERTS), F32),
                   jax.ShapeDtypeStruct((n // tm, 1, N_EXPERTS), F32)],
        compiler_params=_params("arbitrary"),
        name="odd_out_router",
    )(h, c, wo, g, wr)


def _moe_plan(counts, n_slot_tiles):
    cnt = counts.reshape(-1, N_EXPERTS).astype(jnp.int32)
    grp = (cnt + MOE_ALIGN - 1) // MOE_ALIGN * MOE_ALIGN
    tot = jnp.sum(grp, axis=0)
    region = (tot + MOE_SLOT_TILE - 1) // MOE_SLOT_TILE * MOE_SLOT_TILE
    region_end = jnp.cumsum(region)
    region_start = region_end - region
    start = region_start[None, :] + jnp.cumsum(grp, axis=0) - grp
    n_used = region_end[-1] // MOE_SLOT_TILE
    tile_row = jnp.arange(n_slot_tiles, dtype=jnp.int32) * MOE_SLOT_TILE
    tile_expert = jnp.searchsorted(region_end, tile_row, side="right").astype(jnp.int32)
    tile_expert = jnp.minimum(tile_expert, N_EXPERTS - 1)
    tile_expert = jnp.where(tile_row < region_end[-1], tile_expert, tile_expert[n_used - 1])
    return dict(cnt=cnt, start=start.astype(jnp.int32), tail_start=(region_start + tot).astype(jnp.int32),
                tail_len=(region - tot).astype(jnp.int32), tile_expert=tile_expert,
                n_used=n_used.reshape(1).astype(jnp.int32))


def _group_rows(cnt):
    return (cnt + MOE_ALIGN - 1) // MOE_ALIGN * MOE_ALIGN


def _group_pieces(rows):
    pieces = [(c * MOE_CHUNK, MOE_CHUNK, (c + 1) * MOE_CHUNK <= rows)
              for c in range(MOE_TOKEN_TILE // MOE_CHUNK)]
    base = rows // MOE_CHUNK * MOE_CHUNK
    rem = rows - base
    size = MOE_CHUNK // 2
    while size >= MOE_ALIGN:
        pieces.append((base + (rem & ~(2 * size - 1)), size, (rem & size) != 0))
        size //= 2
    return pieces


def _moe_dispatch_kernel(cnt_ref, start_ref, tail_start_ref, tail_len_ref, x_ref, sel_ref, upper_ref,
                         xs_ref, stage_ref, zero_ref, sem, tail_sem):
    t = pl.program_id(0)
    tile = x_ref.shape[0]
    x = x_ref[...]
    eye = jnp.where(lax.broadcasted_iota(jnp.int32, (N_EXPERTS, N_EXPERTS), 0)
                    == lax.broadcasted_iota(jnp.int32, (N_EXPERTS, N_EXPERTS), 1), 1.0, 0.0)
    sel_t = _dot_nt(eye.astype(BF16), sel_ref[...].astype(BF16))
    rank_t = _dot(sel_t.astype(BF16), upper_ref[...])
    slot_row = lax.broadcasted_iota(jnp.int32, (MOE_CHUNK, tile), 0).astype(F32)

    def copies(e):
        rows = _group_rows(cnt_ref[t, e])
        start = start_ref[t, e]
        return [(cond, pltpu.make_async_copy(
            stage_ref.at[e % 2, pl.ds(pl.multiple_of(off, MOE_ALIGN), size)],
            xs_ref.at[pl.ds(pl.multiple_of(start + off, MOE_ALIGN), size)],
            sem.at[e % 2])) for off, size, cond in _group_pieces(rows)]

    def start_all(copy_list):
        for cond, cp in copy_list:
            pl.when(cond)(cp.start)

    def wait_all(copy_list):
        for cond, cp in copy_list:
            pl.when(cond)(cp.wait)

    for e in range(N_EXPERTS):
        if e >= 2:
            wait_all(copies(e - 2))
        rows = _group_rows(cnt_ref[t, e])
        for c in range(tile // MOE_CHUNK):
            @pl.when(c * MOE_CHUNK < rows)
            def _(e=e, c=c):
                hit = (slot_row + c * MOE_CHUNK == rank_t[e:e + 1, :]) & (sel_t[e:e + 1, :] > 0.5)
                onehot = jnp.where(hit, 1.0, 0.0).astype(BF16)
                stage_ref[e % 2, c * MOE_CHUNK:(c + 1) * MOE_CHUNK, :] = _dot(onehot, x).astype(BF16)
        start_all(copies(e))
    wait_all(copies(N_EXPERTS - 2))
    wait_all(copies(N_EXPERTS - 1))

    @pl.when(t == pl.num_programs(0) - 1)
    def _():
        zero_ref[...] = jnp.zeros(zero_ref.shape, BF16)
        tails = []
        for e in range(N_EXPERTS):
            size = MOE_SLOT_TILE // 2
            while size >= MOE_ALIGN:
                off = tail_len_ref[e] & ~(2 * size - 1)
                dst = pl.multiple_of(tail_start_ref[e] + off, MOE_ALIGN)
                tails.append(((tail_len_ref[e] & size) != 0, pltpu.make_async_copy(
                    zero_ref.at[pl.ds(0, size)], xs_ref.at[pl.ds(dst, size)], tail_sem)))
                size //= 2
        start_all(tails)
        wait_all(tails)

        used_rows = tail_start_ref[N_EXPERTS - 1] + tail_len_ref[N_EXPERTS - 1]
        piece = zero_ref.shape[0]

        def unused_copy(k):
            dst = pl.multiple_of(used_rows + k * piece, piece)
            return pltpu.make_async_copy(zero_ref, xs_ref.at[pl.ds(dst, piece)], tail_sem)

        n_pieces = (xs_ref.shape[0] - used_rows) // piece
        lax.fori_loop(0, n_pieces, lambda k, c: (unused_copy(k).start(), c)[1], 0)
        lax.fori_loop(0, n_pieces, lambda k, c: (unused_copy(k).wait(), c)[1], 0)


def _moe_dispatch(xn, sel, plan, n_slot_tiles, tm=MOE_TOKEN_TILE):
    n, d = xn.shape
    upper = jnp.asarray(np.triu(np.ones((tm, tm), np.float32), 1), BF16)
    smem = pl.BlockSpec(memory_space=pltpu.SMEM)
    return pl.pallas_call(
        _moe_dispatch_kernel,
        grid=(n // tm,),
        in_specs=[smem, smem, smem, smem,
                  pl.BlockSpec((tm, d), lambda i: (i, 0)),
                  pl.BlockSpec((tm, N_EXPERTS), lambda i: (i, 0)),
                  _resident((tm, tm))],
        out_specs=pl.BlockSpec(memory_space=pl.ANY),
        out_shape=jax.ShapeDtypeStruct((n_slot_tiles * MOE_SLOT_TILE, d), BF16),
        scratch_shapes=[pltpu.VMEM((2, tm, d), BF16), pltpu.VMEM((MOE_SLOT_TILE // 2, d), BF16),
                        pltpu.SemaphoreType.DMA((2,)), pltpu.SemaphoreType.DMA(())],
        compiler_params=_params("arbitrary"),
        name="moe_dispatch",
    )(plan["cnt"], plan["start"], plan["tail_start"], plan["tail_len"], xn, sel, upper)


def _moe_experts_kernel(te_ref, nu_ref, x_ref, wg_ref, wu_ref, wd_ref, y_ref, acc_ref):
    i = pl.program_id(0)
    f = pl.program_id(1)
    last = pl.num_programs(1) - 1
    used = i < nu_ref[0]

    @pl.when(used & (f == 0))
    def _():
        acc_ref[...] = jnp.zeros(acc_ref.shape, F32)

    @pl.when(used)
    def _():
        x = x_ref[...]
        gate = _dot(x, wg_ref[...])
        up = _dot(x, wu_ref[...])
        hid = (gate * jax.nn.sigmoid(gate) * up).astype(BF16)
        acc_ref[...] += _dot(hid, wd_ref[...])

    @pl.when(used & (f == last))
    def _():
        y_ref[...] = acc_ref[...].astype(BF16)

    @pl.when(jnp.logical_not(used) & (f == last))
    def _():
        y_ref[...] = jnp.zeros(y_ref.shape, BF16)


def _moe_experts(xs, plan, wg, wu, wd, tf=512):
    rows, d = xs.shape
    d_ff = wg.shape[2]
    nf = d_ff // tf
    tm = MOE_SLOT_TILE
    f_idx = lambda i, f, nu: jnp.where(i < nu[0], f, nf - 1)
    return pl.pallas_call(
        _moe_experts_kernel,
        grid_spec=pltpu.PrefetchScalarGridSpec(
            num_scalar_prefetch=2,
            grid=(rows // tm, nf),
            in_specs=[pl.BlockSpec((tm, d), lambda i, f, te, nu: (jnp.minimum(i, nu[0] - 1), 0)),
                      pl.BlockSpec((None, d, tf), lambda i, f, te, nu: (te[i], 0, f_idx(i, f, nu))),
                      pl.BlockSpec((None, d, tf), lambda i, f, te, nu: (te[i], 0, f_idx(i, f, nu))),
                      pl.BlockSpec((None, tf, d), lambda i, f, te, nu: (te[i], f_idx(i, f, nu), 0))],
            out_specs=pl.BlockSpec((tm, d), lambda i, f, te, nu: (i, 0)),
            scratch_shapes=[pltpu.VMEM((tm, d), F32)]),
        out_shape=jax.ShapeDtypeStruct((rows, d), BF16),
        compiler_params=_params("arbitrary", "arbitrary"),
        name="moe_experts",
    )(plan["tile_expert"], plan["n_used"], xs, wg, wu, wd)


def _moe_combine_kernel(cnt_ref, start_ref, h_ref, gates_ref, sel_ref, lower_ref, p_ref, gp_ref,
                        wpg_ref, wpp_ref, gfin_ref, ys_ref, out_ref, ybuf_ref, acc_ref, sem):
    t = pl.program_id(0)
    tile = h_ref.shape[0]
    n_chunks = tile // MOE_CHUNK

    def chunk_copy(e, c):
        src = pl.multiple_of(start_ref[t, e] + c * MOE_CHUNK, MOE_ALIGN)
        k = e * n_chunks + c
        return pltpu.make_async_copy(ys_ref.at[pl.ds(src, MOE_CHUNK)], ybuf_ref.at[k], sem.at[k])

    chunks = [(e, c) for e in range(N_EXPERTS) for c in range(n_chunks)]
    for e, c in chunks:
        pl.when(c * MOE_CHUNK < cnt_ref[t, e])(chunk_copy(e, c).start)

    sel = sel_ref[...]
    gates = gates_ref[...]
    rank = _dot(lower_ref[...], sel.astype(BF16))
    slot_col = lax.broadcasted_iota(jnp.int32, (tile, MOE_CHUNK), 1).astype(F32)
    acc_ref[...] = h_ref[...]
    for e, c in chunks:
        @pl.when(c * MOE_CHUNK < cnt_ref[t, e])
        def _(e=e, c=c):
            chunk_copy(e, c).wait()
            hit = (slot_col + c * MOE_CHUNK == rank[:, e:e + 1]) & (sel[:, e:e + 1] > 0.5)
            onehot = jnp.where(hit, 1.0, 0.0).astype(BF16)
            acc_ref[...] += gates[:, e:e + 1] * _dot(onehot, ybuf_ref[e * n_chunks + c])

    h = _ple(acc_ref[...], p_ref[...], gp_ref[...], wpg_ref[...], wpp_ref[...])
    out_ref[...] = _rms(h, gfin_ref[...])


def _moe_combine(h, gates, sel, ys, plan, p, gp, wpg, wpp, gfin, tm=MOE_TOKEN_TILE):
    n, d = h.shape
    lower = jnp.asarray(np.tril(np.ones((tm, tm), np.float32), -1), BF16)
    smem = pl.BlockSpec(memory_space=pltpu.SMEM)
    row = lambda w: pl.BlockSpec((tm, w), lambda i: (i, 0))
    n_bufs = N_EXPERTS * (tm // MOE_CHUNK)
    return pl.pallas_call(
        _moe_combine_kernel,
        grid=(n // tm,),
        in_specs=[smem, smem, row(d), row(N_EXPERTS), row(N_EXPERTS), _resident((tm, tm)),
                  row(p.shape[1]), _resident(gp.shape), _resident(wpg.shape), _resident(wpp.shape),
                  _resident(gfin.shape), pl.BlockSpec(memory_space=pl.ANY)],
        out_specs=row(d),
        out_shape=jax.ShapeDtypeStruct((n, d), F32),
        scratch_shapes=[pltpu.VMEM((n_bufs, MOE_CHUNK, d), BF16), pltpu.VMEM((tm, d), F32),
                        pltpu.SemaphoreType.DMA((n_bufs,))],
        compiler_params=_params("arbitrary"),
        name="moe_combine_ple_norm",
    )(plan["cnt"], plan["start"], h, gates, sel, lower, p, gp, wpg, wpp, gfin, ys)


def kernel(x, p, rel_bias, final_norm, e_norm_mix, e_w_in, e_conv_w, e_sinks, e_w_out, e_norm_ffn, e_ffn_gate, e_ffn_up, e_ffn_down, e_norm_ple, e_ple_gate, e_ple_proj, o_norm_mix, o_w_qkv, o_w_o, o_norm_ffn, o_router, o_exp_gate, o_exp_up, o_exp_down, o_norm_ple, o_ple_gate, o_ple_proj):
    bsz, seq, d = x.shape
    n = bsz * seq
    bf = lambda w: w.astype(BF16)
    h = x.reshape(n, d)
    p2 = p.reshape(p.shape[0], n, p.shape[-1])

    qi = np.arange(SWA_BLOCK)[:, None]
    kj = np.arange(2 * SWA_BLOCK)[None, :]
    swa_tab = _bias_table(rel_bias, _t5_bucket_np(qi - kj + SWA_BLOCK))
    kk = np.arange(MOBA_BLOCK)[:, None]
    qq = np.arange(MOBA_BLOCK)[None, :]
    own_bucket = np.where(qq >= kk, _t5_bucket_np(qq - kk), -1)
    moba_bucket = np.concatenate([_t5_bucket_np(qq - kk + MOBA_BLOCK), own_bucket], axis=1)
    moba_tab = _bias_table(rel_bias, moba_bucket, mult=LOG2E)

    a_out, qkv0 = _even_proj(h, e_norm_mix[0:1], bf(e_w_in[0]), e_conv_w[0], seq)
    b_out = _swa(qkv0, e_sinks[0], swa_tab, bsz, seq)
    h = _even_ffn(h, a_out, b_out, p2[0], bf(e_w_out[0]), e_norm_ffn[0:1],
                  bf(e_ffn_gate[0]), bf(e_ffn_up[0]), bf(e_ffn_down[0]),
                  e_norm_ple[0:1], bf(e_ple_gate[0]), bf(e_ple_proj[0]))

    w_qkv = o_w_qkv[0]
    q1, k1, vt1, kmean = _odd_proj(h, o_norm_mix[0:1], bf(w_qkv[:, :d]), bf(w_qkv[:, d:2 * d]),
                                   bf(w_qkv[:, 2 * d:].T))
    c_out = _moba(q1, k1, vt1, kmean, moba_tab, rel_bias, bsz, seq)
    h, xn, gates, sel, counts = _odd_router(h, c_out, bf(o_w_o[0]), o_norm_ffn[0:1], o_router[0])
    n_groups = (n // MOE_TOKEN_TILE) * N_EXPERTS
    max_rows = 2 * n + n_groups * (MOE_ALIGN - 1) + N_EXPERTS * (MOE_SLOT_TILE - 1)
    n_slot_tiles = -(-max_rows // MOE_SLOT_TILE) + 1
    plan = _moe_plan(counts, n_slot_tiles)
    xs = _moe_dispatch(xn, sel, plan, n_slot_tiles)
    ys = _moe_experts(xs, plan, bf(o_exp_gate[0]), bf(o_exp_up[0]), bf(o_exp_down[0]))
    out = _moe_combine(h, gates, sel, ys, plan, p2[1], o_norm_ple[0:1], bf(o_ple_gate[0]),
                       bf(o_ple_proj[0]), final_norm.reshape(1, d))
    return out.reshape(bsz, seq, d)
```

```python
import functools
import math

import jax
import jax.numpy as jnp
import numpy as np
from jax import lax
from jax.experimental import pallas as pl
from jax.experimental.pallas import tpu as pltpu

F32 = jnp.float32
BF16 = jnp.bfloat16

EPS = 1e-6
N_HEADS = 8
N_BUCKETS = 32
MAX_DISTANCE = 128

SC_WIDTH = 512
CONV_WIDTH = 3
SWA_KV_HEADS = 2
SWA_HEAD_DIM = 64
SWA_BLOCK = 128
SWA_Q_WIDTH = N_HEADS * SWA_HEAD_DIM
SWA_KV_WIDTH = SWA_KV_HEADS * SWA_HEAD_DIM

MOBA_HEAD_DIM = 128
MOBA_BLOCK = 256
MOBA_TOPK = 3
MOBA_FAR_GROUP = 4

N_EXPERTS = 8
MOE_TOKEN_TILE = 512
MOE_SLOT_TILE = 512
MOE_ALIGN = 16
MOE_LOCAL_ROWS = -(-(2 * MOE_TOKEN_TILE + N_EXPERTS * (MOE_ALIGN - 1)) // 128) * 128

VMEM_LIMIT_BYTES = 56 * 1024 * 1024
NEG_INF = float("-inf")
LOG2E = math.log2(math.e)


def _params(*semantics):
    return pltpu.CompilerParams(dimension_semantics=semantics,
                                vmem_limit_bytes=VMEM_LIMIT_BYTES)


def _resident(shape):
    zeros = (0,) * len(shape)
    return pl.BlockSpec(shape, lambda *_: zeros, pipeline_mode=pl.Buffered(1))


def _rms(x, g):
    return x * lax.rsqrt(jnp.mean(x * x, axis=-1, keepdims=True) + EPS) * g


def _dot(a, b):
    return jnp.dot(a, b, preferred_element_type=F32)


def _dot_nt(a, b):
    return lax.dot_general(a, b, (((1,), (1,)), ((), ())), preferred_element_type=F32)


def _t5_bucket_np(dist):
    d = np.maximum(dist, 0).astype(np.int32)
    max_exact = N_BUCKETS // 2
    scaled = (np.log(np.maximum(d, max_exact).astype(np.float32) / np.float32(max_exact))
              / np.float32(math.log(MAX_DISTANCE / max_exact)))
    large = np.minimum(max_exact + (scaled * (N_BUCKETS - max_exact)).astype(np.int32),
                       N_BUCKETS - 1)
    return np.where(d < max_exact, d, large).astype(np.int32)


def _bias_table_kernel(rb_ref, bkt_ref, out_ref, *, mult):
    h = pl.program_id(0)
    bkt = bkt_ref[...]
    acc = jnp.full(bkt.shape, NEG_INF, F32)
    for b in range(N_BUCKETS):
        acc = jnp.where(bkt == b, rb_ref[b, h], acc)
    out_ref[...] = acc * mult


def _bias_table(rel_bias, bucket, mult=1.0):
    rows, cols = bucket.shape
    return pl.pallas_call(
        functools.partial(_bias_table_kernel, mult=mult),
        grid=(N_HEADS,),
        in_specs=[pl.BlockSpec(memory_space=pltpu.SMEM),
                  pl.BlockSpec((rows, cols), lambda h: (0, 0))],
        out_specs=pl.BlockSpec((None, rows, cols), lambda h: (h, 0, 0)),
        out_shape=jax.ShapeDtypeStruct((N_HEADS, rows, cols), F32),
        compiler_params=_params("arbitrary"),
        name="bias_table",
    )(rel_bias, jnp.asarray(bucket))


def _even_proj_kernel(x_ref, g_ref, w_ref, cw_ref, a_ref, qkv_ref, cu_scr, *, tm, tiles_per_seq):
    i = pl.program_id(0)
    xn = _rms(x_ref[...], g_ref[...]).astype(BF16)
    b_gate = _dot(xn, w_ref[:, 0:SC_WIDTH])
    c_gate = _dot(xn, w_ref[:, SC_WIDTH:2 * SC_WIDTH])
    u = _dot(xn, w_ref[:, 2 * SC_WIDTH:3 * SC_WIDTH])
    cu = c_gate * u

    @pl.when(i % tiles_per_seq == 0)
    def _():
        cu_scr[0:8, :] = jnp.zeros((8, SC_WIDTH), F32)

    @pl.when(i % tiles_per_seq != 0)
    def _():
        cu_scr[0:8, :] = cu_scr[tm:tm + 8, :]

    cu_scr[8:8 + tm, :] = cu
    y = (cw_ref[0:1, :] * cu_scr[6:6 + tm, :] + cw_ref[1:2, :] * cu_scr[7:7 + tm, :]
         + cw_ref[2:3, :] * cu)
    a_ref[...] = (b_gate * y).astype(BF16)
    qkv_ref[...] = _dot(xn, w_ref[:, 3 * SC_WIDTH:]).astype(BF16)


def _even_proj(x, g, w, conv_w, seq, tm=512):
    n, d = x.shape
    d_in = w.shape[1]
    d_qkv = d_in - 3 * SC_WIDTH
    return pl.pallas_call(
        functools.partial(_even_proj_kernel, tm=tm, tiles_per_seq=seq // tm),
        grid=(n // tm,),
        in_specs=[pl.BlockSpec((tm, d), lambda i: (i, 0)),
                  _resident((1, d)), _resident((d, d_in)), _resident((CONV_WIDTH, SC_WIDTH))],
        out_specs=[pl.BlockSpec((tm, SC_WIDTH), lambda i: (i, 0)),
                   pl.BlockSpec((tm, d_qkv), lambda i: (i, 0))],
        out_shape=[jax.ShapeDtypeStruct((n, SC_WIDTH), BF16),
                   jax.ShapeDtypeStruct((n, d_qkv), BF16)],
        scratch_shapes=[pltpu.VMEM((tm + 8, SC_WIDTH), F32)],
        compiler_params=_params("arbitrary"),
        name="even_proj_conv",
    )(x, g, w, conv_w)


def _swa_kernel(sink_ref, cur_ref, prev_ref, tab_ref, out_ref):
    i = pl.program_id(1)
    blk = SWA_BLOCK
    hd = SWA_HEAD_DIM
    grp = N_HEADS // SWA_KV_HEADS
    qi = lax.broadcasted_iota(jnp.int32, (blk, 2 * blk), 0)
    kj = lax.broadcasted_iota(jnp.int32, (blk, 2 * blk), 1)
    diff = qi - kj + blk
    valid = (diff >= 0) & (diff < blk) & ((kj >= blk) | (i > 0))
    scale = hd ** -0.5
    outs = []
    for g in range(SWA_KV_HEADS):
        k_cat = jnp.concatenate([prev_ref[:, g * hd:(g + 1) * hd],
                                 cur_ref[:, SWA_Q_WIDTH + g * hd:SWA_Q_WIDTH + (g + 1) * hd]], axis=0)
        v_cat = jnp.concatenate(
            [prev_ref[:, SWA_KV_WIDTH + g * hd:SWA_KV_WIDTH + (g + 1) * hd],
             cur_ref[:, SWA_Q_WIDTH + SWA_KV_WIDTH + g * hd:SWA_Q_WIDTH + SWA_KV_WIDTH + (g + 1) * hd]],
            axis=0)
        for hh in range(grp):
            h = g * grp + hh
            q = cur_ref[:, h * hd:(h + 1) * hd]
            s = _dot_nt(q, k_cat) * scale + tab_ref[h]
            s = jnp.where(valid, s, NEG_INF)
            sink = sink_ref[h]
            m = jnp.maximum(jnp.max(s, axis=-1, keepdims=True), sink)
            p = jnp.exp(s - m)
            denom = jnp.sum(p, axis=-1, keepdims=True) + jnp.exp(sink - m)
            o = _dot(p.astype(BF16), v_cat)
            outs.append(o / denom)
    out_ref[...] = jnp.concatenate(outs, axis=-1).astype(BF16)


def _swa(qkv, sinks, tab, bsz, seq):
    n = qkv.shape[0]
    nb = seq // SWA_BLOCK
    kv_col_block = SWA_Q_WIDTH // (2 * SWA_KV_WIDTH)
    return pl.pallas_call(
        _swa_kernel,
        grid=(bsz, nb),
        in_specs=[pl.BlockSpec(memory_space=pltpu.SMEM),
                  pl.BlockSpec((SWA_BLOCK, qkv.shape[1]), lambda b, i: (b * nb + i, 0)),
                  pl.BlockSpec((SWA_BLOCK, 2 * SWA_KV_WIDTH),
                               lambda b, i: (b * nb + jnp.maximum(i - 1, 0), kv_col_block)),
                  _resident((N_HEADS, SWA_BLOCK, 2 * SWA_BLOCK))],
        out_specs=pl.BlockSpec((SWA_BLOCK, SWA_Q_WIDTH), lambda b, i: (b * nb + i, 0)),
        out_shape=jax.ShapeDtypeStruct((n, SWA_Q_WIDTH), BF16),
        compiler_params=_params("arbitrary", "arbitrary"),
        name="swa",
    )(sinks, qkv, qkv, tab)


def _ple(h, p, g, w_gate, w_proj):
    gate = jax.nn.sigmoid(_dot(_rms(h, g).astype(BF16), w_gate))
    return h + gate * _dot(p.astype(BF16), w_proj)


def _even_ffn_kernel(h_ref, a_ref, b_ref, p_ref, wo_ref, gf_ref, wg_ref, wu_ref, wd_ref,
                     gp_ref, wpg_ref, wpp_ref, out_ref, *, tf):
    h1 = (h_ref[...] + _dot(a_ref[...], wo_ref[0:SC_WIDTH, :])
          + _dot(b_ref[...], wo_ref[SC_WIDTH:, :]))
    xn = _rms(h1, gf_ref[...]).astype(BF16)
    d_ff = wg_ref.shape[1]
    acc = jnp.zeros(h1.shape, F32)
    for c in range(d_ff // tf):
        gate = _dot(xn, wg_ref[:, c * tf:(c + 1) * tf])
        up = _dot(xn, wu_ref[:, c * tf:(c + 1) * tf])
        hid = (gate * jax.nn.sigmoid(gate) * up).astype(BF16)
        acc = acc + _dot(hid, wd_ref[c * tf:(c + 1) * tf, :])
    h2 = h1 + acc
    out_ref[...] = _ple(h2, p_ref[...], gp_ref[...], wpg_ref[...], wpp_ref[...])


def _even_ffn(h, a, b, p, wo, gf, wg, wu, wd, gp, wpg, wpp, tm=512, tf=256):
    n, d = h.shape
    row = lambda w: pl.BlockSpec((tm, w), lambda i: (i, 0))
    return pl.pallas_call(
        functools.partial(_even_ffn_kernel, tf=tf),
        grid=(n // tm,),
        in_specs=[row(d), row(a.shape[1]), row(b.shape[1]), row(p.shape[1]),
                  _resident(wo.shape), _resident(gf.shape), _resident(wg.shape),
                  _resident(wu.shape), _resident(wd.shape), _resident(gp.shape),
                  _resident(wpg.shape), _resident(wpp.shape)],
        out_specs=row(d),
        out_shape=jax.ShapeDtypeStruct((n, d), F32),
        compiler_params=_params("arbitrary"),
        name="even_out_ffn_ple",
    )(h, a, b, p, wo, gf, wg, wu, wd, gp, wpg, wpp)


def _odd_proj_kernel(h_ref, g_ref, wq_ref, wk_ref, wvt_ref, q_ref, k_ref, vt_ref, kmean_ref, *, tm):
    xn = _rms(h_ref[...], g_ref[...]).astype(BF16)
    q_ref[...] = (_dot(xn, wq_ref[...]) * (MOBA_HEAD_DIM ** -0.5 * LOG2E)).astype(BF16)
    k = _dot(xn, wk_ref[...])
    k_ref[...] = k.astype(BF16)
    vt_ref[...] = _dot_nt(wvt_ref[...], xn).astype(BF16)
    for r in range(tm // MOBA_BLOCK):
        kmean_ref[r] = jnp.mean(k[r * MOBA_BLOCK:(r + 1) * MOBA_BLOCK, :], axis=0, keepdims=True)


def _odd_proj(h, g, wq, wk, wvt, tm=512):
    n, d = h.shape
    bpt = tm // MOBA_BLOCK
    row = pl.BlockSpec((tm, d), lambda i: (i, 0))
    return pl.pallas_call(
        functools.partial(_odd_proj_kernel, tm=tm),
        grid=(n // tm,),
        in_specs=[row, _resident(g.shape), _resident(wq.shape), _resident(wk.shape),
                  _resident(wvt.shape)],
        out_specs=[row, row, pl.BlockSpec((d, tm), lambda i: (0, i)),
                   pl.BlockSpec((bpt, 1, d), lambda i: (i, 0, 0))],
        out_shape=[jax.ShapeDtypeStruct((n, d), BF16), jax.ShapeDtypeStruct((n, d), BF16),
                   jax.ShapeDtypeStruct((d, n), BF16),
                   jax.ShapeDtypeStruct((n // MOBA_BLOCK, 1, d), F32)],
        compiler_params=_params("arbitrary"),
        name="odd_qkv",
    )(h, g, wq, wk, wvt)


def _moba_kernel(rb_ref, q_ref, k_ref, vt_ref, km_ref, tab_ref, out_ref,
                 sel_scr, adj_scr, m_scr, l_scr, acc_scr, *, nblk, hg):
    head0 = pl.program_id(1) * hg
    qi = pl.program_id(2)
    bs = MOBA_BLOCK
    hd = MOBA_HEAD_DIM
    heads = [slice(hh * hd, (hh + 1) * hd) for hh in range(hg)]

    blk = lax.broadcasted_iota(jnp.int32, (nblk, bs), 0)
    past = blk < qi
    for hh in range(hg):
        gate = lax.dot_general(km_ref[:, heads[hh]], q_ref[:, heads[hh]].astype(F32),
                               (((1,), (1,)), ((), ())), preferred_element_type=F32,
                               precision=lax.Precision.HIGHEST)
        g = jnp.where(past, gate, NEG_INF)
        sel = jnp.zeros(gate.shape, jnp.bool_)
        for _ in range(MOBA_TOPK):
            top = jnp.max(g, axis=0, keepdims=True)
            idx = jnp.min(jnp.where(g == top, blk, nblk), axis=0, keepdims=True)
            hit = blk == idx
            sel = sel | hit
            g = jnp.where(hit, NEG_INF, g)
        far_bias = rb_ref[N_BUCKETS - 1, head0 + hh] * LOG2E
        sel_scr[hh] = jnp.where(sel & (blk < qi - 1), far_bias, NEG_INF)
        adj_scr[hh] = jnp.max(jnp.where(sel & (blk == qi - 1), 0.0, NEG_INF), axis=0, keepdims=True)

    def visit(hh, start, tables, query_biases, first):
        nb = len(tables)
        s = _dot_nt(k_ref[pl.ds(start, nb * bs), heads[hh]], q_ref[:, heads[hh]])
        parts = []
        for r in range(nb):
            s_r = s[r * bs:(r + 1) * bs]
            if tables[r] is not None:
                s_r = s_r + tables[r]
            if query_biases[r] is not None:
                s_r = s_r + query_biases[r]
            parts.append(s_r)
        s = jnp.concatenate(parts, axis=0)
        s_max = jnp.max(s, axis=0, keepdims=True)
        m_new = s_max if first else jnp.maximum(m_scr[hh], s_max)
        p = jnp.exp2(s - m_new)
        l_new = jnp.sum(p, axis=0, keepdims=True)
        acc_new = _dot(vt_ref[heads[hh], pl.ds(start, nb * bs)], p.astype(BF16))
        if not first:
            alpha = jnp.exp2(m_scr[hh] - m_new)
            l_new = alpha * l_scr[hh] + l_new
            acc_new = alpha * acc_scr[hh] + acc_new
        m_scr[hh] = m_new
        l_scr[hh] = l_new
        acc_scr[hh] = acc_new

    @pl.when(qi == 0)
    def _():
        for hh in range(hg):
            visit(hh, 0, [tab_ref[hh, :, bs:2 * bs]], [None], first=True)

    @pl.when(qi > 0)
    def _():
        start = pl.multiple_of((qi - 1) * bs, bs)
        for hh in range(hg):
            visit(hh, start, [tab_ref[hh, :, 0:bs], tab_ref[hh, :, bs:2 * bs]],
                  [adj_scr[hh], None], first=True)

    def far_body(g, carry):
        first_blk = g * MOBA_FAR_GROUP
        start = pl.multiple_of(first_blk * bs, MOBA_FAR_GROUP * bs)
        for hh in range(hg):
            visit(hh, start, [None] * MOBA_FAR_GROUP,
                  [sel_scr[hh, pl.ds(first_blk + r, 1), :] for r in range(MOBA_FAR_GROUP)],
                  first=False)
        return carry

    lax.fori_loop(0, (qi - 1 + MOBA_FAR_GROUP - 1) // MOBA_FAR_GROUP, far_body, 0)

    for hh in range(hg):
        out_ref[:, heads[hh]] = (acc_scr[hh] / l_scr[hh]).T.astype(BF16)


def _moba(q, k, vt, kmean, tab, rel_bias, bsz, seq, hg=4):
    n, d = q.shape
    nblk = seq // MOBA_BLOCK
    assert nblk % MOBA_FAR_GROUP == 0, "far-block groups must not run past the sequence"
    bs = MOBA_BLOCK
    w = hg * MOBA_HEAD_DIM
    return pl.pallas_call(
        functools.partial(_moba_kernel, nblk=nblk, hg=hg),
        grid=(bsz, N_HEADS // hg, nblk),
        in_specs=[pl.BlockSpec(memory_space=pltpu.SMEM),
                  pl.BlockSpec((bs, w), lambda b, g, i: (b * nblk + i, g)),
                  pl.BlockSpec((seq, w), lambda b, g, i: (b, g)),
                  pl.BlockSpec((w, seq), lambda b, g, i: (g, b)),
                  pl.BlockSpec((None, nblk, w), lambda b, g, i: (b, 0, g)),
                  pl.BlockSpec((hg, bs, 2 * bs), lambda b, g, i: (g, 0, 0))],
        out_specs=pl.BlockSpec((bs, w), lambda b, g, i: (b * nblk + i, g)),
        out_shape=jax.ShapeDtypeStruct((n, d), BF16),
        scratch_shapes=[pltpu.VMEM((hg, nblk, bs), F32), pltpu.VMEM((hg, 1, bs), F32),
                        pltpu.VMEM((hg, 1, bs), F32), pltpu.VMEM((hg, 1, bs), F32),
                        pltpu.VMEM((hg, MOBA_HEAD_DIM, bs), F32)],
        compiler_params=_params("arbitrary", "arbitrary", "arbitrary"),
        name="moba",
    )(rel_bias, q, k, vt, kmean.reshape(bsz, nblk, d), tab)


def _odd_router_kernel(h_ref, c_ref, wo_ref, g_ref, wr_ref, h_out_ref, xn_ref, gates_ref, sel_ref,
                       cnt_ref):
    h1 = h_ref[...] + _dot(c_ref[...], wo_ref[...])
    h_out_ref[...] = h1
    xn = _rms(h1, g_ref[...])
    xn_hi = xn.astype(BF16)
    xn_ref[...] = xn_hi
    xn_lo = (xn - xn_hi.astype(F32)).astype(BF16)
    wr = wr_ref[...]
    wr_hi = wr.astype(BF16)
    wr_lo = (wr - wr_hi.astype(F32)).astype(BF16)
    logits = _dot(xn_hi, wr_hi) + (_dot(xn_lo, wr_hi) + _dot(xn_hi, wr_lo))
    lane = lax.broadcasted_iota(jnp.int32, logits.shape, 1)
    v1 = jnp.max(logits, axis=-1, keepdims=True)
    i1 = jnp.min(jnp.where(logits == v1, lane, N_EXPERTS), axis=-1, keepdims=True)
    rest = jnp.where(lane == i1, NEG_INF, logits)
    v2 = jnp.max(rest, axis=-1, keepdims=True)
    i2 = jnp.min(jnp.where(rest == v2, lane, N_EXPERTS), axis=-1, keepdims=True)
    e2 = jnp.exp(v2 - v1)
    w1 = 1.0 / (1.0 + e2)
    w2 = e2 / (1.0 + e2)
    gates_ref[...] = jnp.where(lane == i1, w1, 0.0) + jnp.where(lane == i2, w2, 0.0)
    sel = jnp.where((lane == i1) | (lane == i2), 1.0, 0.0)
    sel_ref[...] = sel
    cnt_ref[0] = jnp.sum(sel, axis=0, keepdims=True)


def _odd_router(h, c, wo, g, wr, tm=MOE_TOKEN_TILE):
    n, d = h.shape
    row = lambda w: pl.BlockSpec((tm, w), lambda i: (i, 0))
    return pl.pallas_call(
        _odd_router_kernel,
        grid=(n // tm,),
        in_specs=[row(d), row(d), _resident(wo.shape), _resident(g.shape), _resident(wr.shape)],
        out_specs=[row(d), row(d), row(N_EXPERTS), row(N_EXPERTS),
                   pl.BlockSpec((1, 1, N_EXPERTS), lambda i: (i, 0, 0))],
        out_shape=[jax.ShapeDtypeStruct((n, d), F32), jax.ShapeDtypeStruct((n, d), BF16),
                   jax.ShapeDtypeStruct((n, N_EXPERTS), F32),
                   jax.ShapeDtypeStruct((n, N_EXPERTS), F32),
                   jax.ShapeDtypeStruct((n // tm, 1, N_EXPERTS), F32)],
        compiler_params=_params("arbitrary"),
        name="odd_out_router",
    )(h, c, wo, g, wr)


def _moe_plan(counts, n_slot_tiles):
    cnt = counts.reshape(-1, N_EXPERTS).astype(jnp.int32)
    grp = (cnt + MOE_ALIGN - 1) // MOE_ALIGN * MOE_ALIGN
    local = jnp.cumsum(grp, axis=1) - grp
    tot = jnp.sum(grp, axis=0)
    region = (tot + MOE_SLOT_TILE - 1) // MOE_SLOT_TILE * MOE_SLOT_TILE
    region_end = jnp.cumsum(region)
    region_start = region_end - region
    start = region_start[None, :] + jnp.cumsum(grp, axis=0) - grp
    n_used = region_end[-1] // MOE_SLOT_TILE
    tile_row = jnp.arange(n_slot_tiles, dtype=jnp.int32) * MOE_SLOT_TILE
    tile_expert = jnp.searchsorted(region_end, tile_row, side="right").astype(jnp.int32)
    tile_expert = jnp.minimum(tile_expert, N_EXPERTS - 1)
    tile_expert = jnp.where(tile_row < region_end[-1], tile_expert, tile_expert[n_used - 1])
    return dict(cnt=cnt, local=local.astype(jnp.int32), start=start.astype(jnp.int32),
                tail_start=(region_start + tot).astype(jnp.int32),
                tail_len=(region - tot).astype(jnp.int32), tile_expert=tile_expert,
                n_used=n_used.reshape(1).astype(jnp.int32))


def _group_copies(cnt_ref, local_ref, start_ref, t, local_buf, slot_array, sem, to_slots):
    copies = []
    for e in range(N_EXPERTS):
        rows = (cnt_ref[t, e] + MOE_ALIGN - 1) // MOE_ALIGN * MOE_ALIGN
        size = MOE_TOKEN_TILE
        while size >= MOE_ALIGN:
            off = rows & ~(2 * size - 1)
            local = local_buf.at[pl.ds(pl.multiple_of(local_ref[t, e] + off, MOE_ALIGN), size)]
            slots = slot_array.at[pl.ds(pl.multiple_of(start_ref[t, e] + off, MOE_ALIGN), size)]
            src, dst = (local, slots) if to_slots else (slots, local)
            copies.append(((rows & size) != 0, pltpu.make_async_copy(src, dst, sem)))
            size //= 2
    return copies


def _start_all(copies):
    for cond, cp in copies:
        pl.when(cond)(cp.start)


def _wait_all(copies):
    for cond, cp in copies:
        pl.when(cond)(cp.wait)


def _moe_dispatch_kernel(cnt_ref, local_ref, start_ref, tail_start_ref, tail_len_ref, x_ref, sel_ref,
                         upper_ref, xs_ref, stage_ref, zero_ref, sem, tail_sem):
    t = pl.program_id(0)
    last = pl.num_programs(0) - 1
    tile = x_ref.shape[0]
    slot = t % 2
    eye = jnp.where(lax.broadcasted_iota(jnp.int32, (N_EXPERTS, N_EXPERTS), 0)
                    == lax.broadcasted_iota(jnp.int32, (N_EXPERTS, N_EXPERTS), 1), 1.0, 0.0)
    sel_t = _dot_nt(eye.astype(BF16), sel_ref[...].astype(BF16))
    rank_t = _dot(sel_t.astype(BF16), upper_ref[...])
    expert = lax.broadcasted_iota(jnp.int32, (N_EXPERTS, 1), 0)
    base = jnp.zeros((N_EXPERTS, 1), F32)
    for e in range(N_EXPERTS):
        base = jnp.where(expert == e, local_ref[t, e].astype(F32), base)
    pos = base + rank_t
    pos_lo = jnp.min(jnp.where(sel_t > 0.5, pos, float(MOE_LOCAL_ROWS)), axis=0, keepdims=True)
    pos_hi = jnp.max(jnp.where(sel_t > 0.5, pos, -1.0), axis=0, keepdims=True)
    row = lax.broadcasted_iota(jnp.int32, (MOE_LOCAL_ROWS, tile), 0).astype(F32)
    onehot = jnp.where(row == pos_lo, 1.0, jnp.where(row == pos_hi, 1.0, 0.0)).astype(BF16)
    stage_ref[slot] = _dot(onehot, x_ref[...]).astype(BF16)

    def copies(step):
        return _group_copies(cnt_ref, local_ref, start_ref, step, stage_ref.at[step % 2], xs_ref,
                             sem.at[step % 2], to_slots=True)

    _start_all(copies(t))

    @pl.when(t > 0)
    def _():
        _wait_all(copies(t - 1))

    @pl.when(t == last)
    def _():
        _wait_all(copies(t))
        zero_ref[...] = jnp.zeros(zero_ref.shape, BF16)
        tails = []
        for e in range(N_EXPERTS):
            size = MOE_SLOT_TILE // 2
            while size >= MOE_ALIGN:
                off = tail_len_ref[e] & ~(2 * size - 1)
                dst = pl.multiple_of(tail_start_ref[e] + off, MOE_ALIGN)
                tails.append(((tail_len_ref[e] & size) != 0, pltpu.make_async_copy(
                    zero_ref.at[pl.ds(0, size)], xs_ref.at[pl.ds(dst, size)], tail_sem)))
                size //= 2
        _start_all(tails)
        _wait_all(tails)

        used_rows = tail_start_ref[N_EXPERTS - 1] + tail_len_ref[N_EXPERTS - 1]
        piece = zero_ref.shape[0]

        def unused_copy(k):
            dst = pl.multiple_of(used_rows + k * piece, piece)
            return pltpu.make_async_copy(zero_ref, xs_ref.at[pl.ds(dst, piece)], tail_sem)

        n_pieces = (xs_ref.shape[0] - used_rows) // piece
        lax.fori_loop(0, n_pieces, lambda k, c: (unused_copy(k).start(), c)[1], 0)
        lax.fori_loop(0, n_pieces, lambda k, c: (unused_copy(k).wait(), c)[1], 0)


def _moe_dispatch(xn, sel, plan, n_slot_tiles, tm=MOE_TOKEN_TILE):
    n, d = xn.shape
    upper = jnp.asarray(np.triu(np.ones((tm, tm), np.float32), 1), BF16)
    smem = pl.BlockSpec(memory_space=pltpu.SMEM)
    return pl.pallas_call(
        _moe_dispatch_kernel,
        grid=(n // tm,),
        in_specs=[smem, smem, smem, smem, smem,
                  pl.BlockSpec((tm, d), lambda i: (i, 0)),
                  pl.BlockSpec((tm, N_EXPERTS), lambda i: (i, 0)),
                  _resident((tm, tm))],
        out_specs=pl.BlockSpec(memory_space=pl.ANY),
        out_shape=jax.ShapeDtypeStruct((n_slot_tiles * MOE_SLOT_TILE, d), BF16),
        scratch_shapes=[pltpu.VMEM((2, MOE_LOCAL_ROWS, d), BF16),
                        pltpu.VMEM((MOE_SLOT_TILE // 2, d), BF16),
                        pltpu.SemaphoreType.DMA((2,)), pltpu.SemaphoreType.DMA(())],
        compiler_params=_params("arbitrary"),
        name="moe_dispatch",
    )(plan["cnt"], plan["local"], plan["start"], plan["tail_start"], plan["tail_len"], xn, sel, upper)


def _moe_experts_kernel(te_ref, nu_ref, x_ref, wg_ref, wu_ref, wd_ref, y_ref, acc_ref):
    i = pl.program_id(0)
    f = pl.program_id(1)
    last = pl.num_programs(1) - 1
    used = i < nu_ref[0]

    @pl.when(used & (f == 0))
    def _():
        acc_ref[...] = jnp.zeros(acc_ref.shape, F32)

    @pl.when(used)
    def _():
        x = x_ref[...]
        gate = _dot(x, wg_ref[...])
        up = _dot(x, wu_ref[...])
        hid = (gate * jax.nn.sigmoid(gate) * up).astype(BF16)
        acc_ref[...] += _dot(hid, wd_ref[...])

    @pl.when(used & (f == last))
    def _():
        y_ref[...] = acc_ref[...].astype(BF16)

    @pl.when(jnp.logical_not(used) & (f == last))
    def _():
        y_ref[...] = jnp.zeros(y_ref.shape, BF16)


def _moe_experts(xs, plan, wg, wu, wd, tf=512):
    rows, d = xs.shape
    d_ff = wg.shape[2]
    nf = d_ff // tf
    tm = MOE_SLOT_TILE
    f_idx = lambda i, f, nu: jnp.where(i < nu[0], f, nf - 1)
    return pl.pallas_call(
        _moe_experts_kernel,
        grid_spec=pltpu.PrefetchScalarGridSpec(
            num_scalar_prefetch=2,
            grid=(rows // tm, nf),
            in_specs=[pl.BlockSpec((tm, d), lambda i, f, te, nu: (jnp.minimum(i, nu[0] - 1), 0)),
                      pl.BlockSpec((None, d, tf), lambda i, f, te, nu: (te[i], 0, f_idx(i, f, nu))),
                      pl.BlockSpec((None, d, tf), lambda i, f, te, nu: (te[i], 0, f_idx(i, f, nu))),
                      pl.BlockSpec((None, tf, d), lambda i, f, te, nu: (te[i], f_idx(i, f, nu), 0))],
            out_specs=pl.BlockSpec((tm, d), lambda i, f, te, nu: (i, 0)),
            scratch_shapes=[pltpu.VMEM((tm, d), F32)]),
        out_shape=jax.ShapeDtypeStruct((rows, d), BF16),
        compiler_params=_params("arbitrary", "arbitrary"),
        name="moe_experts",
    )(plan["tile_expert"], plan["n_used"], xs, wg, wu, wd)


def _moe_combine_kernel(cnt_ref, local_ref, start_ref, h_ref, gates_ref, sel_ref, lower_ref, p_ref,
                        gp_ref, wpg_ref, wpp_ref, gfin_ref, ys_ref, out_ref, ybuf_ref, sem):
    t = pl.program_id(0)
    tile = h_ref.shape[0]

    @pl.when(t == 0)
    def _():
        ybuf_ref[...] = jnp.zeros(ybuf_ref.shape, BF16)

    copies = _group_copies(cnt_ref, local_ref, start_ref, t, ybuf_ref, ys_ref, sem, to_slots=False)
    _start_all(copies)

    sel = sel_ref[...] > 0.5
    gates = gates_ref[...]
    rank = _dot(lower_ref[...], sel_ref[...].astype(BF16))
    expert = lax.broadcasted_iota(jnp.int32, (1, N_EXPERTS), 1)
    base = jnp.zeros((1, N_EXPERTS), F32)
    for e in range(N_EXPERTS):
        base = jnp.where(expert == e, local_ref[t, e].astype(F32), base)
    pos = base + rank
    pos_lo = jnp.min(jnp.where(sel, pos, float(MOE_LOCAL_ROWS)), axis=1, keepdims=True)
    pos_hi = jnp.max(jnp.where(sel, pos, -1.0), axis=1, keepdims=True)
    gate_lo = jnp.sum(jnp.where(sel & (pos == pos_lo), gates, 0.0), axis=1, keepdims=True)
    gate_hi = jnp.sum(jnp.where(sel & (pos == pos_hi), gates, 0.0), axis=1, keepdims=True)
    col = lax.broadcasted_iota(jnp.int32, (tile, MOE_LOCAL_ROWS), 1).astype(F32)
    onehot_lo = jnp.where(col == pos_lo, 1.0, 0.0).astype(BF16)
    onehot_hi = jnp.where(col == pos_hi, 1.0, 0.0).astype(BF16)

    _wait_all(copies)
    y = ybuf_ref[...]
    h = h_ref[...] + gate_lo * _dot(onehot_lo, y) + gate_hi * _dot(onehot_hi, y)
    h = _ple(h, p_ref[...], gp_ref[...], wpg_ref[...], wpp_ref[...])
    out_ref[...] = _rms(h, gfin_ref[...])


def _moe_combine(h, gates, sel, ys, plan, p, gp, wpg, wpp, gfin, tm=MOE_TOKEN_TILE):
    n, d = h.shape
    lower = jnp.asarray(np.tril(np.ones((tm, tm), np.float32), -1), BF16)
    smem = pl.BlockSpec(memory_space=pltpu.SMEM)
    row = lambda w: pl.BlockSpec((tm, w), lambda i: (i, 0))
    return pl.pallas_call(
        _moe_combine_kernel,
        grid=(n // tm,),
        in_specs=[smem, smem, smem, row(d), row(N_EXPERTS), row(N_EXPERTS), _resident((tm, tm)),
                  row(p.shape[1]), _resident(gp.shape), _resident(wpg.shape), _resident(wpp.shape),
                  _resident(gfin.shape), pl.BlockSpec(memory_space=pl.ANY)],
        out_specs=row(d),
        out_shape=jax.ShapeDtypeStruct((n, d), F32),
        scratch_shapes=[pltpu.VMEM((MOE_LOCAL_ROWS, d), BF16), pltpu.SemaphoreType.DMA(())],
        compiler_params=_params("arbitrary"),
        name="moe_combine_ple_norm",
    )(plan["cnt"], plan["local"], plan["start"], h, gates, sel, lower, p, gp, wpg, wpp, gfin, ys)


def kernel(x, p, rel_bias, final_norm, e_norm_mix, e_w_in, e_conv_w, e_sinks, e_w_out, e_norm_ffn, e_ffn_gate, e_ffn_up, e_ffn_down, e_norm_ple, e_ple_gate, e_ple_proj, o_norm_mix, o_w_qkv, o_w_o, o_norm_ffn, o_router, o_exp_gate, o_exp_up, o_exp_down, o_norm_ple, o_ple_gate, o_ple_proj):
    bsz, seq, d = x.shape
    n = bsz * seq
    bf = lambda w: w.astype(BF16)
    h = x.reshape(n, d)
    p2 = p.reshape(p.shape[0], n, p.shape[-1])

    qi = np.arange(SWA_BLOCK)[:, None]
    kj = np.arange(2 * SWA_BLOCK)[None, :]
    swa_tab = _bias_table(rel_bias, _t5_bucket_np(qi - kj + SWA_BLOCK))
    kk = np.arange(MOBA_BLOCK)[:, None]
    qq = np.arange(MOBA_BLOCK)[None, :]
    own_bucket = np.where(qq >= kk, _t5_bucket_np(qq - kk), -1)
    moba_bucket = np.concatenate([_t5_bucket_np(qq - kk + MOBA_BLOCK), own_bucket], axis=1)
    moba_tab = _bias_table(rel_bias, moba_bucket, mult=LOG2E)

    a_out, qkv0 = _even_proj(h, e_norm_mix[0:1], bf(e_w_in[0]), e_conv_w[0], seq)
    b_out = _swa(qkv0, e_sinks[0], swa_tab, bsz, seq)
    h = _even_ffn(h, a_out, b_out, p2[0], bf(e_w_out[0]), e_norm_ffn[0:1],
                  bf(e_ffn_gate[0]), bf(e_ffn_up[0]), bf(e_ffn_down[0]),
                  e_norm_ple[0:1], bf(e_ple_gate[0]), bf(e_ple_proj[0]))

    w_qkv = o_w_qkv[0]
    q1, k1, vt1, kmean = _odd_proj(h, o_norm_mix[0:1], bf(w_qkv[:, :d]), bf(w_qkv[:, d:2 * d]),
                                   bf(w_qkv[:, 2 * d:].T))
    c_out = _moba(q1, k1, vt1, kmean, moba_tab, rel_bias, bsz, seq)
    h, xn, gates, sel, counts = _odd_router(h, c_out, bf(o_w_o[0]), o_norm_ffn[0:1], o_router[0])
    n_groups = (n // MOE_TOKEN_TILE) * N_EXPERTS
    max_rows = 2 * n + n_groups * (MOE_ALIGN - 1) + N_EXPERTS * (MOE_SLOT_TILE - 1)
    n_slot_tiles = -(-max_rows // MOE_SLOT_TILE) + 1
    plan = _moe_plan(counts, n_slot_tiles)
    xs = _moe_dispatch(xn, sel, plan, n_slot_tiles)
    ys = _moe_experts(xs, plan, bf(o_exp_gate[0]), bf(o_exp_up[0]), bf(o_exp_down[0]))
    out = _moe_combine(h, gates, sel, ys, plan, p2[1], o_norm_ple[0:1], bf(o_ple_gate[0]),
                       bf(o_ple_proj[0]), final_norm.reshape(1, d))
    return out.reshape(bsz, seq, d)
```

```python
import functools
import math

import jax
import jax.numpy as jnp
import numpy as np
from jax import lax
from jax.experimental import pallas as pl
from jax.experimental.pallas import tpu as pltpu

F32 = jnp.float32
BF16 = jnp.bfloat16

EPS = 1e-6
N_HEADS = 8
N_BUCKETS = 32
MAX_DISTANCE = 128

SC_WIDTH = 512
CONV_WIDTH = 3
SWA_KV_HEADS = 2
SWA_HEAD_DIM = 64
SWA_BLOCK = 128
SWA_Q_WIDTH = N_HEADS * SWA_HEAD_DIM
SWA_KV_WIDTH = SWA_KV_HEADS * SWA_HEAD_DIM

MOBA_HEAD_DIM = 128
MOBA_BLOCK = 256
MOBA_TOPK = 3
MOBA_FAR_GROUP = 4

N_EXPERTS = 8
MOE_TOKEN_TILE = 512
MOE_SLOT_TILE = 512
MOE_ALIGN = 16
MOE_LOCAL_ROWS = -(-(2 * MOE_TOKEN_TILE + N_EXPERTS * (MOE_ALIGN - 1)) // 128) * 128

VMEM_LIMIT_BYTES = 56 * 1024 * 1024
NEG_INF = float("-inf")
LOG2E = math.log2(math.e)


def _params(*semantics):
    return pltpu.CompilerParams(dimension_semantics=semantics,
                                vmem_limit_bytes=VMEM_LIMIT_BYTES)


def _resident(shape):
    zeros = (0,) * len(shape)
    return pl.BlockSpec(shape, lambda *_: zeros, pipeline_mode=pl.Buffered(1))


def _rms(x, g):
    return x * lax.rsqrt(jnp.mean(x * x, axis=-1, keepdims=True) + EPS) * g


def _dot(a, b):
    return jnp.dot(a, b, preferred_element_type=F32)


def _dot_nt(a, b):
    return lax.dot_general(a, b, (((1,), (1,)), ((), ())), preferred_element_type=F32)


def _t5_bucket_np(dist):
    d = np.maximum(dist, 0).astype(np.int32)
    max_exact = N_BUCKETS // 2
    scaled = (np.log(np.maximum(d, max_exact).astype(np.float32) / np.float32(max_exact))
              / np.float32(math.log(MAX_DISTANCE / max_exact)))
    large = np.minimum(max_exact + (scaled * (N_BUCKETS - max_exact)).astype(np.int32),
                       N_BUCKETS - 1)
    return np.where(d < max_exact, d, large).astype(np.int32)


def _bias_table_kernel(rb_ref, bkt_ref, out_ref, *, mult):
    h = pl.program_id(0)
    bkt = bkt_ref[...]
    acc = jnp.full(bkt.shape, NEG_INF, F32)
    for b in range(N_BUCKETS):
        acc = jnp.where(bkt == b, rb_ref[b, h], acc)
    out_ref[...] = acc * mult


def _bias_table(rel_bias, bucket, mult=1.0):
    rows, cols = bucket.shape
    return pl.pallas_call(
        functools.partial(_bias_table_kernel, mult=mult),
        grid=(N_HEADS,),
        in_specs=[pl.BlockSpec(memory_space=pltpu.SMEM),
                  pl.BlockSpec((rows, cols), lambda h: (0, 0))],
        out_specs=pl.BlockSpec((None, rows, cols), lambda h: (h, 0, 0)),
        out_shape=jax.ShapeDtypeStruct((N_HEADS, rows, cols), F32),
        compiler_params=_params("arbitrary"),
        name="bias_table",
    )(rel_bias, jnp.asarray(bucket))


def _even_proj_kernel(x_ref, g_ref, w_ref, cw_ref, a_ref, qkv_ref, cu_scr, *, tm, tiles_per_seq):
    i = pl.program_id(0)
    xn = _rms(x_ref[...], g_ref[...]).astype(BF16)
    b_gate = _dot(xn, w_ref[:, 0:SC_WIDTH])
    c_gate = _dot(xn, w_ref[:, SC_WIDTH:2 * SC_WIDTH])
    u = _dot(xn, w_ref[:, 2 * SC_WIDTH:3 * SC_WIDTH])
    cu = c_gate * u

    @pl.when(i % tiles_per_seq == 0)
    def _():
        cu_scr[0:8, :] = jnp.zeros((8, SC_WIDTH), F32)

    @pl.when(i % tiles_per_seq != 0)
    def _():
        cu_scr[0:8, :] = cu_scr[tm:tm + 8, :]

    cu_scr[8:8 + tm, :] = cu
    y = (cw_ref[0:1, :] * cu_scr[6:6 + tm, :] + cw_ref[1:2, :] * cu_scr[7:7 + tm, :]
         + cw_ref[2:3, :] * cu)
    a_ref[...] = (b_gate * y).astype(BF16)
    qkv_ref[...] = _dot(xn, w_ref[:, 3 * SC_WIDTH:]).astype(BF16)


def _even_proj(x, g, w, conv_w, seq, tm=512):
    n, d = x.shape
    d_in = w.shape[1]
    d_qkv = d_in - 3 * SC_WIDTH
    return pl.pallas_call(
        functools.partial(_even_proj_kernel, tm=tm, tiles_per_seq=seq // tm),
        grid=(n // tm,),
        in_specs=[pl.BlockSpec((tm, d), lambda i: (i, 0)),
                  _resident((1, d)), _resident((d, d_in)), _resident((CONV_WIDTH, SC_WIDTH))],
        out_specs=[pl.BlockSpec((tm, SC_WIDTH), lambda i: (i, 0)),
                   pl.BlockSpec((tm, d_qkv), lambda i: (i, 0))],
        out_shape=[jax.ShapeDtypeStruct((n, SC_WIDTH), BF16),
                   jax.ShapeDtypeStruct((n, d_qkv), BF16)],
        scratch_shapes=[pltpu.VMEM((tm + 8, SC_WIDTH), F32)],
        compiler_params=_params("arbitrary"),
        name="even_proj_conv",
    )(x, g, w, conv_w)


def _swa_kernel(sink_ref, cur_ref, prev_ref, tab_ref, out_ref):
    i = pl.program_id(1)
    blk = SWA_BLOCK
    hd = SWA_HEAD_DIM
    grp = N_HEADS // SWA_KV_HEADS
    qi = lax.broadcasted_iota(jnp.int32, (blk, 2 * blk), 0)
    kj = lax.broadcasted_iota(jnp.int32, (blk, 2 * blk), 1)
    diff = qi - kj + blk
    valid = (diff >= 0) & (diff < blk) & ((kj >= blk) | (i > 0))
    scale = hd ** -0.5
    outs = []
    for g in range(SWA_KV_HEADS):
        k_cat = jnp.concatenate([prev_ref[:, g * hd:(g + 1) * hd],
                                 cur_ref[:, SWA_Q_WIDTH + g * hd:SWA_Q_WIDTH + (g + 1) * hd]], axis=0)
        v_cat = jnp.concatenate(
            [prev_ref[:, SWA_KV_WIDTH + g * hd:SWA_KV_WIDTH + (g + 1) * hd],
             cur_ref[:, SWA_Q_WIDTH + SWA_KV_WIDTH + g * hd:SWA_Q_WIDTH + SWA_KV_WIDTH + (g + 1) * hd]],
            axis=0)
        for hh in range(grp):
            h = g * grp + hh
            q = cur_ref[:, h * hd:(h + 1) * hd]
            s = _dot_nt(q, k_cat) * scale + tab_ref[h]
            s = jnp.where(valid, s, NEG_INF)
            sink = sink_ref[h]
            m = jnp.maximum(jnp.max(s, axis=-1, keepdims=True), sink)
            p = jnp.exp(s - m)
            denom = jnp.sum(p, axis=-1, keepdims=True) + jnp.exp(sink - m)
            o = _dot(p.astype(BF16), v_cat)
            outs.append(o / denom)
    out_ref[...] = jnp.concatenate(outs, axis=-1).astype(BF16)


def _swa(qkv, sinks, tab, bsz, seq):
    n = qkv.shape[0]
    nb = seq // SWA_BLOCK
    kv_col_block = SWA_Q_WIDTH // (2 * SWA_KV_WIDTH)
    return pl.pallas_call(
        _swa_kernel,
        grid=(bsz, nb),
        in_specs=[pl.BlockSpec(memory_space=pltpu.SMEM),
                  pl.BlockSpec((SWA_BLOCK, qkv.shape[1]), lambda b, i: (b * nb + i, 0)),
                  pl.BlockSpec((SWA_BLOCK, 2 * SWA_KV_WIDTH),
                               lambda b, i: (b * nb + jnp.maximum(i - 1, 0), kv_col_block)),
                  _resident((N_HEADS, SWA_BLOCK, 2 * SWA_BLOCK))],
        out_specs=pl.BlockSpec((SWA_BLOCK, SWA_Q_WIDTH), lambda b, i: (b * nb + i, 0)),
        out_shape=jax.ShapeDtypeStruct((n, SWA_Q_WIDTH), BF16),
        compiler_params=_params("arbitrary", "arbitrary"),
        name="swa",
    )(sinks, qkv, qkv, tab)


def _ple(h, p, g, w_gate, w_proj):
    gate = jax.nn.sigmoid(_dot(_rms(h, g).astype(BF16), w_gate))
    return h + gate * _dot(p.astype(BF16), w_proj)


def _even_ffn_kernel(h_ref, a_ref, b_ref, p_ref, wo_ref, gf_ref, wg_ref, wu_ref, wd_ref,
                     gp_ref, wpg_ref, wpp_ref, out_ref, *, tf):
    h1 = (h_ref[...] + _dot(a_ref[...], wo_ref[0:SC_WIDTH, :])
          + _dot(b_ref[...], wo_ref[SC_WIDTH:, :]))
    xn = _rms(h1, gf_ref[...]).astype(BF16)
    d_ff = wg_ref.shape[1]
    acc = jnp.zeros(h1.shape, F32)
    for c in range(d_ff // tf):
        gate = _dot(xn, wg_ref[:, c * tf:(c + 1) * tf])
        up = _dot(xn, wu_ref[:, c * tf:(c + 1) * tf])
        hid = (gate * jax.nn.sigmoid(gate) * up).astype(BF16)
        acc = acc + _dot(hid, wd_ref[c * tf:(c + 1) * tf, :])
    h2 = h1 + acc
    out_ref[...] = _ple(h2, p_ref[...], gp_ref[...], wpg_ref[...], wpp_ref[...])


def _even_ffn(h, a, b, p, wo, gf, wg, wu, wd, gp, wpg, wpp, tm=512, tf=256):
    n, d = h.shape
    row = lambda w: pl.BlockSpec((tm, w), lambda i: (i, 0))
    return pl.pallas_call(
        functools.partial(_even_ffn_kernel, tf=tf),
        grid=(n // tm,),
        in_specs=[row(d), row(a.shape[1]), row(b.shape[1]), row(p.shape[1]),
                  _resident(wo.shape), _resident(gf.shape), _resident(wg.shape),
                  _resident(wu.shape), _resident(wd.shape), _resident(gp.shape),
                  _resident(wpg.shape), _resident(wpp.shape)],
        out_specs=row(d),
        out_shape=jax.ShapeDtypeStruct((n, d), F32),
        compiler_params=_params("arbitrary"),
        name="even_out_ffn_ple",
    )(h, a, b, p, wo, gf, wg, wu, wd, gp, wpg, wpp)


def _odd_proj_kernel(h_ref, g_ref, wq_ref, wk_ref, wvt_ref, q_ref, k_ref, vt_ref, kmean_ref, *, tm):
    xn = _rms(h_ref[...], g_ref[...]).astype(BF16)
    q_ref[...] = (_dot(xn, wq_ref[...]) * (MOBA_HEAD_DIM ** -0.5 * LOG2E)).astype(BF16)
    k = _dot(xn, wk_ref[...])
    k_ref[...] = k.astype(BF16)
    vt_ref[...] = _dot_nt(wvt_ref[...], xn).astype(BF16)
    for r in range(tm // MOBA_BLOCK):
        kmean_ref[r] = jnp.mean(k[r * MOBA_BLOCK:(r + 1) * MOBA_BLOCK, :], axis=0, keepdims=True)


def _odd_proj(h, g, wq, wk, wvt, tm=512):
    n, d = h.shape
    bpt = tm // MOBA_BLOCK
    row = pl.BlockSpec((tm, d), lambda i: (i, 0))
    return pl.pallas_call(
        functools.partial(_odd_proj_kernel, tm=tm),
        grid=(n // tm,),
        in_specs=[row, _resident(g.shape), _resident(wq.shape), _resident(wk.shape),
                  _resident(wvt.shape)],
        out_specs=[row, row, pl.BlockSpec((d, tm), lambda i: (0, i)),
                   pl.BlockSpec((bpt, 1, d), lambda i: (i, 0, 0))],
        out_shape=[jax.ShapeDtypeStruct((n, d), BF16), jax.ShapeDtypeStruct((n, d), BF16),
                   jax.ShapeDtypeStruct((d, n), BF16),
                   jax.ShapeDtypeStruct((n // MOBA_BLOCK, 1, d), F32)],
        compiler_params=_params("arbitrary"),
        name="odd_qkv",
    )(h, g, wq, wk, wvt)


def _moba_kernel(rb_ref, q_ref, k_ref, vt_ref, km_ref, tab_ref, out_ref,
                 sel_scr, adj_scr, m_scr, l_scr, acc_scr, far_a_scr, far_b_scr, gmax_scr, *, nblk, hg):
    head0 = pl.program_id(1) * hg
    qi = pl.program_id(2)
    bs = MOBA_BLOCK
    hd = MOBA_HEAD_DIM
    heads = [slice(hh * hd, (hh + 1) * hd) for hh in range(hg)]

    blk = lax.broadcasted_iota(jnp.int32, (nblk, bs), 0)
    past = blk < qi
    for hh in range(hg):
        gate = lax.dot_general(km_ref[:, heads[hh]], q_ref[:, heads[hh]].astype(F32),
                               (((1,), (1,)), ((), ())), preferred_element_type=F32,
                               precision=lax.Precision.HIGHEST)
        g = jnp.where(past, gate, NEG_INF)
        sel = jnp.zeros(gate.shape, jnp.bool_)
        for _ in range(MOBA_TOPK):
            top = jnp.max(g, axis=0, keepdims=True)
            idx = jnp.min(jnp.where(g == top, blk, nblk), axis=0, keepdims=True)
            hit = blk == idx
            sel = sel | hit
            g = jnp.where(hit, NEG_INF, g)
        far_bias = rb_ref[N_BUCKETS - 1, head0 + hh] * LOG2E
        sel_scr[hh] = jnp.where(sel & (blk < qi - 1), far_bias, NEG_INF)
        adj_scr[hh] = jnp.max(jnp.where(sel & (blk == qi - 1), 0.0, NEG_INF), axis=0, keepdims=True)

    def score_dots(start, nb):
        return [_dot_nt(k_ref[pl.ds(start, nb * bs), heads[hh]], q_ref[:, heads[hh]])
                for hh in range(hg)]

    def softmax_update(scores, start, nb, tables, query_biases, first):
        probs, alphas = [], []
        for hh in range(hg):
            parts, s_max = [], None
            for r in range(nb):
                s_r = scores[hh][r * bs:(r + 1) * bs]
                if tables(hh, r) is not None:
                    s_r = s_r + tables(hh, r)
                col_max = jnp.max(s_r, axis=0, keepdims=True)
                if query_biases(hh, r) is not None:
                    col_max = col_max + query_biases(hh, r)
                parts.append(s_r)
                s_max = col_max if s_max is None else jnp.maximum(s_max, col_max)
            m_new = s_max if first else jnp.maximum(m_scr[hh], s_max)
            p = jnp.concatenate(
                [jnp.exp2(parts[r] - (m_new if query_biases(hh, r) is None
                                      else m_new - query_biases(hh, r))) for r in range(nb)], axis=0)
            l_new = jnp.sum(p, axis=0, keepdims=True)
            if not first:
                alpha = jnp.exp2(m_scr[hh] - m_new)
                alphas.append(alpha)
                l_new = alpha * l_scr[hh] + l_new
            m_scr[hh] = m_new
            l_scr[hh] = l_new
            probs.append(p.astype(BF16))
        for hh in range(hg):
            acc_new = _dot(vt_ref[heads[hh], pl.ds(start, nb * bs)], probs[hh])
            if not first:
                acc_new = alphas[hh] * acc_scr[hh] + acc_new
            acc_scr[hh] = acc_new

    @pl.when(qi == 0)
    def _():
        softmax_update(score_dots(0, 1), 0, 1, lambda hh, r: tab_ref[hh, :, bs:2 * bs],
                       lambda hh, r: None, first=True)

    far_rows = MOBA_FAR_GROUP * bs
    n_far = (qi - 1 + MOBA_FAR_GROUP - 1) // MOBA_FAR_GROUP
    far_bufs = (far_a_scr, far_b_scr)

    def far_start(g):
        return pl.multiple_of(g * far_rows, far_rows)

    def far_bias(hh, g, r):
        return sel_scr[hh, pl.ds(g * MOBA_FAR_GROUP + r, 1), :]

    def far_scores(g, buf):
        scores = score_dots(far_start(g), MOBA_FAR_GROUP)
        for hh in range(hg):
            far_bufs[buf][hh] = scores[hh]
            group_max = None
            for r in range(MOBA_FAR_GROUP):
                col_max = (jnp.max(scores[hh][r * bs:(r + 1) * bs], axis=0, keepdims=True)
                           + far_bias(hh, g, r))
                group_max = col_max if group_max is None else jnp.maximum(group_max, col_max)
            gmax_scr[buf, hh] = group_max

    def far_softmax(g, buf):
        probs, alphas = [], []
        for hh in range(hg):
            m_old = m_scr[hh]
            m_new = jnp.maximum(m_old, gmax_scr[buf, hh])
            alpha = jnp.exp2(m_old - m_new)
            p = jnp.concatenate(
                [jnp.exp2(far_bufs[buf][hh, r * bs:(r + 1) * bs, :] - (m_new - far_bias(hh, g, r)))
                 for r in range(MOBA_FAR_GROUP)], axis=0)
            m_scr[hh] = m_new
            l_scr[hh] = alpha * l_scr[hh] + jnp.sum(p, axis=0, keepdims=True)
            probs.append(p.astype(BF16))
            alphas.append(alpha)
        for hh in range(hg):
            acc_scr[hh] = (alphas[hh] * acc_scr[hh]
                           + _dot(vt_ref[heads[hh], pl.ds(far_start(g), far_rows)], probs[hh]))

    def far_step(g, buf, prefetch):
        if prefetch:
            far_scores(g + 1, 1 - buf)
        far_softmax(g, buf)

    @pl.when(qi > 0)
    def _():
        start = pl.multiple_of((qi - 1) * bs, bs)
        scores = score_dots(start, 2)
        far_scores(0, 0)
        softmax_update(scores, start, 2, lambda hh, r: tab_ref[hh, :, r * bs:(r + 1) * bs],
                       lambda hh, r: adj_scr[hh] if r == 0 else None, first=True)

    def far_pair(i, carry):
        far_step(2 * i, 0, True)
        far_step(2 * i + 1, 1, True)
        return carry

    lax.fori_loop(0, (n_far - 1) // 2, far_pair, 0)

    @pl.when((n_far > 0) & (n_far % 2 == 0))
    def _():
        far_step(n_far - 2, 0, True)
        far_step(n_far - 1, 1, False)

    @pl.when(n_far % 2 == 1)
    def _():
        far_step(n_far - 1, 0, False)

    for hh in range(hg):
        out_ref[:, heads[hh]] = (acc_scr[hh] / l_scr[hh]).T.astype(BF16)


def _moba(q, k, vt, kmean, tab, rel_bias, bsz, seq, hg=4):
    n, d = q.shape
    nblk = seq // MOBA_BLOCK
    assert nblk % MOBA_FAR_GROUP == 0, "far-block groups must not run past the sequence"
    bs = MOBA_BLOCK
    w = hg * MOBA_HEAD_DIM
    return pl.pallas_call(
        functools.partial(_moba_kernel, nblk=nblk, hg=hg),
        grid=(bsz, N_HEADS // hg, nblk),
        in_specs=[pl.BlockSpec(memory_space=pltpu.SMEM),
                  pl.BlockSpec((bs, w), lambda b, g, i: (b * nblk + i, g)),
                  pl.BlockSpec((seq, w), lambda b, g, i: (b, g)),
                  pl.BlockSpec((w, seq), lambda b, g, i: (g, b)),
                  pl.BlockSpec((None, nblk, w), lambda b, g, i: (b, 0, g)),
                  pl.BlockSpec((hg, bs, 2 * bs), lambda b, g, i: (g, 0, 0))],
        out_specs=pl.BlockSpec((bs, w), lambda b, g, i: (b * nblk + i, g)),
        out_shape=jax.ShapeDtypeStruct((n, d), BF16),
        scratch_shapes=[pltpu.VMEM((hg, nblk, bs), F32), pltpu.VMEM((hg, 1, bs), F32),
                        pltpu.VMEM((hg, 1, bs), F32), pltpu.VMEM((hg, 1, bs), F32),
                        pltpu.VMEM((hg, MOBA_HEAD_DIM, bs), F32),
                        pltpu.VMEM((hg, MOBA_FAR_GROUP * bs, bs), F32),
                        pltpu.VMEM((hg, MOBA_FAR_GROUP * bs, bs), F32),
                        pltpu.VMEM((2, hg, 1, bs), F32)],
        compiler_params=_params("arbitrary", "arbitrary", "arbitrary"),
        name="moba",
    )(rel_bias, q, k, vt, kmean.reshape(bsz, nblk, d), tab)


def _odd_router_kernel(h_ref, c_ref, wo_ref, g_ref, wr_ref, h_out_ref, xn_ref, gates_ref, sel_ref,
                       cnt_ref):
    h1 = h_ref[...] + _dot(c_ref[...], wo_ref[...])
    h_out_ref[...] = h1
    xn = _rms(h1, g_ref[...])
    xn_hi = xn.astype(BF16)
    xn_ref[...] = xn_hi
    xn_lo = (xn - xn_hi.astype(F32)).astype(BF16)
    wr = wr_ref[...]
    wr_hi = wr.astype(BF16)
    wr_lo = (wr - wr_hi.astype(F32)).astype(BF16)
    logits = _dot(xn_hi, wr_hi) + (_dot(xn_lo, wr_hi) + _dot(xn_hi, wr_lo))
    lane = lax.broadcasted_iota(jnp.int32, logits.shape, 1)
    v1 = jnp.max(logits, axis=-1, keepdims=True)
    i1 = jnp.min(jnp.where(logits == v1, lane, N_EXPERTS), axis=-1, keepdims=True)
    rest = jnp.where(lane == i1, NEG_INF, logits)
    v2 = jnp.max(rest, axis=-1, keepdims=True)
    i2 = jnp.min(jnp.where(rest == v2, lane, N_EXPERTS), axis=-1, keepdims=True)
    e2 = jnp.exp(v2 - v1)
    w1 = 1.0 / (1.0 + e2)
    w2 = e2 / (1.0 + e2)
    gates_ref[...] = jnp.where(lane == i1, w1, 0.0) + jnp.where(lane == i2, w2, 0.0)
    sel = jnp.where((lane == i1) | (lane == i2), 1.0, 0.0)
    sel_ref[...] = sel
    cnt_ref[0] = jnp.sum(sel, axis=0, keepdims=True)


def _odd_router(h, c, wo, g, wr, tm=MOE_TOKEN_TILE):
    n, d = h.shape
    row = lambda w: pl.BlockSpec((tm, w), lambda i: (i, 0))
    return pl.pallas_call(
        _odd_router_kernel,
        grid=(n // tm,),
        in_specs=[row(d), row(d), _resident(wo.shape), _resident(g.shape), _resident(wr.shape)],
        out_specs=[row(d), row(d), row(N_EXPERTS), row(N_EXPERTS),
                   pl.BlockSpec((1, 1, N_EXPERTS), lambda i: (i, 0, 0))],
        out_shape=[jax.ShapeDtypeStruct((n, d), F32), jax.ShapeDtypeStruct((n, d), BF16),
                   jax.ShapeDtypeStruct((n, N_EXPERTS), F32),
                   jax.ShapeDtypeStruct((n, N_EXPERTS), F32),
                   jax.ShapeDtypeStruct((n // tm, 1, N_EXPERTS), F32)],
        compiler_params=_params("arbitrary"),
        name="odd_out_router",
    )(h, c, wo, g, wr)


def _moe_plan(counts, n_slot_tiles):
    cnt = counts.reshape(-1, N_EXPERTS).astype(jnp.int32)
    grp = (cnt + MOE_ALIGN - 1) // MOE_ALIGN * MOE_ALIGN
    local = jnp.cumsum(grp, axis=1) - grp
    tot = jnp.sum(grp, axis=0)
    region = (tot + MOE_SLOT_TILE - 1) // MOE_SLOT_TILE * MOE_SLOT_TILE
    region_end = jnp.cumsum(region)
    region_start = region_end - region
    start = region_start[None, :] + jnp.cumsum(grp, axis=0) - grp
    n_used = region_end[-1] // MOE_SLOT_TILE
    tile_row = jnp.arange(n_slot_tiles, dtype=jnp.int32) * MOE_SLOT_TILE
    tile_expert = jnp.searchsorted(region_end, tile_row, side="right").astype(jnp.int32)
    tile_expert = jnp.minimum(tile_expert, N_EXPERTS - 1)
    tile_expert = jnp.where(tile_row < region_end[-1], tile_expert, tile_expert[n_used - 1])
    return dict(cnt=cnt, local=local.astype(jnp.int32), start=start.astype(jnp.int32),
                tail_start=(region_start + tot).astype(jnp.int32),
                tail_len=(region - tot).astype(jnp.int32), tile_expert=tile_expert,
                n_used=n_used.reshape(1).astype(jnp.int32))


def _group_copies(cnt_ref, local_ref, start_ref, t, local_buf, slot_array, sem, to_slots):
    copies = []
    for e in range(N_EXPERTS):
        rows = (cnt_ref[t, e] + MOE_ALIGN - 1) // MOE_ALIGN * MOE_ALIGN
        size = MOE_TOKEN_TILE
        while size >= MOE_ALIGN:
            off = rows & ~(2 * size - 1)
            local = local_buf.at[pl.ds(pl.multiple_of(local_ref[t, e] + off, MOE_ALIGN), size)]
            slots = slot_array.at[pl.ds(pl.multiple_of(start_ref[t, e] + off, MOE_ALIGN), size)]
            src, dst = (local, slots) if to_slots else (slots, local)
            copies.append(((rows & size) != 0, pltpu.make_async_copy(src, dst, sem)))
            size //= 2
    return copies


def _start_all(copies):
    for cond, cp in copies:
        pl.when(cond)(cp.start)


def _wait_all(copies):
    for cond, cp in copies:
        pl.when(cond)(cp.wait)


def _moe_dispatch_kernel(cnt_ref, local_ref, start_ref, tail_start_ref, tail_len_ref, x_ref, sel_ref,
                         upper_ref, xs_ref, stage_ref, zero_ref, sem, tail_sem):
    t = pl.program_id(0)
    last = pl.num_programs(0) - 1
    tile = x_ref.shape[0]
    slot = t % 2
    eye = jnp.where(lax.broadcasted_iota(jnp.int32, (N_EXPERTS, N_EXPERTS), 0)
                    == lax.broadcasted_iota(jnp.int32, (N_EXPERTS, N_EXPERTS), 1), 1.0, 0.0)
    sel_t = _dot_nt(eye.astype(BF16), sel_ref[...].astype(BF16))
    rank_t = _dot(sel_t.astype(BF16), upper_ref[...])
    expert = lax.broadcasted_iota(jnp.int32, (N_EXPERTS, 1), 0)
    base = jnp.zeros((N_EXPERTS, 1), F32)
    for e in range(N_EXPERTS):
        base = jnp.where(expert == e, local_ref[t, e].astype(F32), base)
    pos = base + rank_t
    pos_lo = jnp.min(jnp.where(sel_t > 0.5, pos, float(MOE_LOCAL_ROWS)), axis=0, keepdims=True)
    pos_hi = jnp.max(jnp.where(sel_t > 0.5, pos, -1.0), axis=0, keepdims=True)
    row = lax.broadcasted_iota(jnp.int32, (MOE_LOCAL_ROWS, tile), 0).astype(F32)
    onehot = jnp.where(row == pos_lo, 1.0, jnp.where(row == pos_hi, 1.0, 0.0)).astype(BF16)
    stage_ref[slot] = _dot(onehot, x_ref[...]).astype(BF16)

    def copies(step):
        return _group_copies(cnt_ref, local_ref, start_ref, step, stage_ref.at[step % 2], xs_ref,
                             sem.at[step % 2], to_slots=True)

    _start_all(copies(t))

    @pl.when(t > 0)
    def _():
        _wait_all(copies(t - 1))

    @pl.when(t == last)
    def _():
        _wait_all(copies(t))
        zero_ref[...] = jnp.zeros(zero_ref.shape, BF16)
        tails = []
        for e in range(N_EXPERTS):
            size = MOE_SLOT_TILE // 2
            while size >= MOE_ALIGN:
                off = tail_len_ref[e] & ~(2 * size - 1)
                dst = pl.multiple_of(tail_start_ref[e] + off, MOE_ALIGN)
                tails.append(((tail_len_ref[e] & size) != 0, pltpu.make_async_copy(
                    zero_ref.at[pl.ds(0, size)], xs_ref.at[pl.ds(dst, size)], tail_sem)))
                size //= 2
        _start_all(tails)
        _wait_all(tails)

        used_rows = tail_start_ref[N_EXPERTS - 1] + tail_len_ref[N_EXPERTS - 1]
        piece = zero_ref.shape[0]

        def unused_copy(k):
            dst = pl.multiple_of(used_rows + k * piece, piece)
            return pltpu.make_async_copy(zero_ref, xs_ref.at[pl.ds(dst, piece)], tail_sem)

        n_pieces = (xs_ref.shape[0] - used_rows) // piece
        lax.fori_loop(0, n_pieces, lambda k, c: (unused_copy(k).start(), c)[1], 0)
        lax.fori_loop(0, n_pieces, lambda k, c: (unused_copy(k).wait(), c)[1], 0)


def _moe_dispatch(xn, sel, plan, n_slot_tiles, tm=MOE_TOKEN_TILE):
    n, d = xn.shape
    upper = jnp.asarray(np.triu(np.ones((tm, tm), np.float32), 1), BF16)
    smem = pl.BlockSpec(memory_space=pltpu.SMEM)
    return pl.pallas_call(
        _moe_dispatch_kernel,
        grid=(n // tm,),
        in_specs=[smem, smem, smem, smem, smem,
                  pl.BlockSpec((tm, d), lambda i: (i, 0)),
                  pl.BlockSpec((tm, N_EXPERTS), lambda i: (i, 0)),
                  _resident((tm, tm))],
        out_specs=pl.BlockSpec(memory_space=pl.ANY),
        out_shape=jax.ShapeDtypeStruct((n_slot_tiles * MOE_SLOT_TILE, d), BF16),
        scratch_shapes=[pltpu.VMEM((2, MOE_LOCAL_ROWS, d), BF16),
                        pltpu.VMEM((MOE_SLOT_TILE // 2, d), BF16),
                        pltpu.SemaphoreType.DMA((2,)), pltpu.SemaphoreType.DMA(())],
        compiler_params=_params("arbitrary"),
        name="moe_dispatch",
    )(plan["cnt"], plan["local"], plan["start"], plan["tail_start"], plan["tail_len"], xn, sel, upper)


def _moe_experts_kernel(te_ref, nu_ref, x_ref, wg_ref, wu_ref, wd_ref, y_ref, acc_ref):
    i = pl.program_id(0)
    f = pl.program_id(1)
    last = pl.num_programs(1) - 1
    used = i < nu_ref[0]

    @pl.when(used & (f == 0))
    def _():
        acc_ref[...] = jnp.zeros(acc_ref.shape, F32)

    @pl.when(used)
    def _():
        x = x_ref[...]
        gate = _dot(x, wg_ref[...])
        up = _dot(x, wu_ref[...])
        hid = (gate * jax.nn.sigmoid(gate) * up).astype(BF16)
        acc_ref[...] += _dot(hid, wd_ref[...])

    @pl.when(used & (f == last))
    def _():
        y_ref[...] = acc_ref[...].astype(BF16)

    @pl.when(jnp.logical_not(used) & (f == last))
    def _():
        y_ref[...] = jnp.zeros(y_ref.shape, BF16)


def _moe_experts(xs, plan, wg, wu, wd, tf=512):
    rows, d = xs.shape
    d_ff = wg.shape[2]
    nf = d_ff // tf
    tm = MOE_SLOT_TILE
    f_idx = lambda i, f, nu: jnp.where(i < nu[0], f, nf - 1)
    return pl.pallas_call(
        _moe_experts_kernel,
        grid_spec=pltpu.PrefetchScalarGridSpec(
            num_scalar_prefetch=2,
            grid=(rows // tm, nf),
            in_specs=[pl.BlockSpec((tm, d), lambda i, f, te, nu: (jnp.minimum(i, nu[0] - 1), 0)),
                      pl.BlockSpec((None, d, tf), lambda i, f, te, nu: (te[i], 0, f_idx(i, f, nu))),
                      pl.BlockSpec((None, d, tf), lambda i, f, te, nu: (te[i], 0, f_idx(i, f, nu))),
                      pl.BlockSpec((None, tf, d), lambda i, f, te, nu: (te[i], f_idx(i, f, nu), 0))],
            out_specs=pl.BlockSpec((tm, d), lambda i, f, te, nu: (i, 0)),
            scratch_shapes=[pltpu.VMEM((tm, d), F32)]),
        out_shape=jax.ShapeDtypeStruct((rows, d), BF16),
        compiler_params=_params("arbitrary", "arbitrary"),
        name="moe_experts",
    )(plan["tile_expert"], plan["n_used"], xs, wg, wu, wd)


def _moe_combine_kernel(cnt_ref, local_ref, start_ref, h_ref, gates_ref, sel_ref, lower_ref, p_ref,
                        gp_ref, wpg_ref, wpp_ref, gfin_ref, ys_ref, out_ref, ybuf_ref, sem):
    t = pl.program_id(0)
    tile = h_ref.shape[0]

    @pl.when(t == 0)
    def _():
        ybuf_ref[...] = jnp.zeros(ybuf_ref.shape, BF16)

    copies = _group_copies(cnt_ref, local_ref, start_ref, t, ybuf_ref, ys_ref, sem, to_slots=False)
    _start_all(copies)

    sel = sel_ref[...] > 0.5
    gates = gates_ref[...]
    rank = _dot(lower_ref[...], sel_ref[...].astype(BF16))
    expert = lax.broadcasted_iota(jnp.int32, (1, N_EXPERTS), 1)
    base = jnp.zeros((1, N_EXPERTS), F32)
    for e in range(N_EXPERTS):
        base = jnp.where(expert == e, local_ref[t, e].astype(F32), base)
    pos = base + rank
    pos_lo = jnp.min(jnp.where(sel, pos, float(MOE_LOCAL_ROWS)), axis=1, keepdims=True)
    pos_hi = jnp.max(jnp.where(sel, pos, -1.0), axis=1, keepdims=True)
    gate_lo = jnp.sum(jnp.where(sel & (pos == pos_lo), gates, 0.0), axis=1, keepdims=True)
    gate_hi = jnp.sum(jnp.where(sel & (pos == pos_hi), gates, 0.0), axis=1, keepdims=True)
    col = lax.broadcasted_iota(jnp.int32, (tile, MOE_LOCAL_ROWS), 1).astype(F32)
    onehot_lo = jnp.where(col == pos_lo, 1.0, 0.0).astype(BF16)
    onehot_hi = jnp.where(col == pos_hi, 1.0, 0.0).astype(BF16)

    _wait_all(copies)
    y = ybuf_ref[...]
    h = h_ref[...] + gate_lo * _dot(onehot_lo, y) + gate_hi * _dot(onehot_hi, y)
    h = _ple(h, p_ref[...], gp_ref[...], wpg_ref[...], wpp_ref[...])
    out_ref[...] = _rms(h, gfin_ref[...])


def _moe_combine(h, gates, sel, ys, plan, p, gp, wpg, wpp, gfin, tm=MOE_TOKEN_TILE):
    n, d = h.shape
    lower = jnp.asarray(np.tril(np.ones((tm, tm), np.float32), -1), BF16)
    smem = pl.BlockSpec(memory_space=pltpu.SMEM)
    row = lambda w: pl.BlockSpec((tm, w), lambda i: (i, 0))
    return pl.pallas_call(
        _moe_combine_kernel,
        grid=(n // tm,),
        in_specs=[smem, smem, smem, row(d), row(N_EXPERTS), row(N_EXPERTS), _resident((tm, tm)),
                  row(p.shape[1]), _resident(gp.shape), _resident(wpg.shape), _resident(wpp.shape),
                  _resident(gfin.shape), pl.BlockSpec(memory_space=pl.ANY)],
        out_specs=row(d),
        out_shape=jax.ShapeDtypeStruct((n, d), F32),
        scratch_shapes=[pltpu.VMEM((MOE_LOCAL_ROWS, d), BF16), pltpu.SemaphoreType.DMA(())],
        compiler_params=_params("arbitrary"),
        name="moe_combine_ple_norm",
    )(plan["cnt"], plan["local"], plan["start"], h, gates, sel, lower, p, gp, wpg, wpp, gfin, ys)


def kernel(x, p, rel_bias, final_norm, e_norm_mix, e_w_in, e_conv_w, e_sinks, e_w_out, e_norm_ffn, e_ffn_gate, e_ffn_up, e_ffn_down, e_norm_ple, e_ple_gate, e_ple_proj, o_norm_mix, o_w_qkv, o_w_o, o_norm_ffn, o_router, o_exp_gate, o_exp_up, o_exp_down, o_norm_ple, o_ple_gate, o_ple_proj):
    bsz, seq, d = x.shape
    n = bsz * seq
    bf = lambda w: w.astype(BF16)
    h = x.reshape(n, d)
    p2 = p.reshape(p.shape[0], n, p.shape[-1])

    qi = np.arange(SWA_BLOCK)[:, None]
    kj = np.arange(2 * SWA_BLOCK)[None, :]
    swa_tab = _bias_table(rel_bias, _t5_bucket_np(qi - kj + SWA_BLOCK))
    kk = np.arange(MOBA_BLOCK)[:, None]
    qq = np.arange(MOBA_BLOCK)[None, :]
    own_bucket = np.where(qq >= kk, _t5_bucket_np(qq - kk), -1)
    moba_bucket = np.concatenate([_t5_bucket_np(qq - kk + MOBA_BLOCK), own_bucket], axis=1)
    moba_tab = _bias_table(rel_bias, moba_bucket, mult=LOG2E)

    a_out, qkv0 = _even_proj(h, e_norm_mix[0:1], bf(e_w_in[0]), e_conv_w[0], seq)
    b_out = _swa(qkv0, e_sinks[0], swa_tab, bsz, seq)
    h = _even_ffn(h, a_out, b_out, p2[0], bf(e_w_out[0]), e_norm_ffn[0:1],
                  bf(e_ffn_gate[0]), bf(e_ffn_up[0]), bf(e_ffn_down[0]),
                  e_norm_ple[0:1], bf(e_ple_gate[0]), bf(e_ple_proj[0]))

    w_qkv = o_w_qkv[0]
    q1, k1, vt1, kmean = _odd_proj(h, o_norm_mix[0:1], bf(w_qkv[:, :d]), bf(w_qkv[:, d:2 * d]),
                                   bf(w_qkv[:, 2 * d:].T))
    c_out = _moba(q1, k1, vt1, kmean, moba_tab, rel_bias, bsz, seq)
    h, xn, gates, sel, counts = _odd_router(h, c_out, bf(o_w_o[0]), o_norm_ffn[0:1], o_router[0])
    n_groups = (n // MOE_TOKEN_TILE) * N_EXPERTS
    max_rows = 2 * n + n_groups * (MOE_ALIGN - 1) + N_EXPERTS * (MOE_SLOT_TILE - 1)
    n_slot_tiles = -(-max_rows // MOE_SLOT_TILE) + 1
    plan = _moe_plan(counts, n_slot_tiles)
    xs = _moe_dispatch(xn, sel, plan, n_slot_tiles)
    ys = _moe_experts(xs, plan, bf(o_exp_gate[0]), bf(o_exp_up[0]), bf(o_exp_down[0]))
    out = _moe_combine(h, gates, sel, ys, plan, p2[1], o_norm_ple[0:1], bf(o_ple_gate[0]),
                       bf(o_ple_proj[0]), final_norm.reshape(1, d))
    return out.reshape(bsz, seq, d)
```

```python
import functools
import math

import jax
import jax.numpy as jnp
import numpy as np
from jax import lax
from jax.experimental import pallas as pl
from jax.experimental.pallas import tpu as pltpu

F32 = jnp.float32
BF16 = jnp.bfloat16

EPS = 1e-6
N_HEADS = 8
N_BUCKETS = 32
MAX_DISTANCE = 128

SC_WIDTH = 512
CONV_WIDTH = 3
SWA_KV_HEADS = 2
SWA_HEAD_DIM = 64
SWA_BLOCK = 128
SWA_Q_WIDTH = N_HEADS * SWA_HEAD_DIM
SWA_KV_WIDTH = SWA_KV_HEADS * SWA_HEAD_DIM

MOBA_HEAD_DIM = 128
MOBA_BLOCK = 256
MOBA_TOPK = 3
MOBA_FAR_GROUP = 4

N_EXPERTS = 8
MOE_TOKEN_TILE = 512
MOE_SLOT_TILE = 512
MOE_ALIGN = 16
MOE_FF_CHUNK = 256
MOE_LOCAL_ROWS = -(-(2 * MOE_TOKEN_TILE + N_EXPERTS * (MOE_ALIGN - 1)) // 128) * 128

VMEM_LIMIT_BYTES = 56 * 1024 * 1024
NEG_INF = float("-inf")
LOG2E = math.log2(math.e)


def _params(*semantics):
    return pltpu.CompilerParams(dimension_semantics=semantics,
                                vmem_limit_bytes=VMEM_LIMIT_BYTES)


def _resident(shape):
    zeros = (0,) * len(shape)
    return pl.BlockSpec(shape, lambda *_: zeros, pipeline_mode=pl.Buffered(1))


def _rms(x, g):
    return x * lax.rsqrt(jnp.mean(x * x, axis=-1, keepdims=True) + EPS) * g


def _dot(a, b):
    return jnp.dot(a, b, preferred_element_type=F32)


def _dot_nt(a, b):
    return lax.dot_general(a, b, (((1,), (1,)), ((), ())), preferred_element_type=F32)


def _t5_bucket_np(dist):
    d = np.maximum(dist, 0).astype(np.int32)
    max_exact = N_BUCKETS // 2
    scaled = (np.log(np.maximum(d, max_exact).astype(np.float32) / np.float32(max_exact))
              / np.float32(math.log(MAX_DISTANCE / max_exact)))
    large = np.minimum(max_exact + (scaled * (N_BUCKETS - max_exact)).astype(np.int32),
                       N_BUCKETS - 1)
    return np.where(d < max_exact, d, large).astype(np.int32)


def _bias_table_kernel(rb_ref, bkt_ref, out_ref, *, mult):
    h = pl.program_id(0)
    bkt = bkt_ref[...]
    acc = jnp.full(bkt.shape, NEG_INF, F32)
    for b in range(N_BUCKETS):
        acc = jnp.where(bkt == b, rb_ref[b, h], acc)
    out_ref[...] = acc * mult


def _bias_table(rel_bias, bucket, mult=1.0):
    rows, cols = bucket.shape
    return pl.pallas_call(
        functools.partial(_bias_table_kernel, mult=mult),
        grid=(N_HEADS,),
        in_specs=[pl.BlockSpec(memory_space=pltpu.SMEM),
                  pl.BlockSpec((rows, cols), lambda h: (0, 0))],
        out_specs=pl.BlockSpec((None, rows, cols), lambda h: (h, 0, 0)),
        out_shape=jax.ShapeDtypeStruct((N_HEADS, rows, cols), F32),
        compiler_params=_params("arbitrary"),
        name="bias_table",
    )(rel_bias, jnp.asarray(bucket))


def _even_proj_kernel(x_ref, g_ref, w_ref, cw_ref, a_ref, qkv_ref, cu_scr, *, tm, tiles_per_seq):
    i = pl.program_id(0)
    xn = _rms(x_ref[...], g_ref[...]).astype(BF16)
    b_gate = _dot(xn, w_ref[:, 0:SC_WIDTH])
    c_gate = _dot(xn, w_ref[:, SC_WIDTH:2 * SC_WIDTH])
    u = _dot(xn, w_ref[:, 2 * SC_WIDTH:3 * SC_WIDTH])
    cu = c_gate * u

    @pl.when(i % tiles_per_seq == 0)
    def _():
        cu_scr[0:8, :] = jnp.zeros((8, SC_WIDTH), F32)

    @pl.when(i % tiles_per_seq != 0)
    def _():
        cu_scr[0:8, :] = cu_scr[tm:tm + 8, :]

    cu_scr[8:8 + tm, :] = cu
    y = (cw_ref[0:1, :] * cu_scr[6:6 + tm, :] + cw_ref[1:2, :] * cu_scr[7:7 + tm, :]
         + cw_ref[2:3, :] * cu)
    a_ref[...] = (b_gate * y).astype(BF16)
    qkv_ref[...] = _dot(xn, w_ref[:, 3 * SC_WIDTH:]).astype(BF16)


def _even_proj(x, g, w, conv_w, seq, tm=512):
    n, d = x.shape
    d_in = w.shape[1]
    d_qkv = d_in - 3 * SC_WIDTH
    return pl.pallas_call(
        functools.partial(_even_proj_kernel, tm=tm, tiles_per_seq=seq // tm),
        grid=(n // tm,),
        in_specs=[pl.BlockSpec((tm, d), lambda i: (i, 0)),
                  _resident((1, d)), _resident((d, d_in)), _resident((CONV_WIDTH, SC_WIDTH))],
        out_specs=[pl.BlockSpec((tm, SC_WIDTH), lambda i: (i, 0)),
                   pl.BlockSpec((tm, d_qkv), lambda i: (i, 0))],
        out_shape=[jax.ShapeDtypeStruct((n, SC_WIDTH), BF16),
                   jax.ShapeDtypeStruct((n, d_qkv), BF16)],
        scratch_shapes=[pltpu.VMEM((tm + 8, SC_WIDTH), F32)],
        compiler_params=_params("arbitrary"),
        name="even_proj_conv",
    )(x, g, w, conv_w)


def _swa_kernel(sink_ref, cur_ref, prev_ref, tab_ref, out_ref):
    i = pl.program_id(1)
    blk = SWA_BLOCK
    hd = SWA_HEAD_DIM
    grp = N_HEADS // SWA_KV_HEADS
    qi = lax.broadcasted_iota(jnp.int32, (blk, 2 * blk), 0)
    kj = lax.broadcasted_iota(jnp.int32, (blk, 2 * blk), 1)
    diff = qi - kj + blk
    valid = (diff >= 0) & (diff < blk) & ((kj >= blk) | (i > 0))
    scale = hd ** -0.5
    outs = []
    for g in range(SWA_KV_HEADS):
        k_cat = jnp.concatenate([prev_ref[:, g * hd:(g + 1) * hd],
                                 cur_ref[:, SWA_Q_WIDTH + g * hd:SWA_Q_WIDTH + (g + 1) * hd]], axis=0)
        v_cat = jnp.concatenate(
            [prev_ref[:, SWA_KV_WIDTH + g * hd:SWA_KV_WIDTH + (g + 1) * hd],
             cur_ref[:, SWA_Q_WIDTH + SWA_KV_WIDTH + g * hd:SWA_Q_WIDTH + SWA_KV_WIDTH + (g + 1) * hd]],
            axis=0)
        for hh in range(grp):
            h = g * grp + hh
            q = cur_ref[:, h * hd:(h + 1) * hd]
            s = _dot_nt(q, k_cat) * scale + tab_ref[h]
            s = jnp.where(valid, s, NEG_INF)
            sink = sink_ref[h]
            m = jnp.maximum(jnp.max(s, axis=-1, keepdims=True), sink)
            p = jnp.exp(s - m)
            denom = jnp.sum(p, axis=-1, keepdims=True) + jnp.exp(sink - m)
            o = _dot(p.astype(BF16), v_cat)
            outs.append(o / denom)
    out_ref[...] = jnp.concatenate(outs, axis=-1).astype(BF16)


def _swa(qkv, sinks, tab, bsz, seq):
    n = qkv.shape[0]
    nb = seq // SWA_BLOCK
    kv_col_block = SWA_Q_WIDTH // (2 * SWA_KV_WIDTH)
    return pl.pallas_call(
        _swa_kernel,
        grid=(bsz, nb),
        in_specs=[pl.BlockSpec(memory_space=pltpu.SMEM),
                  pl.BlockSpec((SWA_BLOCK, qkv.shape[1]), lambda b, i: (b * nb + i, 0)),
                  pl.BlockSpec((SWA_BLOCK, 2 * SWA_KV_WIDTH),
                               lambda b, i: (b * nb + jnp.maximum(i - 1, 0), kv_col_block)),
                  _resident((N_HEADS, SWA_BLOCK, 2 * SWA_BLOCK))],
        out_specs=pl.BlockSpec((SWA_BLOCK, SWA_Q_WIDTH), lambda b, i: (b * nb + i, 0)),
        out_shape=jax.ShapeDtypeStruct((n, SWA_Q_WIDTH), BF16),
        compiler_params=_params("arbitrary", "arbitrary"),
        name="swa",
    )(sinks, qkv, qkv, tab)


def _ple(h, p, g, w_gate, w_proj):
    gate = jax.nn.sigmoid(_dot(_rms(h, g).astype(BF16), w_gate))
    return h + gate * _dot(p.astype(BF16), w_proj)


def _even_ffn_kernel(h_ref, a_ref, b_ref, p_ref, wo_ref, gf_ref, wg_ref, wu_ref, wd_ref,
                     gp_ref, wpg_ref, wpp_ref, out_ref, *, tf):
    h1 = (h_ref[...] + _dot(a_ref[...], wo_ref[0:SC_WIDTH, :])
          + _dot(b_ref[...], wo_ref[SC_WIDTH:, :]))
    xn = _rms(h1, gf_ref[...]).astype(BF16)
    d_ff = wg_ref.shape[1]
    acc = jnp.zeros(h1.shape, F32)
    for c in range(d_ff // tf):
        gate = _dot(xn, wg_ref[:, c * tf:(c + 1) * tf])
        up = _dot(xn, wu_ref[:, c * tf:(c + 1) * tf])
        hid = (gate * jax.nn.sigmoid(gate) * up).astype(BF16)
        acc = acc + _dot(hid, wd_ref[c * tf:(c + 1) * tf, :])
    h2 = h1 + acc
    out_ref[...] = _ple(h2, p_ref[...], gp_ref[...], wpg_ref[...], wpp_ref[...])


def _even_ffn(h, a, b, p, wo, gf, wg, wu, wd, gp, wpg, wpp, tm=512, tf=256):
    n, d = h.shape
    row = lambda w: pl.BlockSpec((tm, w), lambda i: (i, 0))
    return pl.pallas_call(
        functools.partial(_even_ffn_kernel, tf=tf),
        grid=(n // tm,),
        in_specs=[row(d), row(a.shape[1]), row(b.shape[1]), row(p.shape[1]),
                  _resident(wo.shape), _resident(gf.shape), _resident(wg.shape),
                  _resident(wu.shape), _resident(wd.shape), _resident(gp.shape),
                  _resident(wpg.shape), _resident(wpp.shape)],
        out_specs=row(d),
        out_shape=jax.ShapeDtypeStruct((n, d), F32),
        compiler_params=_params("arbitrary"),
        name="even_out_ffn_ple",
    )(h, a, b, p, wo, gf, wg, wu, wd, gp, wpg, wpp)


def _odd_proj_kernel(h_ref, g_ref, wq_ref, wk_ref, wvt_ref, q_ref, k_ref, vt_ref, kmean_ref, *, tm):
    xn = _rms(h_ref[...], g_ref[...]).astype(BF16)
    q_ref[...] = (_dot(xn, wq_ref[...]) * (MOBA_HEAD_DIM ** -0.5 * LOG2E)).astype(BF16)
    k = _dot(xn, wk_ref[...])
    k_ref[...] = k.astype(BF16)
    vt_ref[...] = _dot_nt(wvt_ref[...], xn).astype(BF16)
    for r in range(tm // MOBA_BLOCK):
        kmean_ref[r] = jnp.mean(k[r * MOBA_BLOCK:(r + 1) * MOBA_BLOCK, :], axis=0, keepdims=True)


def _odd_proj(h, g, wq, wk, wvt, tm=512):
    n, d = h.shape
    bpt = tm // MOBA_BLOCK
    row = pl.BlockSpec((tm, d), lambda i: (i, 0))
    return pl.pallas_call(
        functools.partial(_odd_proj_kernel, tm=tm),
        grid=(n // tm,),
        in_specs=[row, _resident(g.shape), _resident(wq.shape), _resident(wk.shape),
                  _resident(wvt.shape)],
        out_specs=[row, row, pl.BlockSpec((d, tm), lambda i: (0, i)),
                   pl.BlockSpec((bpt, 1, d), lambda i: (i, 0, 0))],
        out_shape=[jax.ShapeDtypeStruct((n, d), BF16), jax.ShapeDtypeStruct((n, d), BF16),
                   jax.ShapeDtypeStruct((d, n), BF16),
                   jax.ShapeDtypeStruct((n // MOBA_BLOCK, 1, d), F32)],
        compiler_params=_params("arbitrary"),
        name="odd_qkv",
    )(h, g, wq, wk, wvt)


def _moba_kernel(rb_ref, q_ref, k_ref, vt_ref, km_ref, tab_ref, out_ref,
                 sel_scr, adj_scr, m_scr, l_scr, acc_scr, far_a_scr, far_b_scr, gmax_scr, *, nblk, hg):
    head0 = pl.program_id(1) * hg
    qi = pl.program_id(2)
    bs = MOBA_BLOCK
    hd = MOBA_HEAD_DIM
    heads = [slice(hh * hd, (hh + 1) * hd) for hh in range(hg)]

    def select_blocks():
        blk = lax.broadcasted_iota(jnp.int32, (nblk, bs), 0)
        past = blk < qi
        for hh in range(hg):
            gate = lax.dot_general(km_ref[:, heads[hh]], q_ref[:, heads[hh]].astype(F32),
                                   (((1,), (1,)), ((), ())), preferred_element_type=F32,
                                   precision=lax.Precision.HIGHEST)
            g = jnp.where(past, gate, NEG_INF)
            sel = jnp.zeros(gate.shape, jnp.bool_)
            for _ in range(MOBA_TOPK):
                top = jnp.max(g, axis=0, keepdims=True)
                idx = jnp.min(jnp.where(g == top, blk, nblk), axis=0, keepdims=True)
                hit = blk == idx
                sel = sel | hit
                g = jnp.where(hit, NEG_INF, g)
            far_bias = rb_ref[N_BUCKETS - 1, head0 + hh] * LOG2E
            sel_scr[hh] = jnp.where(sel & (blk < qi - 1), far_bias, NEG_INF)
            adj_scr[hh] = jnp.max(jnp.where(sel & (blk == qi - 1), 0.0, NEG_INF), axis=0,
                                  keepdims=True)

    def score_dots(start, nb):
        return [_dot_nt(k_ref[pl.ds(start, nb * bs), heads[hh]], q_ref[:, heads[hh]])
                for hh in range(hg)]

    def softmax_update(scores, start, nb, tables, query_biases, first):
        probs, alphas = [], []
        for hh in range(hg):
            parts, s_max = [], None
            for r in range(nb):
                s_r = scores[hh][r * bs:(r + 1) * bs]
                if tables(hh, r) is not None:
                    s_r = s_r + tables(hh, r)
                col_max = jnp.max(s_r, axis=0, keepdims=True)
                if query_biases(hh, r) is not None:
                    col_max = col_max + query_biases(hh, r)
                parts.append(s_r)
                s_max = col_max if s_max is None else jnp.maximum(s_max, col_max)
            m_new = s_max if first else jnp.maximum(m_scr[hh], s_max)
            p = jnp.concatenate(
                [jnp.exp2(parts[r] - (m_new if query_biases(hh, r) is None
                                      else m_new - query_biases(hh, r))) for r in range(nb)], axis=0)
            l_new = jnp.sum(p, axis=0, keepdims=True)
            if not first:
                alpha = jnp.exp2(m_scr[hh] - m_new)
                alphas.append(alpha)
                l_new = alpha * l_scr[hh] + l_new
            m_scr[hh] = m_new
            l_scr[hh] = l_new
            probs.append(p.astype(BF16))
        for hh in range(hg):
            acc_new = _dot(vt_ref[heads[hh], pl.ds(start, nb * bs)], probs[hh])
            if not first:
                acc_new = alphas[hh] * acc_scr[hh] + acc_new
            acc_scr[hh] = acc_new

    @pl.when(qi == 0)
    def _():
        softmax_update(score_dots(0, 1), 0, 1, lambda hh, r: tab_ref[hh, :, bs:2 * bs],
                       lambda hh, r: None, first=True)

    far_rows = MOBA_FAR_GROUP * bs
    n_far = (qi - 1 + MOBA_FAR_GROUP - 1) // MOBA_FAR_GROUP
    far_bufs = (far_a_scr, far_b_scr)

    def far_start(g):
        return pl.multiple_of(g * far_rows, far_rows)

    def far_bias(hh, g, r):
        return sel_scr[hh, pl.ds(g * MOBA_FAR_GROUP + r, 1), :]

    def far_scores(g, buf):
        scores = score_dots(far_start(g), MOBA_FAR_GROUP)
        for hh in range(hg):
            far_bufs[buf][hh] = scores[hh]
            group_max = None
            for r in range(MOBA_FAR_GROUP):
                col_max = (jnp.max(scores[hh][r * bs:(r + 1) * bs], axis=0, keepdims=True)
                           + far_bias(hh, g, r))
                group_max = col_max if group_max is None else jnp.maximum(group_max, col_max)
            gmax_scr[buf, hh] = group_max

    def far_softmax(g, buf):
        probs, alphas = [], []
        for hh in range(hg):
            m_old = m_scr[hh]
            m_new = jnp.maximum(m_old, gmax_scr[buf, hh])
            alpha = jnp.exp2(m_old - m_new)
            p = jnp.concatenate(
                [jnp.exp2(far_bufs[buf][hh, r * bs:(r + 1) * bs, :] - (m_new - far_bias(hh, g, r)))
                 for r in range(MOBA_FAR_GROUP)], axis=0)
            m_scr[hh] = m_new
            l_scr[hh] = alpha * l_scr[hh] + jnp.sum(p, axis=0, keepdims=True)
            probs.append(p.astype(BF16))
            alphas.append(alpha)
        for hh in range(hg):
            acc_scr[hh] = (alphas[hh] * acc_scr[hh]
                           + _dot(vt_ref[heads[hh], pl.ds(far_start(g), far_rows)], probs[hh]))

    def far_step(g, buf, prefetch):
        if prefetch:
            far_scores(g + 1, 1 - buf)
        far_softmax(g, buf)

    @pl.when(qi > 0)
    def _():
        select_blocks()
        start = pl.multiple_of((qi - 1) * bs, bs)
        scores = score_dots(start, 2)
        far_scores(0, 0)
        softmax_update(scores, start, 2, lambda hh, r: tab_ref[hh, :, r * bs:(r + 1) * bs],
                       lambda hh, r: adj_scr[hh] if r == 0 else None, first=True)

    def far_pair(i, carry):
        far_step(2 * i, 0, True)
        far_step(2 * i + 1, 1, True)
        return carry

    lax.fori_loop(0, (n_far - 1) // 2, far_pair, 0)

    @pl.when((n_far > 0) & (n_far % 2 == 0))
    def _():
        far_step(n_far - 2, 0, True)
        far_step(n_far - 1, 1, False)

    @pl.when(n_far % 2 == 1)
    def _():
        far_step(n_far - 1, 0, False)

    for hh in range(hg):
        out_ref[:, heads[hh]] = (acc_scr[hh] / l_scr[hh]).T.astype(BF16)


def _moba(q, k, vt, kmean, tab, rel_bias, bsz, seq, hg=4):
    n, d = q.shape
    nblk = seq // MOBA_BLOCK
    assert nblk % MOBA_FAR_GROUP == 0, "far-block groups must not run past the sequence"
    bs = MOBA_BLOCK
    w = hg * MOBA_HEAD_DIM
    return pl.pallas_call(
        functools.partial(_moba_kernel, nblk=nblk, hg=hg),
        grid=(bsz, N_HEADS // hg, nblk),
        in_specs=[pl.BlockSpec(memory_space=pltpu.SMEM),
                  pl.BlockSpec((bs, w), lambda b, g, i: (b * nblk + i, g)),
                  pl.BlockSpec((seq, w), lambda b, g, i: (b, g)),
                  pl.BlockSpec((w, seq), lambda b, g, i: (g, b)),
                  pl.BlockSpec((None, nblk, w), lambda b, g, i: (b, 0, g)),
                  pl.BlockSpec((hg, bs, 2 * bs), lambda b, g, i: (g, 0, 0))],
        out_specs=pl.BlockSpec((bs, w), lambda b, g, i: (b * nblk + i, g)),
        out_shape=jax.ShapeDtypeStruct((n, d), BF16),
        scratch_shapes=[pltpu.VMEM((hg, nblk, bs), F32), pltpu.VMEM((hg, 1, bs), F32),
                        pltpu.VMEM((hg, 1, bs), F32), pltpu.VMEM((hg, 1, bs), F32),
                        pltpu.VMEM((hg, MOBA_HEAD_DIM, bs), F32),
                        pltpu.VMEM((hg, MOBA_FAR_GROUP * bs, bs), F32),
                        pltpu.VMEM((hg, MOBA_FAR_GROUP * bs, bs), F32),
                        pltpu.VMEM((2, hg, 1, bs), F32)],
        compiler_params=_params("arbitrary", "arbitrary", "arbitrary"),
        name="moba",
    )(rel_bias, q, k, vt, kmean.reshape(bsz, nblk, d), tab)


def _odd_router_kernel(h_ref, c_ref, wo_ref, g_ref, wr_ref, h_out_ref, xn_ref, gates_ref, sel_ref,
                       cnt_ref):
    h1 = h_ref[...] + _dot(c_ref[...], wo_ref[...])
    h_out_ref[...] = h1
    xn = _rms(h1, g_ref[...])
    xn_hi = xn.astype(BF16)
    xn_ref[...] = xn_hi
    xn_lo = (xn - xn_hi.astype(F32)).astype(BF16)
    wr = wr_ref[...]
    wr_hi = wr.astype(BF16)
    wr_lo = (wr - wr_hi.astype(F32)).astype(BF16)
    logits = _dot(xn_hi, wr_hi) + (_dot(xn_lo, wr_hi) + _dot(xn_hi, wr_lo))
    lane = lax.broadcasted_iota(jnp.int32, logits.shape, 1)
    v1 = jnp.max(logits, axis=-1, keepdims=True)
    i1 = jnp.min(jnp.where(logits == v1, lane, N_EXPERTS), axis=-1, keepdims=True)
    rest = jnp.where(lane == i1, NEG_INF, logits)
    v2 = jnp.max(rest, axis=-1, keepdims=True)
    i2 = jnp.min(jnp.where(rest == v2, lane, N_EXPERTS), axis=-1, keepdims=True)
    e2 = jnp.exp(v2 - v1)
    w1 = 1.0 / (1.0 + e2)
    w2 = e2 / (1.0 + e2)
    gates_ref[...] = jnp.where(lane == i1, w1, 0.0) + jnp.where(lane == i2, w2, 0.0)
    sel = jnp.where((lane == i1) | (lane == i2), 1.0, 0.0)
    sel_ref[...] = sel
    cnt_ref[0] = jnp.sum(sel, axis=0, keepdims=True)


def _odd_router(h, c, wo, g, wr, tm=MOE_TOKEN_TILE):
    n, d = h.shape
    row = lambda w: pl.BlockSpec((tm, w), lambda i: (i, 0))
    return pl.pallas_call(
        _odd_router_kernel,
        grid=(n // tm,),
        in_specs=[row(d), row(d), _resident(wo.shape), _resident(g.shape), _resident(wr.shape)],
        out_specs=[row(d), row(d), row(N_EXPERTS), row(N_EXPERTS),
                   pl.BlockSpec((1, 1, N_EXPERTS), lambda i: (i, 0, 0))],
        out_shape=[jax.ShapeDtypeStruct((n, d), F32), jax.ShapeDtypeStruct((n, d), BF16),
                   jax.ShapeDtypeStruct((n, N_EXPERTS), F32),
                   jax.ShapeDtypeStruct((n, N_EXPERTS), F32),
                   jax.ShapeDtypeStruct((n // tm, 1, N_EXPERTS), F32)],
        compiler_params=_params("arbitrary"),
        name="odd_out_router",
    )(h, c, wo, g, wr)


def _moe_plan(counts, n_slot_tiles):
    cnt = counts.reshape(-1, N_EXPERTS).astype(jnp.int32)
    grp = (cnt + MOE_ALIGN - 1) // MOE_ALIGN * MOE_ALIGN
    local = jnp.cumsum(grp, axis=1) - grp
    tot = jnp.sum(grp, axis=0)
    region = (tot + MOE_SLOT_TILE - 1) // MOE_SLOT_TILE * MOE_SLOT_TILE
    region_end = jnp.cumsum(region)
    region_start = region_end - region
    start = region_start[None, :] + jnp.cumsum(grp, axis=0) - grp
    n_used = region_end[-1] // MOE_SLOT_TILE
    tile_row = jnp.arange(n_slot_tiles, dtype=jnp.int32) * MOE_SLOT_TILE
    tile_expert = jnp.searchsorted(region_end, tile_row, side="right").astype(jnp.int32)
    tile_expert = jnp.minimum(tile_expert, N_EXPERTS - 1)
    tile_expert = jnp.where(tile_row < region_end[-1], tile_expert, tile_expert[n_used - 1])
    return dict(cnt=cnt, local=local.astype(jnp.int32), start=start.astype(jnp.int32),
                tail_start=(region_start + tot).astype(jnp.int32),
                tail_len=(region - tot).astype(jnp.int32), tile_expert=tile_expert,
                n_used=n_used.reshape(1).astype(jnp.int32))


def _group_copies(cnt_ref, local_ref, start_ref, t, local_buf, slot_array, sem, to_slots):
    copies = []
    for e in range(N_EXPERTS):
        rows = (cnt_ref[t, e] + MOE_ALIGN - 1) // MOE_ALIGN * MOE_ALIGN
        size = MOE_TOKEN_TILE
        while size >= MOE_ALIGN:
            off = rows & ~(2 * size - 1)
            local = local_buf.at[pl.ds(pl.multiple_of(local_ref[t, e] + off, MOE_ALIGN), size)]
            slots = slot_array.at[pl.ds(pl.multiple_of(start_ref[t, e] + off, MOE_ALIGN), size)]
            src, dst = (local, slots) if to_slots else (slots, local)
            copies.append(((rows & size) != 0, pltpu.make_async_copy(src, dst, sem)))
            size //= 2
    return copies


def _start_all(copies):
    for cond, cp in copies:
        pl.when(cond)(cp.start)


def _wait_all(copies):
    for cond, cp in copies:
        pl.when(cond)(cp.wait)


def _moe_dispatch_kernel(cnt_ref, local_ref, start_ref, tail_start_ref, tail_len_ref, x_ref, sel_ref,
                         upper_ref, xs_ref, stage_ref, zero_ref, sem, tail_sem):
    t = pl.program_id(0)
    last = pl.num_programs(0) - 1
    tile = x_ref.shape[0]
    slot = t % 2
    eye = jnp.where(lax.broadcasted_iota(jnp.int32, (N_EXPERTS, N_EXPERTS), 0)
                    == lax.broadcasted_iota(jnp.int32, (N_EXPERTS, N_EXPERTS), 1), 1.0, 0.0)
    sel_t = _dot_nt(eye.astype(BF16), sel_ref[...].astype(BF16))
    rank_t = _dot(sel_t.astype(BF16), upper_ref[...])
    expert = lax.broadcasted_iota(jnp.int32, (N_EXPERTS, 1), 0)
    base = jnp.zeros((N_EXPERTS, 1), F32)
    for e in range(N_EXPERTS):
        base = jnp.where(expert == e, local_ref[t, e].astype(F32), base)
    pos = base + rank_t
    pos_lo = jnp.min(jnp.where(sel_t > 0.5, pos, float(MOE_LOCAL_ROWS)), axis=0, keepdims=True)
    pos_hi = jnp.max(jnp.where(sel_t > 0.5, pos, -1.0), axis=0, keepdims=True)
    row = lax.broadcasted_iota(jnp.int32, (MOE_LOCAL_ROWS, tile), 0).astype(F32)
    onehot = jnp.where(row == pos_lo, 1.0, jnp.where(row == pos_hi, 1.0, 0.0)).astype(BF16)
    stage_ref[slot] = _dot(onehot, x_ref[...]).astype(BF16)

    def copies(step):
        return _group_copies(cnt_ref, local_ref, start_ref, step, stage_ref.at[step % 2], xs_ref,
                             sem.at[step % 2], to_slots=True)

    _start_all(copies(t))

    @pl.when(t > 0)
    def _():
        _wait_all(copies(t - 1))

    @pl.when(t == last)
    def _():
        _wait_all(copies(t))
        zero_ref[...] = jnp.zeros(zero_ref.shape, BF16)
        tails = []
        for e in range(N_EXPERTS):
            size = MOE_SLOT_TILE // 2
            while size >= MOE_ALIGN:
                off = tail_len_ref[e] & ~(2 * size - 1)
                dst = pl.multiple_of(tail_start_ref[e] + off, MOE_ALIGN)
                tails.append(((tail_len_ref[e] & size) != 0, pltpu.make_async_copy(
                    zero_ref.at[pl.ds(0, size)], xs_ref.at[pl.ds(dst, size)], tail_sem)))
                size //= 2
        _start_all(tails)
        _wait_all(tails)

        used_rows = tail_start_ref[N_EXPERTS - 1] + tail_len_ref[N_EXPERTS - 1]
        piece = zero_ref.shape[0]

        def unused_copy(k):
            dst = pl.multiple_of(used_rows + k * piece, piece)
            return pltpu.make_async_copy(zero_ref, xs_ref.at[pl.ds(dst, piece)], tail_sem)

        n_pieces = (xs_ref.shape[0] - used_rows) // piece
        lax.fori_loop(0, n_pieces, lambda k, c: (unused_copy(k).start(), c)[1], 0)
        lax.fori_loop(0, n_pieces, lambda k, c: (unused_copy(k).wait(), c)[1], 0)


def _moe_dispatch(xn, sel, plan, n_slot_tiles, tm=MOE_TOKEN_TILE):
    n, d = xn.shape
    upper = jnp.asarray(np.triu(np.ones((tm, tm), np.float32), 1), BF16)
    smem = pl.BlockSpec(memory_space=pltpu.SMEM)
    return pl.pallas_call(
        _moe_dispatch_kernel,
        grid=(n // tm,),
        in_specs=[smem, smem, smem, smem, smem,
                  pl.BlockSpec((tm, d), lambda i: (i, 0)),
                  pl.BlockSpec((tm, N_EXPERTS), lambda i: (i, 0)),
                  _resident((tm, tm))],
        out_specs=pl.BlockSpec(memory_space=pl.ANY),
        out_shape=jax.ShapeDtypeStruct((n_slot_tiles * MOE_SLOT_TILE, d), BF16),
        scratch_shapes=[pltpu.VMEM((2, MOE_LOCAL_ROWS, d), BF16),
                        pltpu.VMEM((MOE_SLOT_TILE // 2, d), BF16),
                        pltpu.SemaphoreType.DMA((2,)), pltpu.SemaphoreType.DMA(())],
        compiler_params=_params("arbitrary"),
        name="moe_dispatch",
    )(plan["cnt"], plan["local"], plan["start"], plan["tail_start"], plan["tail_len"], xn, sel, upper)


def _moe_experts_kernel(te_ref, nu_ref, x_ref, wg_ref, wu_ref, wd_ref, y_ref, acc_ref):
    i = pl.program_id(0)
    f = pl.program_id(1)
    last = pl.num_programs(1) - 1
    used = i < nu_ref[0]

    @pl.when(used & (f == 0))
    def _():
        acc_ref[...] = jnp.zeros(acc_ref.shape, F32)

    @pl.when(used)
    def _():
        x = x_ref[...]
        acc = acc_ref[...]
        for c in range(wg_ref.shape[1] // MOE_FF_CHUNK):
            cols = slice(c * MOE_FF_CHUNK, (c + 1) * MOE_FF_CHUNK)
            gate = _dot(x, wg_ref[:, cols])
            up = _dot(x, wu_ref[:, cols])
            hid = (gate * jax.nn.sigmoid(gate) * up).astype(BF16)
            acc = acc + _dot(hid, wd_ref[cols, :])
        acc_ref[...] = acc

    @pl.when(used & (f == last))
    def _():
        y_ref[...] = acc_ref[...].astype(BF16)

    @pl.when(jnp.logical_not(used) & (f == last))
    def _():
        y_ref[...] = jnp.zeros(y_ref.shape, BF16)


def _moe_experts(xs, plan, wg, wu, wd, tf=1792):
    rows, d = xs.shape
    d_ff = wg.shape[2]
    nf = d_ff // tf
    tm = MOE_SLOT_TILE
    f_idx = lambda i, f, nu: jnp.where(i < nu[0], f, nf - 1)
    return pl.pallas_call(
        _moe_experts_kernel,
        grid_spec=pltpu.PrefetchScalarGridSpec(
            num_scalar_prefetch=2,
            grid=(rows // tm, nf),
            in_specs=[pl.BlockSpec((tm, d), lambda i, f, te, nu: (jnp.minimum(i, nu[0] - 1), 0)),
                      pl.BlockSpec((None, d, tf), lambda i, f, te, nu: (te[i], 0, f_idx(i, f, nu))),
                      pl.BlockSpec((None, d, tf), lambda i, f, te, nu: (te[i], 0, f_idx(i, f, nu))),
                      pl.BlockSpec((None, tf, d), lambda i, f, te, nu: (te[i], f_idx(i, f, nu), 0))],
            out_specs=pl.BlockSpec((tm, d), lambda i, f, te, nu: (i, 0)),
            scratch_shapes=[pltpu.VMEM((tm, d), F32)]),
        out_shape=jax.ShapeDtypeStruct((rows, d), BF16),
        compiler_params=_params("arbitrary", "arbitrary"),
        name="moe_experts",
    )(plan["tile_expert"], plan["n_used"], xs, wg, wu, wd)


def _moe_combine_kernel(cnt_ref, local_ref, start_ref, h_ref, gates_ref, sel_ref, lower_ref, p_ref,
                        gp_ref, wpg_ref, wpp_ref, gfin_ref, ys_ref, out_ref, ybuf_ref, sem):
    t = pl.program_id(0)
    tile = h_ref.shape[0]

    def copies(step):
        return _group_copies(cnt_ref, local_ref, start_ref, step, ybuf_ref.at[step % 2], ys_ref,
                             sem.at[step % 2], to_slots=False)

    @pl.when(t == 0)
    def _():
        ybuf_ref[...] = jnp.zeros(ybuf_ref.shape, BF16)
        _start_all(copies(t))

    @pl.when(t + 1 < pl.num_programs(0))
    def _():
        _start_all(copies(t + 1))

    sel = sel_ref[...] > 0.5
    gates = gates_ref[...]
    rank = _dot(lower_ref[...], sel_ref[...].astype(BF16))
    expert = lax.broadcasted_iota(jnp.int32, (1, N_EXPERTS), 1)
    base = jnp.zeros((1, N_EXPERTS), F32)
    for e in range(N_EXPERTS):
        base = jnp.where(expert == e, local_ref[t, e].astype(F32), base)
    pos = base + rank
    pos_lo = jnp.min(jnp.where(sel, pos, float(MOE_LOCAL_ROWS)), axis=1, keepdims=True)
    pos_hi = jnp.max(jnp.where(sel, pos, -1.0), axis=1, keepdims=True)
    gate_lo = jnp.sum(jnp.where(sel & (pos == pos_lo), gates, 0.0), axis=1, keepdims=True)
    gate_hi = jnp.sum(jnp.where(sel & (pos == pos_hi), gates, 0.0), axis=1, keepdims=True)
    col = lax.broadcasted_iota(jnp.int32, (tile, MOE_LOCAL_ROWS), 1).astype(F32)
    onehot_lo = jnp.where(col == pos_lo, 1.0, 0.0).astype(BF16)
    onehot_hi = jnp.where(col == pos_hi, 1.0, 0.0).astype(BF16)

    _wait_all(copies(t))
    y = ybuf_ref[t % 2]
    h = h_ref[...] + gate_lo * _dot(onehot_lo, y) + gate_hi * _dot(onehot_hi, y)
    h = _ple(h, p_ref[...], gp_ref[...], wpg_ref[...], wpp_ref[...])
    out_ref[...] = _rms(h, gfin_ref[...])


def _moe_combine(h, gates, sel, ys, plan, p, gp, wpg, wpp, gfin, tm=MOE_TOKEN_TILE):
    n, d = h.shape
    lower = jnp.asarray(np.tril(np.ones((tm, tm), np.float32), -1), BF16)
    smem = pl.BlockSpec(memory_space=pltpu.SMEM)
    row = lambda w: pl.BlockSpec((tm, w), lambda i: (i, 0))
    return pl.pallas_call(
        _moe_combine_kernel,
        grid=(n // tm,),
        in_specs=[smem, smem, smem, row(d), row(N_EXPERTS), row(N_EXPERTS), _resident((tm, tm)),
                  row(p.shape[1]), _resident(gp.shape), _resident(wpg.shape), _resident(wpp.shape),
                  _resident(gfin.shape), pl.BlockSpec(memory_space=pl.ANY)],
        out_specs=row(d),
        out_shape=jax.ShapeDtypeStruct((n, d), F32),
        scratch_shapes=[pltpu.VMEM((2, MOE_LOCAL_ROWS, d), BF16), pltpu.SemaphoreType.DMA((2,))],
        compiler_params=_params("arbitrary"),
        name="moe_combine_ple_norm",
    )(plan["cnt"], plan["local"], plan["start"], h, gates, sel, lower, p, gp, wpg, wpp, gfin, ys)


def kernel(x, p, rel_bias, final_norm, e_norm_mix, e_w_in, e_conv_w, e_sinks, e_w_out, e_norm_ffn, e_ffn_gate, e_ffn_up, e_ffn_down, e_norm_ple, e_ple_gate, e_ple_proj, o_norm_mix, o_w_qkv, o_w_o, o_norm_ffn, o_router, o_exp_gate, o_exp_up, o_exp_down, o_norm_ple, o_ple_gate, o_ple_proj):
    bsz, seq, d = x.shape
    n = bsz * seq
    bf = lambda w: w.astype(BF16)
    h = x.reshape(n, d)
    p2 = p.reshape(p.shape[0], n, p.shape[-1])

    qi = np.arange(SWA_BLOCK)[:, None]
    kj = np.arange(2 * SWA_BLOCK)[None, :]
    swa_tab = _bias_table(rel_bias, _t5_bucket_np(qi - kj + SWA_BLOCK))
    kk = np.arange(MOBA_BLOCK)[:, None]
    qq = np.arange(MOBA_BLOCK)[None, :]
    own_bucket = np.where(qq >= kk, _t5_bucket_np(qq - kk), -1)
    moba_bucket = np.concatenate([_t5_bucket_np(qq - kk + MOBA_BLOCK), own_bucket], axis=1)
    moba_tab = _bias_table(rel_bias, moba_bucket, mult=LOG2E)

    a_out, qkv0 = _even_proj(h, e_norm_mix[0:1], bf(e_w_in[0]), e_conv_w[0], seq)
    b_out = _swa(qkv0, e_sinks[0], swa_tab, bsz, seq)
    h = _even_ffn(h, a_out, b_out, p2[0], bf(e_w_out[0]), e_norm_ffn[0:1],
                  bf(e_ffn_gate[0]), bf(e_ffn_up[0]), bf(e_ffn_down[0]),
                  e_norm_ple[0:1], bf(e_ple_gate[0]), bf(e_ple_proj[0]))

    w_qkv = o_w_qkv[0]
    q1, k1, vt1, kmean = _odd_proj(h, o_norm_mix[0:1], bf(w_qkv[:, :d]), bf(w_qkv[:, d:2 * d]),
                                   bf(w_qkv[:, 2 * d:].T))
    c_out = _moba(q1, k1, vt1, kmean, moba_tab, rel_bias, bsz, seq)
    h, xn, gates, sel, counts = _odd_router(h, c_out, bf(o_w_o[0]), o_norm_ffn[0:1], o_router[0])
    n_groups = (n // MOE_TOKEN_TILE) * N_EXPERTS
    max_rows = 2 * n + n_groups * (MOE_ALIGN - 1) + N_EXPERTS * (MOE_SLOT_TILE - 1)
    n_slot_tiles = -(-max_rows // MOE_SLOT_TILE) + 1
    plan = _moe_plan(counts, n_slot_tiles)
    xs = _moe_dispatch(xn, sel, plan, n_slot_tiles)
    ys = _moe_experts(xs, plan, bf(o_exp_gate[0]), bf(o_exp_up[0]), bf(o_exp_down[0]))
    out = _moe_combine(h, gates, sel, ys, plan, p2[1], o_norm_ple[0:1], bf(o_ple_gate[0]),
                       bf(o_ple_proj[0]), final_norm.reshape(1, d))
    return out.reshape(bsz, seq, d)
```

```python
import functools
import math

import jax
import jax.numpy as jnp
import numpy as np
from jax import lax
from jax.experimental import pallas as pl
from jax.experimental.pallas import tpu as pltpu

F32 = jnp.float32
BF16 = jnp.bfloat16

EPS = 1e-6
N_HEADS = 8
N_BUCKETS = 32
MAX_DISTANCE = 128

SC_WIDTH = 512
CONV_WIDTH = 3
SWA_KV_HEADS = 2
SWA_HEAD_DIM = 64
SWA_BLOCK = 128
SWA_Q_WIDTH = N_HEADS * SWA_HEAD_DIM
SWA_KV_WIDTH = SWA_KV_HEADS * SWA_HEAD_DIM

MOBA_HEAD_DIM = 128
MOBA_BLOCK = 256
MOBA_TOPK = 3
MOBA_FAR_GROUP = 4

N_EXPERTS = 8
MOE_TOKEN_TILE = 512
MOE_SLOT_TILE = 512
MOE_ALIGN = 16
MOE_FF_CHUNK = 256
MOE_LOCAL_ROWS = -(-(2 * MOE_TOKEN_TILE + N_EXPERTS * (MOE_ALIGN - 1)) // 128) * 128

VMEM_LIMIT_BYTES = 56 * 1024 * 1024
NEG_INF = float("-inf")
LOG2E = math.log2(math.e)


def _params(*semantics):
    return pltpu.CompilerParams(dimension_semantics=semantics,
                                vmem_limit_bytes=VMEM_LIMIT_BYTES)


def _resident(shape):
    zeros = (0,) * len(shape)
    return pl.BlockSpec(shape, lambda *_: zeros, pipeline_mode=pl.Buffered(1))


def _rms(x, g):
    return x * lax.rsqrt(jnp.mean(x * x, axis=-1, keepdims=True) + EPS) * g


def _dot(a, b):
    return jnp.dot(a, b, preferred_element_type=F32)


def _dot_nt(a, b):
    return lax.dot_general(a, b, (((1,), (1,)), ((), ())), preferred_element_type=F32)


def _t5_bucket_np(dist):
    d = np.maximum(dist, 0).astype(np.int32)
    max_exact = N_BUCKETS // 2
    scaled = (np.log(np.maximum(d, max_exact).astype(np.float32) / np.float32(max_exact))
              / np.float32(math.log(MAX_DISTANCE / max_exact)))
    large = np.minimum(max_exact + (scaled * (N_BUCKETS - max_exact)).astype(np.int32),
                       N_BUCKETS - 1)
    return np.where(d < max_exact, d, large).astype(np.int32)


def _bias_table_kernel(rb_ref, bkt_ref, out_ref, *, mult):
    h = pl.program_id(0)
    bkt = bkt_ref[...]
    acc = jnp.full(bkt.shape, NEG_INF, F32)
    for b in range(N_BUCKETS):
        acc = jnp.where(bkt == b, rb_ref[b, h], acc)
    out_ref[...] = acc * mult


def _bias_table(rel_bias, bucket, mult=1.0):
    rows, cols = bucket.shape
    return pl.pallas_call(
        functools.partial(_bias_table_kernel, mult=mult),
        grid=(N_HEADS,),
        in_specs=[pl.BlockSpec(memory_space=pltpu.SMEM),
                  pl.BlockSpec((rows, cols), lambda h: (0, 0))],
        out_specs=pl.BlockSpec((None, rows, cols), lambda h: (h, 0, 0)),
        out_shape=jax.ShapeDtypeStruct((N_HEADS, rows, cols), F32),
        compiler_params=_params("arbitrary"),
        name="bias_table",
    )(rel_bias, jnp.asarray(bucket))


def _even_proj_kernel(x_ref, g_ref, w_ref, cw_ref, a_ref, qkv_ref, cu_scr, *, tm, tiles_per_seq):
    i = pl.program_id(0)
    cu_scr[0:8, :] = jnp.where(i % tiles_per_seq == 0, 0.0, cu_scr[tm:tm + 8, :])
    xn = _rms(x_ref[...], g_ref[...]).astype(BF16)
    c_gate = _dot(xn, w_ref[:, SC_WIDTH:2 * SC_WIDTH])
    u = _dot(xn, w_ref[:, 2 * SC_WIDTH:3 * SC_WIDTH])
    qkv = _dot(xn, w_ref[:, 3 * SC_WIDTH:])
    b_gate = _dot(xn, w_ref[:, 0:SC_WIDTH])
    cu = c_gate * u
    cu_scr[8:8 + tm, :] = cu
    y = (cw_ref[0:1, :] * cu_scr[6:6 + tm, :] + cw_ref[1:2, :] * cu_scr[7:7 + tm, :]
         + cw_ref[2:3, :] * cu)
    qkv_ref[...] = qkv.astype(BF16)
    a_ref[...] = (b_gate * y).astype(BF16)


def _even_proj(x, g, w, conv_w, seq, tm=512):
    n, d = x.shape
    d_in = w.shape[1]
    d_qkv = d_in - 3 * SC_WIDTH
    return pl.pallas_call(
        functools.partial(_even_proj_kernel, tm=tm, tiles_per_seq=seq // tm),
        grid=(n // tm,),
        in_specs=[pl.BlockSpec((tm, d), lambda i: (i, 0)),
                  _resident((1, d)), _resident((d, d_in)), _resident((CONV_WIDTH, SC_WIDTH))],
        out_specs=[pl.BlockSpec((tm, SC_WIDTH), lambda i: (i, 0)),
                   pl.BlockSpec((tm, d_qkv), lambda i: (i, 0))],
        out_shape=[jax.ShapeDtypeStruct((n, SC_WIDTH), BF16),
                   jax.ShapeDtypeStruct((n, d_qkv), BF16)],
        scratch_shapes=[pltpu.VMEM((tm + 8, SC_WIDTH), F32)],
        compiler_params=_params("arbitrary"),
        name="even_proj_conv",
    )(x, g, w, conv_w)


def _swa_kernel(sink_ref, cur_ref, prev_ref, tab_ref, out_ref):
    i = pl.program_id(1)
    blk = SWA_BLOCK
    hd = SWA_HEAD_DIM
    grp = N_HEADS // SWA_KV_HEADS
    qi = lax.broadcasted_iota(jnp.int32, (blk, 2 * blk), 0)
    kj = lax.broadcasted_iota(jnp.int32, (blk, 2 * blk), 1)
    diff = qi - kj + blk
    valid = (diff >= 0) & (diff < blk) & ((kj >= blk) | (i > 0))
    scale = hd ** -0.5
    outs = []
    for g in range(SWA_KV_HEADS):
        k_cat = jnp.concatenate([prev_ref[:, g * hd:(g + 1) * hd],
                                 cur_ref[:, SWA_Q_WIDTH + g * hd:SWA_Q_WIDTH + (g + 1) * hd]], axis=0)
        v_cat = jnp.concatenate(
            [prev_ref[:, SWA_KV_WIDTH + g * hd:SWA_KV_WIDTH + (g + 1) * hd],
             cur_ref[:, SWA_Q_WIDTH + SWA_KV_WIDTH + g * hd:SWA_Q_WIDTH + SWA_KV_WIDTH + (g + 1) * hd]],
            axis=0)
        for hh in range(grp):
            h = g * grp + hh
            q = cur_ref[:, h * hd:(h + 1) * hd]
            s = _dot_nt(q, k_cat) * scale + tab_ref[h]
            s = jnp.where(valid, s, NEG_INF)
            sink = sink_ref[h]
            m = jnp.maximum(jnp.max(s, axis=-1, keepdims=True), sink)
            p = jnp.exp(s - m)
            denom = jnp.sum(p, axis=-1, keepdims=True) + jnp.exp(sink - m)
            o = _dot(p.astype(BF16), v_cat)
            outs.append(o / denom)
    out_ref[...] = jnp.concatenate(outs, axis=-1).astype(BF16)


def _swa(qkv, sinks, tab, bsz, seq):
    n = qkv.shape[0]
    nb = seq // SWA_BLOCK
    kv_col_block = SWA_Q_WIDTH // (2 * SWA_KV_WIDTH)
    return pl.pallas_call(
        _swa_kernel,
        grid=(bsz, nb),
        in_specs=[pl.BlockSpec(memory_space=pltpu.SMEM),
                  pl.BlockSpec((SWA_BLOCK, qkv.shape[1]), lambda b, i: (b * nb + i, 0)),
                  pl.BlockSpec((SWA_BLOCK, 2 * SWA_KV_WIDTH),
                               lambda b, i: (b * nb + jnp.maximum(i - 1, 0), kv_col_block)),
                  _resident((N_HEADS, SWA_BLOCK, 2 * SWA_BLOCK))],
        out_specs=pl.BlockSpec((SWA_BLOCK, SWA_Q_WIDTH), lambda b, i: (b * nb + i, 0)),
        out_shape=jax.ShapeDtypeStruct((n, SWA_Q_WIDTH), BF16),
        compiler_params=_params("arbitrary", "arbitrary"),
        name="swa",
    )(sinks, qkv, qkv, tab)


def _ple(h, p, g, w_gate, w_proj):
    gate = jax.nn.sigmoid(_dot(_rms(h, g).astype(BF16), w_gate))
    return h + gate * _dot(p.astype(BF16), w_proj)


def _even_ffn_kernel(h_ref, a_ref, b_ref, p_ref, wo_ref, gf_ref, wg_ref, wu_ref, wd_ref,
                     gp_ref, wpg_ref, wpp_ref, out_ref, *, tf):
    h1 = (h_ref[...] + _dot(a_ref[...], wo_ref[0:SC_WIDTH, :])
          + _dot(b_ref[...], wo_ref[SC_WIDTH:, :]))
    xn = _rms(h1, gf_ref[...]).astype(BF16)
    d_ff = wg_ref.shape[1]
    acc = jnp.zeros(h1.shape, F32)
    for c in range(d_ff // tf):
        gate = _dot(xn, wg_ref[:, c * tf:(c + 1) * tf])
        up = _dot(xn, wu_ref[:, c * tf:(c + 1) * tf])
        hid = (gate * jax.nn.sigmoid(gate) * up).astype(BF16)
        acc = acc + _dot(hid, wd_ref[c * tf:(c + 1) * tf, :])
    h2 = h1 + acc
    out_ref[...] = _ple(h2, p_ref[...], gp_ref[...], wpg_ref[...], wpp_ref[...])


def _layer_rows(p, layer, tm):
    return pl.BlockSpec((None, tm, p.shape[2]), lambda i: (layer, i, 0))


def _even_ffn(h, a, b, p, layer, wo, gf, wg, wu, wd, gp, wpg, wpp, tm=512, tf=256):
    n, d = h.shape
    row = lambda w: pl.BlockSpec((tm, w), lambda i: (i, 0))
    return pl.pallas_call(
        functools.partial(_even_ffn_kernel, tf=tf),
        grid=(n // tm,),
        in_specs=[row(d), row(a.shape[1]), row(b.shape[1]), _layer_rows(p, layer, tm),
                  _resident(wo.shape), _resident(gf.shape), _resident(wg.shape),
                  _resident(wu.shape), _resident(wd.shape), _resident(gp.shape),
                  _resident(wpg.shape), _resident(wpp.shape)],
        out_specs=row(d),
        out_shape=jax.ShapeDtypeStruct((n, d), F32),
        compiler_params=_params("arbitrary"),
        name="even_out_ffn_ple",
    )(h, a, b, p, wo, gf, wg, wu, wd, gp, wpg, wpp)


def _odd_proj_kernel(h_ref, g_ref, wq_ref, wk_ref, wvt_ref, q_ref, k_ref, vt_ref, kmean_ref, *, tm):
    xn = _rms(h_ref[...], g_ref[...]).astype(BF16)
    q_ref[...] = (_dot(xn, wq_ref[...]) * (MOBA_HEAD_DIM ** -0.5 * LOG2E)).astype(BF16)
    k = _dot(xn, wk_ref[...])
    k_ref[...] = k.astype(BF16)
    vt_ref[...] = _dot_nt(wvt_ref[...], xn).astype(BF16)
    for r in range(tm // MOBA_BLOCK):
        kmean_ref[r] = jnp.mean(k[r * MOBA_BLOCK:(r + 1) * MOBA_BLOCK, :], axis=0, keepdims=True)


def _odd_proj(h, g, wq, wk, wvt, tm=512):
    n, d = h.shape
    bpt = tm // MOBA_BLOCK
    row = pl.BlockSpec((tm, d), lambda i: (i, 0))
    return pl.pallas_call(
        functools.partial(_odd_proj_kernel, tm=tm),
        grid=(n // tm,),
        in_specs=[row, _resident(g.shape), _resident(wq.shape), _resident(wk.shape),
                  _resident(wvt.shape)],
        out_specs=[row, row, pl.BlockSpec((d, tm), lambda i: (0, i)),
                   pl.BlockSpec((bpt, 1, d), lambda i: (i, 0, 0))],
        out_shape=[jax.ShapeDtypeStruct((n, d), BF16), jax.ShapeDtypeStruct((n, d), BF16),
                   jax.ShapeDtypeStruct((d, n), BF16),
                   jax.ShapeDtypeStruct((n // MOBA_BLOCK, 1, d), F32)],
        compiler_params=_params("arbitrary"),
        name="odd_qkv",
    )(h, g, wq, wk, wvt)


def _moba_kernel(rb_ref, q_ref, k_ref, vt_ref, km_ref, tab_ref, out_ref,
                 sel_scr, adj_scr, m_scr, l_scr, acc_scr, far_a_scr, far_b_scr, gmax_scr, *, nblk, hg):
    head0 = pl.program_id(1) * hg
    qi = pl.program_id(2)
    bs = MOBA_BLOCK
    hd = MOBA_HEAD_DIM
    heads = [slice(hh * hd, (hh + 1) * hd) for hh in range(hg)]

    def select_blocks():
        blk = lax.broadcasted_iota(jnp.int32, (nblk, bs), 0)
        past = blk < qi
        for hh in range(hg):
            gate = lax.dot_general(km_ref[:, heads[hh]], q_ref[:, heads[hh]].astype(F32),
                                   (((1,), (1,)), ((), ())), preferred_element_type=F32,
                                   precision=lax.Precision.HIGHEST)
            g = jnp.where(past, gate, NEG_INF)
            sel = jnp.zeros(gate.shape, jnp.bool_)
            for _ in range(MOBA_TOPK):
                top = jnp.max(g, axis=0, keepdims=True)
                idx = jnp.min(jnp.where(g == top, blk, nblk), axis=0, keepdims=True)
                hit = blk == idx
                sel = sel | hit
                g = jnp.where(hit, NEG_INF, g)
            far_bias = rb_ref[N_BUCKETS - 1, head0 + hh] * LOG2E
            sel_scr[hh] = jnp.where(sel & (blk < qi - 1), far_bias, NEG_INF)
            adj_scr[hh] = jnp.max(jnp.where(sel & (blk == qi - 1), 0.0, NEG_INF), axis=0,
                                  keepdims=True)

    def score_dots(start, nb):
        return [_dot_nt(k_ref[pl.ds(start, nb * bs), heads[hh]], q_ref[:, heads[hh]])
                for hh in range(hg)]

    def softmax_update(scores, start, nb, tables, query_biases, first):
        probs, alphas = [], []
        for hh in range(hg):
            parts, s_max = [], None
            for r in range(nb):
                s_r = scores[hh][r * bs:(r + 1) * bs]
                if tables(hh, r) is not None:
                    s_r = s_r + tables(hh, r)
                col_max = jnp.max(s_r, axis=0, keepdims=True)
                if query_biases(hh, r) is not None:
                    col_max = col_max + query_biases(hh, r)
                parts.append(s_r)
                s_max = col_max if s_max is None else jnp.maximum(s_max, col_max)
            m_new = s_max if first else jnp.maximum(m_scr[hh], s_max)
            p = jnp.concatenate(
                [jnp.exp2(parts[r] - (m_new if query_biases(hh, r) is None
                                      else m_new - query_biases(hh, r))) for r in range(nb)], axis=0)
            l_new = jnp.sum(p, axis=0, keepdims=True)
            if not first:
                alpha = jnp.exp2(m_scr[hh] - m_new)
                alphas.append(alpha)
                l_new = alpha * l_scr[hh] + l_new
            m_scr[hh] = m_new
            l_scr[hh] = l_new
            probs.append(p.astype(BF16))
        for hh in range(hg):
            acc_new = _dot(vt_ref[heads[hh], pl.ds(start, nb * bs)], probs[hh])
            if not first:
                acc_new = alphas[hh] * acc_scr[hh] + acc_new
            acc_scr[hh] = acc_new

    @pl.when(qi == 0)
    def _():
        softmax_update(score_dots(0, 1), 0, 1, lambda hh, r: tab_ref[hh, :, bs:2 * bs],
                       lambda hh, r: None, first=True)

    far_rows = MOBA_FAR_GROUP * bs
    n_far = (qi - 1 + MOBA_FAR_GROUP - 1) // MOBA_FAR_GROUP
    far_bufs = (far_a_scr, far_b_scr)

    def far_start(g):
        return pl.multiple_of(g * far_rows, far_rows)

    def far_bias(hh, g, r):
        return sel_scr[hh, pl.ds(g * MOBA_FAR_GROUP + r, 1), :]

    def far_scores(g, buf):
        scores = score_dots(far_start(g), MOBA_FAR_GROUP)
        for hh in range(hg):
            far_bufs[buf][hh] = scores[hh]
            group_max = None
            for r in range(MOBA_FAR_GROUP):
                col_max = (jnp.max(scores[hh][r * bs:(r + 1) * bs], axis=0, keepdims=True)
                           + far_bias(hh, g, r))
                group_max = col_max if group_max is None else jnp.maximum(group_max, col_max)
            gmax_scr[buf, hh] = group_max

    def far_softmax(g, buf):
        probs, alphas = [], []
        for hh in range(hg):
            m_old = m_scr[hh]
            m_new = jnp.maximum(m_old, gmax_scr[buf, hh])
            alpha = jnp.exp2(m_old - m_new)
            p = jnp.concatenate(
                [jnp.exp2(far_bufs[buf][hh, r * bs:(r + 1) * bs, :] - (m_new - far_bias(hh, g, r)))
                 for r in range(MOBA_FAR_GROUP)], axis=0)
            m_scr[hh] = m_new
            l_scr[hh] = alpha * l_scr[hh] + jnp.sum(p, axis=0, keepdims=True)
            probs.append(p.astype(BF16))
            alphas.append(alpha)
        for hh in range(hg):
            acc_scr[hh] = (alphas[hh] * acc_scr[hh]
                           + _dot(vt_ref[heads[hh], pl.ds(far_start(g), far_rows)], probs[hh]))

    def far_step(g, buf, prefetch):
        if prefetch:
            far_scores(g + 1, 1 - buf)
        far_softmax(g, buf)

    @pl.when(qi > 0)
    def _():
        select_blocks()
        start = pl.multiple_of((qi - 1) * bs, bs)
        scores = score_dots(start, 2)
        far_scores(0, 0)
        softmax_update(scores, start, 2, lambda hh, r: tab_ref[hh, :, r * bs:(r + 1) * bs],
                       lambda hh, r: adj_scr[hh] if r == 0 else None, first=True)

    def far_pair(i, carry):
        far_step(2 * i, 0, True)
        far_step(2 * i + 1, 1, True)
        return carry

    lax.fori_loop(0, (n_far - 1) // 2, far_pair, 0)

    @pl.when((n_far > 0) & (n_far % 2 == 0))
    def _():
        far_step(n_far - 2, 0, True)
        far_step(n_far - 1, 1, False)

    @pl.when(n_far % 2 == 1)
    def _():
        far_step(n_far - 1, 0, False)

    for hh in range(hg):
        out_ref[:, heads[hh]] = (acc_scr[hh] / l_scr[hh]).T.astype(BF16)


def _moba(q, k, vt, kmean, tab, rel_bias, bsz, seq, hg=4):
    n, d = q.shape
    nblk = seq // MOBA_BLOCK
    assert nblk % MOBA_FAR_GROUP == 0, "far-block groups must not run past the sequence"
    bs = MOBA_BLOCK
    w = hg * MOBA_HEAD_DIM
    return pl.pallas_call(
        functools.partial(_moba_kernel, nblk=nblk, hg=hg),
        grid=(bsz, N_HEADS // hg, nblk),
        in_specs=[pl.BlockSpec(memory_space=pltpu.SMEM),
                  pl.BlockSpec((bs, w), lambda b, g, i: (b * nblk + i, g)),
                  pl.BlockSpec((seq, w), lambda b, g, i: (b, g)),
                  pl.BlockSpec((w, seq), lambda b, g, i: (g, b)),
                  pl.BlockSpec((None, nblk, w), lambda b, g, i: (b, 0, g)),
                  pl.BlockSpec((hg, bs, 2 * bs), lambda b, g, i: (g, 0, 0))],
        out_specs=pl.BlockSpec((bs, w), lambda b, g, i: (b * nblk + i, g)),
        out_shape=jax.ShapeDtypeStruct((n, d), BF16),
        scratch_shapes=[pltpu.VMEM((hg, nblk, bs), F32), pltpu.VMEM((hg, 1, bs), F32),
                        pltpu.VMEM((hg, 1, bs), F32), pltpu.VMEM((hg, 1, bs), F32),
                        pltpu.VMEM((hg, MOBA_HEAD_DIM, bs), F32),
                        pltpu.VMEM((hg, MOBA_FAR_GROUP * bs, bs), F32),
                        pltpu.VMEM((hg, MOBA_FAR_GROUP * bs, bs), F32),
                        pltpu.VMEM((2, hg, 1, bs), F32)],
        compiler_params=_params("arbitrary", "arbitrary", "arbitrary"),
        name="moba",
    )(rel_bias, q, k, vt, kmean.reshape(bsz, nblk, d), tab)


def _odd_router_kernel(h_ref, c_ref, wo_ref, g_ref, wr_ref, h_out_ref, xn_ref, gates_ref, sel_ref,
                       cnt_ref):
    wr = wr_ref[...]
    wr_hi = wr.astype(BF16)
    wr_lo = (wr - wr_hi.astype(F32)).astype(BF16)
    n_parts = 2
    part = h_ref.shape[0] // n_parts
    rows = [slice(k * part, (k + 1) * part) for k in range(n_parts)]
    h1 = [h_ref[r, :] + _dot(c_ref[r, :], wo_ref[...]) for r in rows]
    xn = [_rms(h, g_ref[...]) for h in h1]
    xn_hi = [x.astype(BF16) for x in xn]
    xn_lo = [(x - hi.astype(F32)).astype(BF16) for x, hi in zip(xn, xn_hi)]
    logits = [_dot(hi, wr_hi) + (_dot(lo, wr_hi) + _dot(hi, wr_lo)) for hi, lo in zip(xn_hi, xn_lo)]
    count = jnp.zeros((1, N_EXPERTS), F32)
    for k, r in enumerate(rows):
        h_out_ref[r, :] = h1[k]
        xn_ref[r, :] = xn_hi[k]
        lane = lax.broadcasted_iota(jnp.int32, logits[k].shape, 1)
        v1 = jnp.max(logits[k], axis=-1, keepdims=True)
        i1 = jnp.min(jnp.where(logits[k] == v1, lane, N_EXPERTS), axis=-1, keepdims=True)
        rest = jnp.where(lane == i1, NEG_INF, logits[k])
        v2 = jnp.max(rest, axis=-1, keepdims=True)
        i2 = jnp.min(jnp.where(rest == v2, lane, N_EXPERTS), axis=-1, keepdims=True)
        e2 = jnp.exp(v2 - v1)
        w1 = 1.0 / (1.0 + e2)
        w2 = e2 / (1.0 + e2)
        gates_ref[r, :] = jnp.where(lane == i1, w1, 0.0) + jnp.where(lane == i2, w2, 0.0)
        sel = jnp.where((lane == i1) | (lane == i2), 1.0, 0.0)
        sel_ref[r, :] = sel
        count = count + jnp.sum(sel, axis=0, keepdims=True)
    cnt_ref[0] = count


def _odd_router(h, c, wo, g, wr, tm=MOE_TOKEN_TILE):
    n, d = h.shape
    row = lambda w: pl.BlockSpec((tm, w), lambda i: (i, 0))
    return pl.pallas_call(
        _odd_router_kernel,
        grid=(n // tm,),
        in_specs=[row(d), row(d), _resident(wo.shape), _resident(g.shape), _resident(wr.shape)],
        out_specs=[row(d), row(d), row(N_EXPERTS), row(N_EXPERTS),
                   pl.BlockSpec((1, 1, N_EXPERTS), lambda i: (i, 0, 0))],
        out_shape=[jax.ShapeDtypeStruct((n, d), F32), jax.ShapeDtypeStruct((n, d), BF16),
                   jax.ShapeDtypeStruct((n, N_EXPERTS), F32),
                   jax.ShapeDtypeStruct((n, N_EXPERTS), F32),
                   jax.ShapeDtypeStruct((n // tm, 1, N_EXPERTS), F32)],
        compiler_params=_params("arbitrary"),
        name="odd_out_router",
    )(h, c, wo, g, wr)


def _moe_plan(counts, n_slot_tiles):
    cnt = counts.reshape(-1, N_EXPERTS).astype(jnp.int32)
    grp = (cnt + MOE_ALIGN - 1) // MOE_ALIGN * MOE_ALIGN
    local = jnp.cumsum(grp, axis=1) - grp
    tot = jnp.sum(grp, axis=0)
    region = (tot + MOE_SLOT_TILE - 1) // MOE_SLOT_TILE * MOE_SLOT_TILE
    region_end = jnp.cumsum(region)
    region_start = region_end - region
    start = region_start[None, :] + jnp.cumsum(grp, axis=0) - grp
    n_used = region_end[-1] // MOE_SLOT_TILE
    tile_row = jnp.arange(n_slot_tiles, dtype=jnp.int32) * MOE_SLOT_TILE
    tile_expert = jnp.sum((tile_row[:, None] >= region_end[None, :]).astype(jnp.int32), axis=1)
    tile_expert = jnp.minimum(tile_expert, N_EXPERTS - 1)
    tile_expert = jnp.where(tile_row < region_end[-1], tile_expert, tile_expert[n_used - 1])
    return dict(cnt=cnt, local=local.astype(jnp.int32), start=start.astype(jnp.int32),
                tail_start=(region_start + tot).astype(jnp.int32),
                tail_len=(region - tot).astype(jnp.int32), tile_expert=tile_expert,
                n_used=n_used.reshape(1).astype(jnp.int32))


def _group_copies(cnt_ref, local_ref, start_ref, t, local_buf, slot_array, sem, to_slots):
    copies = []
    for e in range(N_EXPERTS):
        rows = (cnt_ref[t, e] + MOE_ALIGN - 1) // MOE_ALIGN * MOE_ALIGN
        size = MOE_TOKEN_TILE
        while size >= MOE_ALIGN:
            off = rows & ~(2 * size - 1)
            local = local_buf.at[pl.ds(pl.multiple_of(local_ref[t, e] + off, MOE_ALIGN), size)]
            slots = slot_array.at[pl.ds(pl.multiple_of(start_ref[t, e] + off, MOE_ALIGN), size)]
            src, dst = (local, slots) if to_slots else (slots, local)
            copies.append(((rows & size) != 0, pltpu.make_async_copy(src, dst, sem)))
            size //= 2
    return copies


def _start_all(copies):
    for cond, cp in copies:
        pl.when(cond)(cp.start)


def _wait_all(copies):
    for cond, cp in copies:
        pl.when(cond)(cp.wait)


def _moe_dispatch_kernel(cnt_ref, local_ref, start_ref, tail_start_ref, tail_len_ref, x_ref, sel_ref,
                         upper_ref, xs_ref, stage_ref, zero_ref, sem, tail_sem):
    t = pl.program_id(0)
    last = pl.num_programs(0) - 1
    tile = x_ref.shape[0]
    slot = t % 2
    eye = jnp.where(lax.broadcasted_iota(jnp.int32, (N_EXPERTS, N_EXPERTS), 0)
                    == lax.broadcasted_iota(jnp.int32, (N_EXPERTS, N_EXPERTS), 1), 1.0, 0.0)
    sel_t = _dot_nt(eye.astype(BF16), sel_ref[...].astype(BF16))
    rank_t = _dot(sel_t.astype(BF16), upper_ref[...])
    expert = lax.broadcasted_iota(jnp.int32, (N_EXPERTS, 1), 0)
    base = jnp.zeros((N_EXPERTS, 1), F32)
    for e in range(N_EXPERTS):
        base = jnp.where(expert == e, local_ref[t, e].astype(F32), base)
    pos = base + rank_t
    pos_lo = jnp.min(jnp.where(sel_t > 0.5, pos, float(MOE_LOCAL_ROWS)), axis=0, keepdims=True)
    pos_hi = jnp.max(jnp.where(sel_t > 0.5, pos, -1.0), axis=0, keepdims=True)
    row = lax.broadcasted_iota(jnp.int32, (MOE_LOCAL_ROWS, tile), 0).astype(F32)
    onehot = jnp.where(row == pos_lo, 1.0, jnp.where(row == pos_hi, 1.0, 0.0)).astype(BF16)
    stage_ref[slot] = _dot(onehot, x_ref[...]).astype(BF16)

    def copies(step):
        return _group_copies(cnt_ref, local_ref, start_ref, step, stage_ref.at[step % 2], xs_ref,
                             sem.at[step % 2], to_slots=True)

    _start_all(copies(t))

    @pl.when(t > 0)
    def _():
        _wait_all(copies(t - 1))

    @pl.when(t == last)
    def _():
        _wait_all(copies(t))
        zero_ref[...] = jnp.zeros(zero_ref.shape, BF16)
        tails = []
        for e in range(N_EXPERTS):
            size = MOE_SLOT_TILE // 2
            while size >= MOE_ALIGN:
                off = tail_len_ref[e] & ~(2 * size - 1)
                dst = pl.multiple_of(tail_start_ref[e] + off, MOE_ALIGN)
                tails.append(((tail_len_ref[e] & size) != 0, pltpu.make_async_copy(
                    zero_ref.at[pl.ds(0, size)], xs_ref.at[pl.ds(dst, size)], tail_sem)))
                size //= 2
        _start_all(tails)
        _wait_all(tails)

        used_rows = tail_start_ref[N_EXPERTS - 1] + tail_len_ref[N_EXPERTS - 1]
        piece = zero_ref.shape[0]

        def unused_copy(k):
            dst = pl.multiple_of(used_rows + k * piece, piece)
            return pltpu.make_async_copy(zero_ref, xs_ref.at[pl.ds(dst, piece)], tail_sem)

        n_pieces = (xs_ref.shape[0] - used_rows) // piece
        lax.fori_loop(0, n_pieces, lambda k, c: (unused_copy(k).start(), c)[1], 0)
        lax.fori_loop(0, n_pieces, lambda k, c: (unused_copy(k).wait(), c)[1], 0)


def _moe_dispatch(xn, sel, plan, n_slot_tiles, tm=MOE_TOKEN_TILE):
    n, d = xn.shape
    upper = jnp.asarray(np.triu(np.ones((tm, tm), np.float32), 1), BF16)
    smem = pl.BlockSpec(memory_space=pltpu.SMEM)
    return pl.pallas_call(
        _moe_dispatch_kernel,
        grid=(n // tm,),
        in_specs=[smem, smem, smem, smem, smem,
                  pl.BlockSpec((tm, d), lambda i: (i, 0)),
                  pl.BlockSpec((tm, N_EXPERTS), lambda i: (i, 0)),
                  _resident((tm, tm))],
        out_specs=pl.BlockSpec(memory_space=pl.ANY),
        out_shape=jax.ShapeDtypeStruct((n_slot_tiles * MOE_SLOT_TILE, d), BF16),
        scratch_shapes=[pltpu.VMEM((2, MOE_LOCAL_ROWS, d), BF16),
                        pltpu.VMEM((MOE_SLOT_TILE // 2, d), BF16),
                        pltpu.SemaphoreType.DMA((2,)), pltpu.SemaphoreType.DMA(())],
        compiler_params=_params("arbitrary"),
        name="moe_dispatch",
    )(plan["cnt"], plan["local"], plan["start"], plan["tail_start"], plan["tail_len"], xn, sel, upper)


def _moe_experts_kernel(te_ref, nu_ref, x_ref, wg_ref, wu_ref, wd_ref, y_ref, acc_ref):
    i = pl.program_id(0)
    f = pl.program_id(1)
    last = pl.num_programs(1) - 1
    used = i < nu_ref[0]

    @pl.when(used & (f == 0))
    def _():
        acc_ref[...] = jnp.zeros(acc_ref.shape, F32)

    @pl.when(used)
    def _():
        x = x_ref[...]
        acc = acc_ref[...]
        for c in range(wg_ref.shape[1] // MOE_FF_CHUNK):
            cols = slice(c * MOE_FF_CHUNK, (c + 1) * MOE_FF_CHUNK)
            gate = _dot(x, wg_ref[:, cols])
            up = _dot(x, wu_ref[:, cols])
            hid = (gate * jax.nn.sigmoid(gate) * up).astype(BF16)
            acc = acc + _dot(hid, wd_ref[cols, :])
        acc_ref[...] = acc

    @pl.when(used & (f == last))
    def _():
        y_ref[...] = acc_ref[...].astype(BF16)

    @pl.when(jnp.logical_not(used) & (f == last))
    def _():
        y_ref[...] = jnp.zeros(y_ref.shape, BF16)


def _moe_experts(xs, plan, wg, wu, wd, tf=1792):
    rows, d = xs.shape
    d_ff = wg.shape[2]
    nf = d_ff // tf
    tm = MOE_SLOT_TILE
    f_idx = lambda i, f, nu: jnp.where(i < nu[0], f, nf - 1)
    return pl.pallas_call(
        _moe_experts_kernel,
        grid_spec=pltpu.PrefetchScalarGridSpec(
            num_scalar_prefetch=2,
            grid=(rows // tm, nf),
            in_specs=[pl.BlockSpec((tm, d), lambda i, f, te, nu: (jnp.minimum(i, nu[0] - 1), 0)),
                      pl.BlockSpec((None, d, tf), lambda i, f, te, nu: (te[i], 0, f_idx(i, f, nu))),
                      pl.BlockSpec((None, d, tf), lambda i, f, te, nu: (te[i], 0, f_idx(i, f, nu))),
                      pl.BlockSpec((None, tf, d), lambda i, f, te, nu: (te[i], f_idx(i, f, nu), 0))],
            out_specs=pl.BlockSpec((tm, d), lambda i, f, te, nu: (i, 0)),
            scratch_shapes=[pltpu.VMEM((tm, d), F32)]),
        out_shape=jax.ShapeDtypeStruct((rows, d), BF16),
        compiler_params=_params("arbitrary", "arbitrary"),
        name="moe_experts",
    )(plan["tile_expert"], plan["n_used"], xs, wg, wu, wd)


def _moe_combine_kernel(cnt_ref, local_ref, start_ref, h_ref, gates_ref, sel_ref, lower_ref, p_ref,
                        gp_ref, wpg_ref, wpp_ref, gfin_ref, ys_ref, out_ref, ybuf_ref, sem):
    t = pl.program_id(0)
    tile = h_ref.shape[0]

    def copies(step):
        return _group_copies(cnt_ref, local_ref, start_ref, step, ybuf_ref.at[step % 2], ys_ref,
                             sem.at[step % 2], to_slots=False)

    @pl.when(t == 0)
    def _():
        ybuf_ref[...] = jnp.zeros(ybuf_ref.shape, BF16)
        _start_all(copies(t))

    @pl.when(t + 1 < pl.num_programs(0))
    def _():
        _start_all(copies(t + 1))

    sel = sel_ref[...] > 0.5
    gates = gates_ref[...]
    rank = _dot(lower_ref[...], sel_ref[...].astype(BF16))
    expert = lax.broadcasted_iota(jnp.int32, (1, N_EXPERTS), 1)
    base = jnp.zeros((1, N_EXPERTS), F32)
    for e in range(N_EXPERTS):
        base = jnp.where(expert == e, local_ref[t, e].astype(F32), base)
    pos = base + rank
    pos_lo = jnp.min(jnp.where(sel, pos, float(MOE_LOCAL_ROWS)), axis=1, keepdims=True)
    pos_hi = jnp.max(jnp.where(sel, pos, -1.0), axis=1, keepdims=True)
    gate_lo = jnp.sum(jnp.where(sel & (pos == pos_lo), gates, 0.0), axis=1, keepdims=True)
    gate_hi = jnp.sum(jnp.where(sel & (pos == pos_hi), gates, 0.0), axis=1, keepdims=True)
    col = lax.broadcasted_iota(jnp.int32, (tile, MOE_LOCAL_ROWS), 1).astype(F32)
    onehot_lo = jnp.where(col == pos_lo, 1.0, 0.0).astype(BF16)
    onehot_hi = jnp.where(col == pos_hi, 1.0, 0.0).astype(BF16)

    _wait_all(copies(t))
    y = ybuf_ref[t % 2]
    h = h_ref[...] + gate_lo * _dot(onehot_lo, y) + gate_hi * _dot(onehot_hi, y)
    h = _ple(h, p_ref[...], gp_ref[...], wpg_ref[...], wpp_ref[...])
    out_ref[...] = _rms(h, gfin_ref[...])


def _moe_combine(h, gates, sel, ys, plan, p, layer, gp, wpg, wpp, gfin, tm=MOE_TOKEN_TILE):
    n, d = h.shape
    lower = jnp.asarray(np.tril(np.ones((tm, tm), np.float32), -1), BF16)
    smem = pl.BlockSpec(memory_space=pltpu.SMEM)
    row = lambda w: pl.BlockSpec((tm, w), lambda i: (i, 0))
    return pl.pallas_call(
        _moe_combine_kernel,
        grid=(n // tm,),
        in_specs=[smem, smem, smem, row(d), row(N_EXPERTS), row(N_EXPERTS), _resident((tm, tm)),
                  _layer_rows(p, layer, tm), _resident(gp.shape), _resident(wpg.shape),
                  _resident(wpp.shape), _resident(gfin.shape), pl.BlockSpec(memory_space=pl.ANY)],
        out_specs=row(d),
        out_shape=jax.ShapeDtypeStruct((n, d), F32),
        scratch_shapes=[pltpu.VMEM((2, MOE_LOCAL_ROWS, d), BF16), pltpu.SemaphoreType.DMA((2,))],
        compiler_params=_params("arbitrary"),
        name="moe_combine_ple_norm",
    )(plan["cnt"], plan["local"], plan["start"], h, gates, sel, lower, p, gp, wpg, wpp, gfin, ys)


def kernel(x, p, rel_bias, final_norm, e_norm_mix, e_w_in, e_conv_w, e_sinks, e_w_out, e_norm_ffn, e_ffn_gate, e_ffn_up, e_ffn_down, e_norm_ple, e_ple_gate, e_ple_proj, o_norm_mix, o_w_qkv, o_w_o, o_norm_ffn, o_router, o_exp_gate, o_exp_up, o_exp_down, o_norm_ple, o_ple_gate, o_ple_proj):
    bsz, seq, d = x.shape
    n = bsz * seq
    bf = lambda w: w.astype(BF16)
    h = x.reshape(n, d)
    p2 = p.reshape(p.shape[0], n, p.shape[-1])

    qi = np.arange(SWA_BLOCK)[:, None]
    kj = np.arange(2 * SWA_BLOCK)[None, :]
    swa_tab = _bias_table(rel_bias, _t5_bucket_np(qi - kj + SWA_BLOCK))
    kk = np.arange(MOBA_BLOCK)[:, None]
    qq = np.arange(MOBA_BLOCK)[None, :]
    own_bucket = np.where(qq >= kk, _t5_bucket_np(qq - kk), -1)
    moba_bucket = np.concatenate([_t5_bucket_np(qq - kk + MOBA_BLOCK), own_bucket], axis=1)
    moba_tab = _bias_table(rel_bias, moba_bucket, mult=LOG2E)

    a_out, qkv0 = _even_proj(h, e_norm_mix[0:1], bf(e_w_in[0]), e_conv_w[0], seq)
    b_out = _swa(qkv0, e_sinks[0], swa_tab, bsz, seq)
    h = _even_ffn(h, a_out, b_out, p2, 0, bf(e_w_out[0]), e_norm_ffn[0:1],
                  bf(e_ffn_gate[0]), bf(e_ffn_up[0]), bf(e_ffn_down[0]),
                  e_norm_ple[0:1], bf(e_ple_gate[0]), bf(e_ple_proj[0]))

    w_qkv = o_w_qkv[0]
    q1, k1, vt1, kmean = _odd_proj(h, o_norm_mix[0:1], bf(w_qkv[:, :d]), bf(w_qkv[:, d:2 * d]),
                                   bf(w_qkv[:, 2 * d:].T))
    c_out = _moba(q1, k1, vt1, kmean, moba_tab, rel_bias, bsz, seq)
    h, xn, gates, sel, counts = _odd_router(h, c_out, bf(o_w_o[0]), o_norm_ffn[0:1], o_router[0])
    n_groups = (n // MOE_TOKEN_TILE) * N_EXPERTS
    max_rows = 2 * n + n_groups * (MOE_ALIGN - 1) + N_EXPERTS * (MOE_SLOT_TILE - 1)
    n_slot_tiles = -(-max_rows // MOE_SLOT_TILE) + 1
    plan = _moe_plan(counts, n_slot_tiles)
    xs = _moe_dispatch(xn, sel, plan, n_slot_tiles)
    ys = _moe_experts(xs, plan, bf(o_exp_gate[0]), bf(o_exp_up[0]), bf(o_exp_down[0]))
    out = _moe_combine(h, gates, sel, ys, plan, p2, 1, o_norm_ple[0:1], bf(o_ple_gate[0]),
                       bf(o_ple_proj[0]), final_norm.reshape(1, d))
    return out.reshape(bsz, seq, d)
```

```python
import functools
import math

import jax
import jax.numpy as jnp
import numpy as np
from jax import lax
from jax.experimental import pallas as pl
from jax.experimental.pallas import tpu as pltpu

F32 = jnp.float32
BF16 = jnp.bfloat16

EPS = 1e-6
N_HEADS = 8
N_BUCKETS = 32
MAX_DISTANCE = 128

SC_WIDTH = 512
CONV_WIDTH = 3
SWA_KV_HEADS = 2
SWA_HEAD_DIM = 64
SWA_BLOCK = 128
SWA_Q_WIDTH = N_HEADS * SWA_HEAD_DIM
SWA_KV_WIDTH = SWA_KV_HEADS * SWA_HEAD_DIM

MOBA_HEAD_DIM = 128
MOBA_BLOCK = 256
MOBA_TOPK = 3
MOBA_FAR_GROUP = 2

N_EXPERTS = 8
MOE_TOKEN_TILE = 512
MOE_SLOT_TILE = 512
MOE_ALIGN = 16
MOE_FF_CHUNK = 256
MOE_LOCAL_ROWS = -(-(2 * MOE_TOKEN_TILE + N_EXPERTS * (MOE_ALIGN - 1)) // 128) * 128

VMEM_LIMIT_BYTES = 56 * 1024 * 1024
NEG_INF = float("-inf")
LOG2E = math.log2(math.e)


def _params(*semantics):
    return pltpu.CompilerParams(dimension_semantics=semantics,
                                vmem_limit_bytes=VMEM_LIMIT_BYTES)


def _resident(shape):
    zeros = (0,) * len(shape)
    return pl.BlockSpec(shape, lambda *_: zeros, pipeline_mode=pl.Buffered(1))


def _rms(x, g):
    return x * lax.rsqrt(jnp.mean(x * x, axis=-1, keepdims=True) + EPS) * g


def _dot(a, b):
    return jnp.dot(a, b, preferred_element_type=F32)


def _dot_nt(a, b):
    return lax.dot_general(a, b, (((1,), (1,)), ((), ())), preferred_element_type=F32)


def _t5_bucket_np(dist):
    d = np.maximum(dist, 0).astype(np.int32)
    max_exact = N_BUCKETS // 2
    scaled = (np.log(np.maximum(d, max_exact).astype(np.float32) / np.float32(max_exact))
              / np.float32(math.log(MAX_DISTANCE / max_exact)))
    large = np.minimum(max_exact + (scaled * (N_BUCKETS - max_exact)).astype(np.int32),
                       N_BUCKETS - 1)
    return np.where(d < max_exact, d, large).astype(np.int32)


def _bias_table_kernel(rb_ref, bkt_ref, out_ref, *, mult):
    h = pl.program_id(0)
    bkt = bkt_ref[...]
    acc = jnp.full(bkt.shape, NEG_INF, F32)
    for b in range(N_BUCKETS):
        acc = jnp.where(bkt == b, rb_ref[b, h], acc)
    out_ref[...] = acc * mult


def _bias_table(rel_bias, bucket, mult=1.0):
    rows, cols = bucket.shape
    return pl.pallas_call(
        functools.partial(_bias_table_kernel, mult=mult),
        grid=(N_HEADS,),
        in_specs=[pl.BlockSpec(memory_space=pltpu.SMEM),
                  pl.BlockSpec((rows, cols), lambda h: (0, 0))],
        out_specs=pl.BlockSpec((None, rows, cols), lambda h: (h, 0, 0)),
        out_shape=jax.ShapeDtypeStruct((N_HEADS, rows, cols), F32),
        compiler_params=_params("arbitrary"),
        name="bias_table",
    )(rel_bias, jnp.asarray(bucket))


def _even_proj_kernel(x_ref, g_ref, w_ref, wvt_ref, cw_ref, a_ref, q_ref, k_ref, vt_ref, cu_scr, *,
                      tm, tiles_per_seq):
    i = pl.program_id(0)
    cu_scr[0:8, :] = jnp.where(i % tiles_per_seq == 0, 0.0, cu_scr[tm:tm + 8, :])
    xn = _rms(x_ref[...], g_ref[...]).astype(BF16)
    q0 = 3 * SC_WIDTH
    k0 = q0 + SWA_Q_WIDTH
    c_gate = _dot(xn, w_ref[:, SC_WIDTH:2 * SC_WIDTH])
    u = _dot(xn, w_ref[:, 2 * SC_WIDTH:3 * SC_WIDTH])
    q = _dot(xn, w_ref[:, q0:k0])
    k = _dot(xn, w_ref[:, k0:])
    vt = _dot_nt(wvt_ref[...], xn)
    b_gate = _dot(xn, w_ref[:, 0:SC_WIDTH])
    cu = c_gate * u
    cu_scr[8:8 + tm, :] = cu
    y = (cw_ref[0:1, :] * cu_scr[6:6 + tm, :] + cw_ref[1:2, :] * cu_scr[7:7 + tm, :]
         + cw_ref[2:3, :] * cu)
    q_ref[...] = (q * (SWA_HEAD_DIM ** -0.5 * LOG2E)).astype(BF16)
    k_ref[...] = k.astype(BF16)
    vt_ref[...] = vt.astype(BF16)
    a_ref[...] = (b_gate * y).astype(BF16)


def _even_proj(x, g, w, wvt, conv_w, seq, tm=512):
    n, d = x.shape
    row = lambda width: pl.BlockSpec((tm, width), lambda i: (i, 0))
    return pl.pallas_call(
        functools.partial(_even_proj_kernel, tm=tm, tiles_per_seq=seq // tm),
        grid=(n // tm,),
        in_specs=[row(d), _resident((1, d)), _resident(w.shape), _resident(wvt.shape),
                  _resident((CONV_WIDTH, SC_WIDTH))],
        out_specs=[row(SC_WIDTH), row(SWA_Q_WIDTH), row(SWA_KV_WIDTH),
                   pl.BlockSpec((SWA_KV_WIDTH, tm), lambda i: (0, i))],
        out_shape=[jax.ShapeDtypeStruct((n, SC_WIDTH), BF16),
                   jax.ShapeDtypeStruct((n, SWA_Q_WIDTH), BF16),
                   jax.ShapeDtypeStruct((n, SWA_KV_WIDTH), BF16),
                   jax.ShapeDtypeStruct((SWA_KV_WIDTH, n), BF16)],
        scratch_shapes=[pltpu.VMEM((tm + 8, SC_WIDTH), F32)],
        compiler_params=_params("arbitrary"),
        name="even_proj_conv",
    )(x, g, w, wvt, conv_w)


def _swa_kernel(sink_ref, q_ref, k_ref, kp_ref, vt_ref, vtp_ref, tab_ref, out_ref, *, nq):
    i = pl.program_id(1)
    blk = SWA_BLOCK
    hd = SWA_HEAD_DIM
    half = N_HEADS // SWA_KV_HEADS
    lane = lax.broadcasted_iota(jnp.int32, (blk, 2 * hd), 1)
    k_all = jnp.concatenate([kp_ref[...], k_ref[...]], axis=0)
    vt_all = jnp.concatenate([vtp_ref[...], vt_ref[...]], axis=1)

    scores = {}
    for s in range(nq):
        keys = k_all[s * blk:(s + 2) * blk]
        for j in range(half):
            q_pair = q_ref[s * blk:(s + 1) * blk, j * 2 * hd:(j + 1) * 2 * hd]
            scores[s, j] = _dot_nt(keys, jnp.where(lane < hd, q_pair, 0))
            scores[s, half + j] = _dot_nt(keys, jnp.where(lane >= hd, q_pair, 0))

    probs, denoms = {}, {}
    for s in range(nq):
        for h in range(N_HEADS):
            tab = tab_ref[h]
            if s == 0:
                tab = jnp.concatenate([jnp.where(i > 0, tab[:blk], NEG_INF), tab[blk:]], axis=0)
            sc = scores[s, h] + tab
            sink = sink_ref[h] * LOG2E
            m = jnp.maximum(jnp.max(sc, axis=0, keepdims=True), sink)
            p = jnp.exp2(sc - m)
            denoms[s, h] = jnp.sum(p, axis=0, keepdims=True) + jnp.exp2(sink - m)
            probs[s, h] = p.astype(BF16)

    for s in range(nq):
        outs = []
        for h in range(N_HEADS):
            g = h // half
            vt = vt_all[g * hd:(g + 1) * hd, s * blk:(s + 2) * blk]
            outs.append(_dot(vt, probs[s, h]) / denoms[s, h])
        out_ref[s * blk:(s + 1) * blk, :] = jnp.concatenate(outs, axis=0).T.astype(BF16)


def _swa(q, k, vt, sinks, tab, bsz, seq, nq=4):
    n = q.shape[0]
    nb = seq // SWA_BLOCK
    steps = nb // nq
    prev = lambda b, i: b * nb + jnp.maximum(i * nq - 1, 0)
    return pl.pallas_call(
        functools.partial(_swa_kernel, nq=nq),
        grid=(bsz, steps),
        in_specs=[pl.BlockSpec(memory_space=pltpu.SMEM),
                  pl.BlockSpec((nq * SWA_BLOCK, SWA_Q_WIDTH), lambda b, i: (b * steps + i, 0)),
                  pl.BlockSpec((nq * SWA_BLOCK, SWA_KV_WIDTH), lambda b, i: (b * steps + i, 0)),
                  pl.BlockSpec((SWA_BLOCK, SWA_KV_WIDTH), lambda b, i: (prev(b, i), 0)),
                  pl.BlockSpec((SWA_KV_WIDTH, nq * SWA_BLOCK), lambda b, i: (0, b * steps + i)),
                  pl.BlockSpec((SWA_KV_WIDTH, SWA_BLOCK), lambda b, i: (0, prev(b, i))),
                  _resident(tab.shape)],
        out_specs=pl.BlockSpec((nq * SWA_BLOCK, SWA_Q_WIDTH), lambda b, i: (b * steps + i, 0)),
        out_shape=jax.ShapeDtypeStruct((n, SWA_Q_WIDTH), BF16),
        compiler_params=_params("arbitrary", "arbitrary"),
        name="swa",
    )(sinks, q, k, k, vt, vt, tab)


def _ple(h, p, g, w_gate, w_proj):
    gate = jax.nn.sigmoid(_dot(_rms(h, g).astype(BF16), w_gate))
    return h + gate * _dot(p.astype(BF16), w_proj)


def _even_ffn_kernel(h_ref, a_ref, b_ref, p_ref, wo_ref, gf_ref, wg_ref, wu_ref, wd_ref,
                     gp_ref, wpg_ref, wpp_ref, out_ref, *, tf):
    h1 = (h_ref[...] + _dot(a_ref[...], wo_ref[0:SC_WIDTH, :])
          + _dot(b_ref[...], wo_ref[SC_WIDTH:, :]))
    xn = _rms(h1, gf_ref[...]).astype(BF16)
    d_ff = wg_ref.shape[1]
    acc = jnp.zeros(h1.shape, F32)
    for c in range(d_ff // tf):
        gate = _dot(xn, wg_ref[:, c * tf:(c + 1) * tf])
        up = _dot(xn, wu_ref[:, c * tf:(c + 1) * tf])
        hid = (gate * jax.nn.sigmoid(gate) * up).astype(BF16)
        acc = acc + _dot(hid, wd_ref[c * tf:(c + 1) * tf, :])
    h2 = h1 + acc
    out_ref[...] = _ple(h2, p_ref[...], gp_ref[...], wpg_ref[...], wpp_ref[...])


def _layer_rows(p, layer, tm):
    return pl.BlockSpec((None, tm, p.shape[2]), lambda i: (layer, i, 0))


def _even_ffn(h, a, b, p, layer, wo, gf, wg, wu, wd, gp, wpg, wpp, tm=512, tf=256):
    n, d = h.shape
    row = lambda w: pl.BlockSpec((tm, w), lambda i: (i, 0))
    return pl.pallas_call(
        functools.partial(_even_ffn_kernel, tf=tf),
        grid=(n // tm,),
        in_specs=[row(d), row(a.shape[1]), row(b.shape[1]), _layer_rows(p, layer, tm),
                  _resident(wo.shape), _resident(gf.shape), _resident(wg.shape),
                  _resident(wu.shape), _resident(wd.shape), _resident(gp.shape),
                  _resident(wpg.shape), _resident(wpp.shape)],
        out_specs=row(d),
        out_shape=jax.ShapeDtypeStruct((n, d), F32),
        compiler_params=_params("arbitrary"),
        name="even_out_ffn_ple",
    )(h, a, b, p, wo, gf, wg, wu, wd, gp, wpg, wpp)


def _odd_proj_kernel(h_ref, g_ref, wq_ref, wk_ref, wvt_ref, q_ref, k_ref, vt_ref, kmean_ref, *, tm):
    xn = _rms(h_ref[...], g_ref[...]).astype(BF16)
    q_ref[...] = (_dot(xn, wq_ref[...]) * (MOBA_HEAD_DIM ** -0.5 * LOG2E)).astype(BF16)
    k = _dot(xn, wk_ref[...])
    k_ref[...] = k.astype(BF16)
    vt_ref[...] = _dot_nt(wvt_ref[...], xn).astype(BF16)
    for r in range(tm // MOBA_BLOCK):
        kmean_ref[r] = jnp.mean(k[r * MOBA_BLOCK:(r + 1) * MOBA_BLOCK, :], axis=0, keepdims=True)


def _odd_proj(h, g, wq, wk, wvt, tm=512):
    n, d = h.shape
    bpt = tm // MOBA_BLOCK
    row = pl.BlockSpec((tm, d), lambda i: (i, 0))
    return pl.pallas_call(
        functools.partial(_odd_proj_kernel, tm=tm),
        grid=(n // tm,),
        in_specs=[row, _resident(g.shape), _resident(wq.shape), _resident(wk.shape),
                  _resident(wvt.shape)],
        out_specs=[row, row, pl.BlockSpec((d, tm), lambda i: (0, i)),
                   pl.BlockSpec((bpt, 1, d), lambda i: (i, 0, 0))],
        out_shape=[jax.ShapeDtypeStruct((n, d), BF16), jax.ShapeDtypeStruct((n, d), BF16),
                   jax.ShapeDtypeStruct((d, n), BF16),
                   jax.ShapeDtypeStruct((n // MOBA_BLOCK, 1, d), F32)],
        compiler_params=_params("arbitrary"),
        name="odd_qkv",
    )(h, g, wq, wk, wvt)


def _moba_kernel(rb_ref, q_ref, k_ref, vt_ref, km_ref, tab_ref, out_ref,
                 sel_scr, adj_scr, m_scr, l_scr, acc_scr, far_a_scr, far_b_scr, gmax_scr, *, nblk, hg):
    head0 = pl.program_id(1) * hg
    qi = pl.program_id(2)
    bs = MOBA_BLOCK
    hd = MOBA_HEAD_DIM
    heads = [slice(hh * hd, (hh + 1) * hd) for hh in range(hg)]

    def select_blocks():
        blk = lax.broadcasted_iota(jnp.int32, (nblk, bs), 0)
        past = blk < qi
        for hh in range(hg):
            gate = lax.dot_general(km_ref[:, heads[hh]], q_ref[:, heads[hh]].astype(F32),
                                   (((1,), (1,)), ((), ())), preferred_element_type=F32,
                                   precision=lax.Precision.HIGHEST)
            g = jnp.where(past, gate, NEG_INF)
            sel = jnp.zeros(gate.shape, jnp.bool_)
            for _ in range(MOBA_TOPK):
                top = jnp.max(g, axis=0, keepdims=True)
                idx = jnp.min(jnp.where(g == top, blk, nblk), axis=0, keepdims=True)
                hit = blk == idx
                sel = sel | hit
                g = jnp.where(hit, NEG_INF, g)
            far_bias = rb_ref[N_BUCKETS - 1, head0 + hh] * LOG2E
            sel_scr[hh] = jnp.where(sel & (blk < qi - 1), far_bias, NEG_INF)
            adj_scr[hh] = jnp.max(jnp.where(sel & (blk == qi - 1), 0.0, NEG_INF), axis=0,
                                  keepdims=True)

    def score_dots(start, nb):
        return [_dot_nt(k_ref[pl.ds(start, nb * bs), heads[hh]], q_ref[:, heads[hh]])
                for hh in range(hg)]

    def softmax_update(scores, start, nb, tables, query_biases, first):
        probs, alphas = [], []
        for hh in range(hg):
            parts, s_max = [], None
            for r in range(nb):
                s_r = scores[hh][r * bs:(r + 1) * bs]
                if tables(hh, r) is not None:
                    s_r = s_r + tables(hh, r)
                col_max = jnp.max(s_r, axis=0, keepdims=True)
                if query_biases(hh, r) is not None:
                    col_max = col_max + query_biases(hh, r)
                parts.append(s_r)
                s_max = col_max if s_max is None else jnp.maximum(s_max, col_max)
            m_new = s_max if first else jnp.maximum(m_scr[hh], s_max)
            p = jnp.concatenate(
                [jnp.exp2(parts[r] - (m_new if query_biases(hh, r) is None
                                      else m_new - query_biases(hh, r))) for r in range(nb)], axis=0)
            l_new = jnp.sum(p, axis=0, keepdims=True)
            if not first:
                alpha = jnp.exp2(m_scr[hh] - m_new)
                alphas.append(alpha)
                l_new = alpha * l_scr[hh] + l_new
            m_scr[hh] = m_new
            l_scr[hh] = l_new
            probs.append(p.astype(BF16))
        for hh in range(hg):
            acc_new = _dot(vt_ref[heads[hh], pl.ds(start, nb * bs)], probs[hh])
            if not first:
                acc_new = alphas[hh] * acc_scr[hh] + acc_new
            acc_scr[hh] = acc_new

    @pl.when(qi == 0)
    def _():
        softmax_update(score_dots(0, 1), 0, 1, lambda hh, r: tab_ref[hh, :, bs:2 * bs],
                       lambda hh, r: None, first=True)

    far_rows = MOBA_FAR_GROUP * bs
    n_far = (qi - 1 + MOBA_FAR_GROUP - 1) // MOBA_FAR_GROUP
    far_bufs = (far_a_scr, far_b_scr)

    def far_start(g):
        return pl.multiple_of(g * far_rows, far_rows)

    def far_bias(hh, g, r):
        return sel_scr[hh, pl.ds(g * MOBA_FAR_GROUP + r, 1), :]

    def far_scores(g, buf):
        scores = score_dots(far_start(g), MOBA_FAR_GROUP)
        for hh in range(hg):
            far_bufs[buf][hh] = scores[hh]
            group_max = None
            for r in range(MOBA_FAR_GROUP):
                col_max = (jnp.max(scores[hh][r * bs:(r + 1) * bs], axis=0, keepdims=True)
                           + far_bias(hh, g, r))
                group_max = col_max if group_max is None else jnp.maximum(group_max, col_max)
            gmax_scr[buf, hh] = group_max

    def far_softmax(g, buf):
        probs, alphas = [], []
        for hh in range(hg):
            m_old = m_scr[hh]
            m_new = jnp.maximum(m_old, gmax_scr[buf, hh])
            alpha = jnp.exp2(m_old - m_new)
            p = jnp.concatenate(
                [jnp.exp2(far_bufs[buf][hh, r * bs:(r + 1) * bs, :] - (m_new - far_bias(hh, g, r)))
                 for r in range(MOBA_FAR_GROUP)], axis=0)
            m_scr[hh] = m_new
            l_scr[hh] = alpha * l_scr[hh] + jnp.sum(p, axis=0, keepdims=True)
            probs.append(p.astype(BF16))
            alphas.append(alpha)
        for hh in range(hg):
            acc_scr[hh] = (alphas[hh] * acc_scr[hh]
                           + _dot(vt_ref[heads[hh], pl.ds(far_start(g), far_rows)], probs[hh]))

    def far_step(g, buf, prefetch):
        if prefetch:
            far_scores(g + 1, 1 - buf)
        far_softmax(g, buf)

    @pl.when(qi > 0)
    def _():
        select_blocks()
        start = pl.multiple_of((qi - 1) * bs, bs)
        scores = score_dots(start, 2)
        far_scores(0, 0)
        softmax_update(scores, start, 2, lambda hh, r: tab_ref[hh, :, r * bs:(r + 1) * bs],
                       lambda hh, r: adj_scr[hh] if r == 0 else None, first=True)

    def far_pair(i, carry):
        far_step(2 * i, 0, True)
        far_step(2 * i + 1, 1, True)
        return carry

    lax.fori_loop(0, (n_far - 1) // 2, far_pair, 0)

    @pl.when((n_far > 0) & (n_far % 2 == 0))
    def _():
        far_step(n_far - 2, 0, True)
        far_step(n_far - 1, 1, False)

    @pl.when(n_far % 2 == 1)
    def _():
        far_step(n_far - 1, 0, False)

    for hh in range(hg):
        out_ref[:, heads[hh]] = (acc_scr[hh] / l_scr[hh]).T.astype(BF16)


def _moba(q, k, vt, kmean, tab, rel_bias, bsz, seq, hg=N_HEADS):
    n, d = q.shape
    nblk = seq // MOBA_BLOCK
    assert nblk % MOBA_FAR_GROUP == 0, "far-block groups must not run past the sequence"
    bs = MOBA_BLOCK
    w = hg * MOBA_HEAD_DIM
    return pl.pallas_call(
        functools.partial(_moba_kernel, nblk=nblk, hg=hg),
        grid=(bsz, N_HEADS // hg, nblk),
        in_specs=[pl.BlockSpec(memory_space=pltpu.SMEM),
                  pl.BlockSpec((bs, w), lambda b, g, i: (b * nblk + i, g)),
                  pl.BlockSpec((seq, w), lambda b, g, i: (b, g), pipeline_mode=pl.Buffered(1)),
                  pl.BlockSpec((w, seq), lambda b, g, i: (g, b), pipeline_mode=pl.Buffered(1)),
                  pl.BlockSpec((None, nblk, w), lambda b, g, i: (b, 0, g)),
                  pl.BlockSpec((hg, bs, 2 * bs), lambda b, g, i: (g, 0, 0),
                               pipeline_mode=pl.Buffered(1))],
        out_specs=pl.BlockSpec((bs, w), lambda b, g, i: (b * nblk + i, g)),
        out_shape=jax.ShapeDtypeStruct((n, d), BF16),
        scratch_shapes=[pltpu.VMEM((hg, nblk, bs), F32), pltpu.VMEM((hg, 1, bs), F32),
                        pltpu.VMEM((hg, 1, bs), F32), pltpu.VMEM((hg, 1, bs), F32),
                        pltpu.VMEM((hg, MOBA_HEAD_DIM, bs), F32),
                        pltpu.VMEM((hg, MOBA_FAR_GROUP * bs, bs), F32),
                        pltpu.VMEM((hg, MOBA_FAR_GROUP * bs, bs), F32),
                        pltpu.VMEM((2, hg, 1, bs), F32)],
        compiler_params=_params("arbitrary", "arbitrary", "arbitrary"),
        name="moba",
    )(rel_bias, q, k, vt, kmean.reshape(bsz, nblk, d), tab)


def _odd_router_kernel(h_ref, c_ref, wo_ref, g_ref, wr_ref, h_out_ref, xn_ref, gates_ref, sel_ref,
                       cnt_ref):
    wr = wr_ref[...]
    wr_hi = wr.astype(BF16)
    wr_lo = (wr - wr_hi.astype(F32)).astype(BF16)
    n_parts = 2
    part = h_ref.shape[0] // n_parts
    rows = [slice(k * part, (k + 1) * part) for k in range(n_parts)]
    h1 = [h_ref[r, :] + _dot(c_ref[r, :], wo_ref[...]) for r in rows]
    xn = [_rms(h, g_ref[...]) for h in h1]
    xn_hi = [x.astype(BF16) for x in xn]
    xn_lo = [(x - hi.astype(F32)).astype(BF16) for x, hi in zip(xn, xn_hi)]
    logits = [_dot(hi, wr_hi) + (_dot(lo, wr_hi) + _dot(hi, wr_lo)) for hi, lo in zip(xn_hi, xn_lo)]
    count = jnp.zeros((1, N_EXPERTS), F32)
    for k, r in enumerate(rows):
        h_out_ref[r, :] = h1[k]
        xn_ref[r, :] = xn_hi[k]
        lane = lax.broadcasted_iota(jnp.int32, logits[k].shape, 1)
        v1 = jnp.max(logits[k], axis=-1, keepdims=True)
        i1 = jnp.min(jnp.where(logits[k] == v1, lane, N_EXPERTS), axis=-1, keepdims=True)
        rest = jnp.where(lane == i1, NEG_INF, logits[k])
        v2 = jnp.max(rest, axis=-1, keepdims=True)
        i2 = jnp.min(jnp.where(rest == v2, lane, N_EXPERTS), axis=-1, keepdims=True)
        e2 = jnp.exp(v2 - v1)
        w1 = 1.0 / (1.0 + e2)
        w2 = e2 / (1.0 + e2)
        gates_ref[r, :] = jnp.where(lane == i1, w1, 0.0) + jnp.where(lane == i2, w2, 0.0)
        sel = jnp.where((lane == i1) | (lane == i2), 1.0, 0.0)
        sel_ref[r, :] = sel
        count = count + jnp.sum(sel, axis=0, keepdims=True)
    cnt_ref[0] = count


def _odd_router(h, c, wo, g, wr, tm=MOE_TOKEN_TILE):
    n, d = h.shape
    row = lambda w: pl.BlockSpec((tm, w), lambda i: (i, 0))
    return pl.pallas_call(
        _odd_router_kernel,
        grid=(n // tm,),
        in_specs=[row(d), row(d), _resident(wo.shape), _resident(g.shape), _resident(wr.shape)],
        out_specs=[row(d), row(d), row(N_EXPERTS), row(N_EXPERTS),
                   pl.BlockSpec((1, 1, N_EXPERTS), lambda i: (i, 0, 0))],
        out_shape=[jax.ShapeDtypeStruct((n, d), F32), jax.ShapeDtypeStruct((n, d), BF16),
                   jax.ShapeDtypeStruct((n, N_EXPERTS), F32),
                   jax.ShapeDtypeStruct((n, N_EXPERTS), F32),
                   jax.ShapeDtypeStruct((n // tm, 1, N_EXPERTS), F32)],
        compiler_params=_params("arbitrary"),
        name="odd_out_router",
    )(h, c, wo, g, wr)


def _moe_plan(counts, n_slot_tiles):
    cnt = counts.reshape(-1, N_EXPERTS).astype(jnp.int32)
    grp = (cnt + MOE_ALIGN - 1) // MOE_ALIGN * MOE_ALIGN
    local = jnp.cumsum(grp, axis=1) - grp
    tot = jnp.sum(grp, axis=0)
    region = (tot + MOE_SLOT_TILE - 1) // MOE_SLOT_TILE * MOE_SLOT_TILE
    region_end = jnp.cumsum(region)
    region_start = region_end - region
    start = region_start[None, :] + jnp.cumsum(grp, axis=0) - grp
    n_used = region_end[-1] // MOE_SLOT_TILE
    tile_row = jnp.arange(n_slot_tiles, dtype=jnp.int32) * MOE_SLOT_TILE
    tile_expert = jnp.sum((tile_row[:, None] >= region_end[None, :]).astype(jnp.int32), axis=1)
    tile_expert = jnp.minimum(tile_expert, N_EXPERTS - 1)
    tile_expert = jnp.where(tile_row < region_end[-1], tile_expert, tile_expert[n_used - 1])
    return dict(cnt=cnt, local=local.astype(jnp.int32), start=start.astype(jnp.int32),
                tail_start=(region_start + tot).astype(jnp.int32),
                tail_len=(region - tot).astype(jnp.int32), tile_expert=tile_expert,
                n_used=n_used.reshape(1).astype(jnp.int32))


def _group_copies(cnt_ref, local_ref, start_ref, t, local_buf, slot_array, sem, to_slots):
    copies = []
    for e in range(N_EXPERTS):
        rows = (cnt_ref[t, e] + MOE_ALIGN - 1) // MOE_ALIGN * MOE_ALIGN
        size = MOE_TOKEN_TILE
        while size >= MOE_ALIGN:
            off = rows & ~(2 * size - 1)
            local = local_buf.at[pl.ds(pl.multiple_of(local_ref[t, e] + off, MOE_ALIGN), size)]
            slots = slot_array.at[pl.ds(pl.multiple_of(start_ref[t, e] + off, MOE_ALIGN), size)]
            src, dst = (local, slots) if to_slots else (slots, local)
            copies.append(((rows & size) != 0, pltpu.make_async_copy(src, dst, sem)))
            size //= 2
    return copies


def _start_all(copies):
    for cond, cp in copies:
        pl.when(cond)(cp.start)


def _wait_all(copies):
    for cond, cp in copies:
        pl.when(cond)(cp.wait)


def _moe_dispatch_kernel(cnt_ref, local_ref, start_ref, tail_start_ref, tail_len_ref, x_ref, sel_ref,
                         upper_ref, xs_ref, stage_ref, zero_ref, sem, tail_sem):
    t = pl.program_id(0)
    last = pl.num_programs(0) - 1
    tile = x_ref.shape[0]
    slot = t % 2
    eye = jnp.where(lax.broadcasted_iota(jnp.int32, (N_EXPERTS, N_EXPERTS), 0)
                    == lax.broadcasted_iota(jnp.int32, (N_EXPERTS, N_EXPERTS), 1), 1.0, 0.0)
    sel_t = _dot_nt(eye.astype(BF16), sel_ref[...].astype(BF16))
    rank_t = _dot(sel_t.astype(BF16), upper_ref[...])
    expert = lax.broadcasted_iota(jnp.int32, (N_EXPERTS, 1), 0)
    base = jnp.zeros((N_EXPERTS, 1), F32)
    for e in range(N_EXPERTS):
        base = jnp.where(expert == e, local_ref[t, e].astype(F32), base)
    pos = base + rank_t
    pos_lo = jnp.min(jnp.where(sel_t > 0.5, pos, float(MOE_LOCAL_ROWS)), axis=0, keepdims=True)
    pos_hi = jnp.max(jnp.where(sel_t > 0.5, pos, -1.0), axis=0, keepdims=True)
    row = lax.broadcasted_iota(jnp.int32, (MOE_LOCAL_ROWS, tile), 0).astype(F32)
    onehot = jnp.where(row == pos_lo, 1.0, jnp.where(row == pos_hi, 1.0, 0.0)).astype(BF16)
    stage_ref[slot] = _dot(onehot, x_ref[...]).astype(BF16)

    def copies(step):
        return _group_copies(cnt_ref, local_ref, start_ref, step, stage_ref.at[step % 2], xs_ref,
                             sem.at[step % 2], to_slots=True)

    _start_all(copies(t))

    @pl.when(t > 0)
    def _():
        _wait_all(copies(t - 1))

    @pl.when(t == last)
    def _():
        _wait_all(copies(t))
        zero_ref[...] = jnp.zeros(zero_ref.shape, BF16)
        tails = []
        for e in range(N_EXPERTS):
            size = MOE_SLOT_TILE // 2
            while size >= MOE_ALIGN:
                off = tail_len_ref[e] & ~(2 * size - 1)
                dst = pl.multiple_of(tail_start_ref[e] + off, MOE_ALIGN)
                tails.append(((tail_len_ref[e] & size) != 0, pltpu.make_async_copy(
                    zero_ref.at[pl.ds(0, size)], xs_ref.at[pl.ds(dst, size)], tail_sem)))
                size //= 2
        _start_all(tails)
        _wait_all(tails)

        used_rows = tail_start_ref[N_EXPERTS - 1] + tail_len_ref[N_EXPERTS - 1]
        piece = zero_ref.shape[0]

        def unused_copy(k):
            dst = pl.multiple_of(used_rows + k * piece, piece)
            return pltpu.make_async_copy(zero_ref, xs_ref.at[pl.ds(dst, piece)], tail_sem)

        n_pieces = (xs_ref.shape[0] - used_rows) // piece
        lax.fori_loop(0, n_pieces, lambda k, c: (unused_copy(k).start(), c)[1], 0)
        lax.fori_loop(0, n_pieces, lambda k, c: (unused_copy(k).wait(), c)[1], 0)


def _moe_dispatch(xn, sel, plan, n_slot_tiles, tm=MOE_TOKEN_TILE):
    n, d = xn.shape
    upper = jnp.asarray(np.triu(np.ones((tm, tm), np.float32), 1), BF16)
    smem = pl.BlockSpec(memory_space=pltpu.SMEM)
    return pl.pallas_call(
        _moe_dispatch_kernel,
        grid=(n // tm,),
        in_specs=[smem, smem, smem, smem, smem,
                  pl.BlockSpec((tm, d), lambda i: (i, 0)),
                  pl.BlockSpec((tm, N_EXPERTS), lambda i: (i, 0)),
                  _resident((tm, tm))],
        out_specs=pl.BlockSpec(memory_space=pl.ANY),
        out_shape=jax.ShapeDtypeStruct((n_slot_tiles * MOE_SLOT_TILE, d), BF16),
        scratch_shapes=[pltpu.VMEM((2, MOE_LOCAL_ROWS, d), BF16),
                        pltpu.VMEM((MOE_SLOT_TILE // 2, d), BF16),
                        pltpu.SemaphoreType.DMA((2,)), pltpu.SemaphoreType.DMA(())],
        compiler_params=_params("arbitrary"),
        name="moe_dispatch",
    )(plan["cnt"], plan["local"], plan["start"], plan["tail_start"], plan["tail_len"], xn, sel, upper)


def _moe_experts_kernel(te_ref, nu_ref, x_ref, wg_ref, wu_ref, wd_ref, y_ref, acc_ref):
    i = pl.program_id(0)
    f = pl.program_id(1)
    last = pl.num_programs(1) - 1
    used = i < nu_ref[0]

    @pl.when(used & (f == 0))
    def _():
        acc_ref[...] = jnp.zeros(acc_ref.shape, F32)

    @pl.when(used)
    def _():
        x = x_ref[...]
        acc = acc_ref[...]
        for c in range(wg_ref.shape[1] // MOE_FF_CHUNK):
            cols = slice(c * MOE_FF_CHUNK, (c + 1) * MOE_FF_CHUNK)
            gate = _dot(x, wg_ref[:, cols])
            up = _dot(x, wu_ref[:, cols])
            hid = (gate * jax.nn.sigmoid(gate) * up).astype(BF16)
            acc = acc + _dot(hid, wd_ref[cols, :])
        acc_ref[...] = acc

    @pl.when(used & (f == last))
    def _():
        y_ref[...] = acc_ref[...].astype(BF16)

    @pl.when(jnp.logical_not(used) & (f == last))
    def _():
        y_ref[...] = jnp.zeros(y_ref.shape, BF16)


def _moe_experts(xs, plan, wg, wu, wd, tf=1792):
    rows, d = xs.shape
    d_ff = wg.shape[2]
    nf = d_ff // tf
    tm = MOE_SLOT_TILE
    f_idx = lambda i, f, nu: jnp.where(i < nu[0], f, nf - 1)
    return pl.pallas_call(
        _moe_experts_kernel,
        grid_spec=pltpu.PrefetchScalarGridSpec(
            num_scalar_prefetch=2,
            grid=(rows // tm, nf),
            in_specs=[pl.BlockSpec((tm, d), lambda i, f, te, nu: (jnp.minimum(i, nu[0] - 1), 0)),
                      pl.BlockSpec((None, d, tf), lambda i, f, te, nu: (te[i], 0, f_idx(i, f, nu))),
                      pl.BlockSpec((None, d, tf), lambda i, f, te, nu: (te[i], 0, f_idx(i, f, nu))),
                      pl.BlockSpec((None, tf, d), lambda i, f, te, nu: (te[i], f_idx(i, f, nu), 0))],
            out_specs=pl.BlockSpec((tm, d), lambda i, f, te, nu: (i, 0)),
            scratch_shapes=[pltpu.VMEM((tm, d), F32)]),
        out_shape=jax.ShapeDtypeStruct((rows, d), BF16),
        compiler_params=_params("arbitrary", "arbitrary"),
        name="moe_experts",
    )(plan["tile_expert"], plan["n_used"], xs, wg, wu, wd)


def _moe_combine_kernel(cnt_ref, local_ref, start_ref, h_ref, gates_ref, sel_ref, lower_ref, p_ref,
                        gp_ref, wpg_ref, wpp_ref, gfin_ref, ys_ref, out_ref, ybuf_ref, sem):
    t = pl.program_id(0)
    tile = h_ref.shape[0]

    def copies(step):
        return _group_copies(cnt_ref, local_ref, start_ref, step, ybuf_ref.at[step % 2], ys_ref,
                             sem.at[step % 2], to_slots=False)

    @pl.when(t == 0)
    def _():
        ybuf_ref[...] = jnp.zeros(ybuf_ref.shape, BF16)
        _start_all(copies(t))

    @pl.when(t + 1 < pl.num_programs(0))
    def _():
        _start_all(copies(t + 1))

    sel = sel_ref[...] > 0.5
    gates = gates_ref[...]
    rank = _dot(lower_ref[...], sel_ref[...].astype(BF16))
    expert = lax.broadcasted_iota(jnp.int32, (1, N_EXPERTS), 1)
    base = jnp.zeros((1, N_EXPERTS), F32)
    for e in range(N_EXPERTS):
        base = jnp.where(expert == e, local_ref[t, e].astype(F32), base)
    pos = base + rank
    pos_lo = jnp.min(jnp.where(sel, pos, float(MOE_LOCAL_ROWS)), axis=1, keepdims=True)
    pos_hi = jnp.max(jnp.where(sel, pos, -1.0), axis=1, keepdims=True)
    gate_lo = jnp.sum(jnp.where(sel & (pos == pos_lo), gates, 0.0), axis=1, keepdims=True)
    gate_hi = jnp.sum(jnp.where(sel & (pos == pos_hi), gates, 0.0), axis=1, keepdims=True)
    col = lax.broadcasted_iota(jnp.int32, (tile, MOE_LOCAL_ROWS), 1).astype(F32)
    onehot_lo = jnp.where(col == pos_lo, 1.0, 0.0).astype(BF16)
    onehot_hi = jnp.where(col == pos_hi, 1.0, 0.0).astype(BF16)

    _wait_all(copies(t))
    y = ybuf_ref[t % 2]
    h = h_ref[...] + gate_lo * _dot(onehot_lo, y) + gate_hi * _dot(onehot_hi, y)
    h = _ple(h, p_ref[...], gp_ref[...], wpg_ref[...], wpp_ref[...])
    out_ref[...] = _rms(h, gfin_ref[...])


def _moe_combine(h, gates, sel, ys, plan, p, layer, gp, wpg, wpp, gfin, tm=MOE_TOKEN_TILE):
    n, d = h.shape
    lower = jnp.asarray(np.tril(np.ones((tm, tm), np.float32), -1), BF16)
    smem = pl.BlockSpec(memory_space=pltpu.SMEM)
    row = lambda w: pl.BlockSpec((tm, w), lambda i: (i, 0))
    return pl.pallas_call(
        _moe_combine_kernel,
        grid=(n // tm,),
        in_specs=[smem, smem, smem, row(d), row(N_EXPERTS), row(N_EXPERTS), _resident((tm, tm)),
                  _layer_rows(p, layer, tm), _resident(gp.shape), _resident(wpg.shape),
                  _resident(wpp.shape), _resident(gfin.shape), pl.BlockSpec(memory_space=pl.ANY)],
        out_specs=row(d),
        out_shape=jax.ShapeDtypeStruct((n, d), F32),
        scratch_shapes=[pltpu.VMEM((2, MOE_LOCAL_ROWS, d), BF16), pltpu.SemaphoreType.DMA((2,))],
        compiler_params=_params("arbitrary"),
        name="moe_combine_ple_norm",
    )(plan["cnt"], plan["local"], plan["start"], h, gates, sel, lower, p, gp, wpg, wpp, gfin, ys)


def kernel(x, p, rel_bias, final_norm, e_norm_mix, e_w_in, e_conv_w, e_sinks, e_w_out, e_norm_ffn, e_ffn_gate, e_ffn_up, e_ffn_down, e_norm_ple, e_ple_gate, e_ple_proj, o_norm_mix, o_w_qkv, o_w_o, o_norm_ffn, o_router, o_exp_gate, o_exp_up, o_exp_down, o_norm_ple, o_ple_gate, o_ple_proj):
    bsz, seq, d = x.shape
    n = bsz * seq
    bf = lambda w: w.astype(BF16)
    h = x.reshape(n, d)
    p2 = p.reshape(p.shape[0], n, p.shape[-1])

    kk = np.arange(2 * SWA_BLOCK)[:, None]
    qq = np.arange(SWA_BLOCK)[None, :]
    dist = qq - kk + SWA_BLOCK
    swa_bucket = np.where((dist >= 0) & (dist < SWA_BLOCK), _t5_bucket_np(dist), -1)
    swa_tab = _bias_table(rel_bias, swa_bucket, mult=LOG2E)
    kk = np.arange(MOBA_BLOCK)[:, None]
    qq = np.arange(MOBA_BLOCK)[None, :]
    own_bucket = np.where(qq >= kk, _t5_bucket_np(qq - kk), -1)
    moba_bucket = np.concatenate([_t5_bucket_np(qq - kk + MOBA_BLOCK), own_bucket], axis=1)
    moba_tab = _bias_table(rel_bias, moba_bucket, mult=LOG2E)

    w_in = e_w_in[0]
    q0 = 3 * SC_WIDTH
    k0 = q0 + SWA_Q_WIDTH
    v0 = k0 + SWA_KV_WIDTH
    half = N_HEADS // SWA_KV_HEADS
    q_cols = np.concatenate([np.arange(h * SWA_HEAD_DIM, (h + 1) * SWA_HEAD_DIM)
                             for j in range(half) for h in (j, half + j)])
    w_main = jnp.concatenate([w_in[:, :q0], w_in[:, q0:k0][:, q_cols], w_in[:, k0:v0]], axis=1)
    a_out, q0_, k0_, vt0 = _even_proj(h, e_norm_mix[0:1], bf(w_main), bf(w_in[:, v0:].T),
                                      e_conv_w[0], seq)
    b_out = _swa(q0_, k0_, vt0, e_sinks[0], swa_tab, bsz, seq)
    h = _even_ffn(h, a_out, b_out, p2, 0, bf(e_w_out[0]), e_norm_ffn[0:1],
                  bf(e_ffn_gate[0]), bf(e_ffn_up[0]), bf(e_ffn_down[0]),
                  e_norm_ple[0:1], bf(e_ple_gate[0]), bf(e_ple_proj[0]))

    w_qkv = o_w_qkv[0]
    q1, k1, vt1, kmean = _odd_proj(h, o_norm_mix[0:1], bf(w_qkv[:, :d]), bf(w_qkv[:, d:2 * d]),
                                   bf(w_qkv[:, 2 * d:].T))
    c_out = _moba(q1, k1, vt1, kmean, moba_tab, rel_bias, bsz, seq)
    h, xn, gates, sel, counts = _odd_router(h, c_out, bf(o_w_o[0]), o_norm_ffn[0:1], o_router[0])
    n_groups = (n // MOE_TOKEN_TILE) * N_EXPERTS
    max_rows = 2 * n + n_groups * (MOE_ALIGN - 1) + N_EXPERTS * (MOE_SLOT_TILE - 1)
    n_slot_tiles = -(-max_rows // MOE_SLOT_TILE) + 1
    plan = _moe_plan(counts, n_slot_tiles)
    xs = _moe_dispatch(xn, sel, plan, n_slot_tiles)
    ys = _moe_experts(xs, plan, bf(o_exp_gate[0]), bf(o_exp_up[0]), bf(o_exp_down[0]))
    out = _moe_combine(h, gates, sel, ys, plan, p2, 1, o_norm_ple[0:1], bf(o_ple_gate[0]),
                       bf(o_ple_proj[0]), final_norm.reshape(1, d))
    return out.reshape(bsz, seq, d)
```

```python
import functools
import math

import jax
import jax.numpy as jnp
import numpy as np
from jax import lax
from jax.experimental import pallas as pl
from jax.experimental.pallas import tpu as pltpu

F32 = jnp.float32
BF16 = jnp.bfloat16

EPS = 1e-6
N_HEADS = 8
N_BUCKETS = 32
MAX_DISTANCE = 128

SC_WIDTH = 512
CONV_WIDTH = 3
SWA_KV_HEADS = 2
SWA_HEAD_DIM = 64
SWA_BLOCK = 128
SWA_Q_WIDTH = N_HEADS * SWA_HEAD_DIM
SWA_KV_WIDTH = SWA_KV_HEADS * SWA_HEAD_DIM

MOBA_HEAD_DIM = 128
MOBA_BLOCK = 256
MOBA_TOPK = 3
MOBA_FAR_GROUP = 2
MOBA_V_ROWS = MOBA_HEAD_DIM + 16

N_EXPERTS = 8
MOE_TOKEN_TILE = 512
MOE_SLOT_TILE = 512
MOE_ALIGN = 16
MOE_FF_CHUNK = 256
MOE_LOCAL_ROWS = -(-(2 * MOE_TOKEN_TILE + N_EXPERTS * (MOE_ALIGN - 1)) // 128) * 128

VMEM_LIMIT_BYTES = 56 * 1024 * 1024
NEG_INF = float("-inf")
LOG2E = math.log2(math.e)


def _params(*semantics):
    return pltpu.CompilerParams(dimension_semantics=semantics,
                                vmem_limit_bytes=VMEM_LIMIT_BYTES)


def _resident(shape):
    zeros = (0,) * len(shape)
    return pl.BlockSpec(shape, lambda *_: zeros, pipeline_mode=pl.Buffered(1))


def _rms(x, g):
    return x * lax.rsqrt(jnp.mean(x * x, axis=-1, keepdims=True) + EPS) * g


def _dot(a, b):
    return jnp.dot(a, b, preferred_element_type=F32)


def _dot_nt(a, b):
    return lax.dot_general(a, b, (((1,), (1,)), ((), ())), preferred_element_type=F32)


def _t5_bucket_np(dist):
    d = np.maximum(dist, 0).astype(np.int32)
    max_exact = N_BUCKETS // 2
    scaled = (np.log(np.maximum(d, max_exact).astype(np.float32) / np.float32(max_exact))
              / np.float32(math.log(MAX_DISTANCE / max_exact)))
    large = np.minimum(max_exact + (scaled * (N_BUCKETS - max_exact)).astype(np.int32),
                       N_BUCKETS - 1)
    return np.where(d < max_exact, d, large).astype(np.int32)


def _bias_table_kernel(rb_ref, bkt_ref, out_ref, *, mult):
    h = pl.program_id(0)
    bkt = bkt_ref[...]
    acc = jnp.full(bkt.shape, NEG_INF, F32)
    for b in range(N_BUCKETS):
        acc = jnp.where(bkt == b, rb_ref[b, h], acc)
    out_ref[...] = acc * mult


def _bias_table(rel_bias, bucket, mult=1.0):
    rows, cols = bucket.shape
    return pl.pallas_call(
        functools.partial(_bias_table_kernel, mult=mult),
        grid=(N_HEADS,),
        in_specs=[pl.BlockSpec(memory_space=pltpu.SMEM),
                  pl.BlockSpec((rows, cols), lambda h: (0, 0))],
        out_specs=pl.BlockSpec((None, rows, cols), lambda h: (h, 0, 0)),
        out_shape=jax.ShapeDtypeStruct((N_HEADS, rows, cols), F32),
        compiler_params=_params("arbitrary"),
        name="bias_table",
    )(rel_bias, jnp.asarray(bucket))


def _even_proj_kernel(x_ref, g_ref, w_ref, wvt_ref, cw_ref, a_ref, q_ref, k_ref, vt_ref, cu_scr, *,
                      tm, tiles_per_seq):
    i = pl.program_id(0)
    cu_scr[0:8, :] = jnp.where(i % tiles_per_seq == 0, 0.0, cu_scr[tm:tm + 8, :])
    xn = _rms(x_ref[...], g_ref[...]).astype(BF16)
    q0 = 3 * SC_WIDTH
    k0 = q0 + SWA_Q_WIDTH
    c_gate = _dot(xn, w_ref[:, SC_WIDTH:2 * SC_WIDTH])
    u = _dot(xn, w_ref[:, 2 * SC_WIDTH:3 * SC_WIDTH])
    q = _dot(xn, w_ref[:, q0:k0])
    k = _dot(xn, w_ref[:, k0:])
    vt = _dot_nt(wvt_ref[...], xn)
    b_gate = _dot(xn, w_ref[:, 0:SC_WIDTH])
    cu = c_gate * u
    cu_scr[8:8 + tm, :] = cu
    y = (cw_ref[0:1, :] * cu_scr[6:6 + tm, :] + cw_ref[1:2, :] * cu_scr[7:7 + tm, :]
         + cw_ref[2:3, :] * cu)
    q_ref[...] = (q * (SWA_HEAD_DIM ** -0.5 * LOG2E)).astype(BF16)
    k_ref[...] = k.astype(BF16)
    vt_ref[...] = vt.astype(BF16)
    a_ref[...] = (b_gate * y).astype(BF16)


def _even_proj(x, g, w, wvt, conv_w, seq, tm=512):
    n, d = x.shape
    row = lambda width: pl.BlockSpec((tm, width), lambda i: (i, 0))
    return pl.pallas_call(
        functools.partial(_even_proj_kernel, tm=tm, tiles_per_seq=seq // tm),
        grid=(n // tm,),
        in_specs=[row(d), _resident((1, d)), _resident(w.shape), _resident(wvt.shape),
                  _resident((CONV_WIDTH, SC_WIDTH))],
        out_specs=[row(SC_WIDTH), row(SWA_Q_WIDTH), row(SWA_KV_WIDTH),
                   pl.BlockSpec((SWA_KV_WIDTH, tm), lambda i: (0, i))],
        out_shape=[jax.ShapeDtypeStruct((n, SC_WIDTH), BF16),
                   jax.ShapeDtypeStruct((n, SWA_Q_WIDTH), BF16),
                   jax.ShapeDtypeStruct((n, SWA_KV_WIDTH), BF16),
                   jax.ShapeDtypeStruct((SWA_KV_WIDTH, n), BF16)],
        scratch_shapes=[pltpu.VMEM((tm + 8, SC_WIDTH), F32)],
        compiler_params=_params("arbitrary"),
        name="even_proj_conv",
    )(x, g, w, wvt, conv_w)


def _swa_kernel(sink_ref, q_ref, k_ref, kp_ref, vt_ref, vtp_ref, tab_ref, out_ref, *, nq):
    i = pl.program_id(1)
    blk = SWA_BLOCK
    hd = SWA_HEAD_DIM
    half = N_HEADS // SWA_KV_HEADS
    lane = lax.broadcasted_iota(jnp.int32, (blk, 2 * hd), 1)
    k_all = jnp.concatenate([kp_ref[...], k_ref[...]], axis=0)
    vt_all = jnp.concatenate([vtp_ref[...], vt_ref[...]], axis=1)

    scores = {}
    for s in range(nq):
        keys = k_all[s * blk:(s + 2) * blk]
        for j in range(half):
            q_pair = q_ref[s * blk:(s + 1) * blk, j * 2 * hd:(j + 1) * 2 * hd]
            scores[s, j] = _dot_nt(keys, jnp.where(lane < hd, q_pair, 0))
            scores[s, half + j] = _dot_nt(keys, jnp.where(lane >= hd, q_pair, 0))

    probs, denoms = {}, {}
    for s in range(nq):
        for h in range(N_HEADS):
            tab = tab_ref[h]
            if s == 0:
                tab = jnp.concatenate([jnp.where(i > 0, tab[:blk], NEG_INF), tab[blk:]], axis=0)
            sc = scores[s, h] + tab
            sink = sink_ref[h] * LOG2E
            m = jnp.maximum(jnp.max(sc, axis=0, keepdims=True), sink)
            p = jnp.exp2(sc - m)
            denoms[s, h] = jnp.sum(p, axis=0, keepdims=True) + jnp.exp2(sink - m)
            probs[s, h] = p.astype(BF16)

    for s in range(nq):
        outs = []
        for h in range(N_HEADS):
            g = h // half
            vt = vt_all[g * hd:(g + 1) * hd, s * blk:(s + 2) * blk]
            outs.append(_dot(vt, probs[s, h]) / denoms[s, h])
        out_ref[s * blk:(s + 1) * blk, :] = jnp.concatenate(outs, axis=0).T.astype(BF16)


def _swa(q, k, vt, sinks, tab, bsz, seq, nq=4):
    n = q.shape[0]
    nb = seq // SWA_BLOCK
    steps = nb // nq
    prev = lambda b, i: b * nb + jnp.maximum(i * nq - 1, 0)
    return pl.pallas_call(
        functools.partial(_swa_kernel, nq=nq),
        grid=(bsz, steps),
        in_specs=[pl.BlockSpec(memory_space=pltpu.SMEM),
                  pl.BlockSpec((nq * SWA_BLOCK, SWA_Q_WIDTH), lambda b, i: (b * steps + i, 0)),
                  pl.BlockSpec((nq * SWA_BLOCK, SWA_KV_WIDTH), lambda b, i: (b * steps + i, 0)),
                  pl.BlockSpec((SWA_BLOCK, SWA_KV_WIDTH), lambda b, i: (prev(b, i), 0)),
                  pl.BlockSpec((SWA_KV_WIDTH, nq * SWA_BLOCK), lambda b, i: (0, b * steps + i)),
                  pl.BlockSpec((SWA_KV_WIDTH, SWA_BLOCK), lambda b, i: (0, prev(b, i))),
                  _resident(tab.shape)],
        out_specs=pl.BlockSpec((nq * SWA_BLOCK, SWA_Q_WIDTH), lambda b, i: (b * steps + i, 0)),
        out_shape=jax.ShapeDtypeStruct((n, SWA_Q_WIDTH), BF16),
        compiler_params=_params("arbitrary", "arbitrary"),
        name="swa",
    )(sinks, q, k, k, vt, vt, tab)


def _ple(h, p, g, w_gate, w_proj):
    gate = jax.nn.sigmoid(_dot(_rms(h, g).astype(BF16), w_gate))
    return h + gate * _dot(p.astype(BF16), w_proj)


def _even_ffn_kernel(h_ref, a_ref, b_ref, p_ref, wo_ref, gf_ref, wg_ref, wu_ref, wd_ref,
                     gp_ref, wpg_ref, wpp_ref, out_ref, *, tf):
    h1 = (h_ref[...] + _dot(a_ref[...], wo_ref[0:SC_WIDTH, :])
          + _dot(b_ref[...], wo_ref[SC_WIDTH:, :]))
    xn = _rms(h1, gf_ref[...]).astype(BF16)
    d_ff = wg_ref.shape[1]
    acc = jnp.zeros(h1.shape, F32)
    for c in range(d_ff // tf):
        gate = _dot(xn, wg_ref[:, c * tf:(c + 1) * tf])
        up = _dot(xn, wu_ref[:, c * tf:(c + 1) * tf])
        hid = (gate * jax.nn.sigmoid(gate) * up).astype(BF16)
        acc = acc + _dot(hid, wd_ref[c * tf:(c + 1) * tf, :])
    h2 = h1 + acc
    out_ref[...] = _ple(h2, p_ref[...], gp_ref[...], wpg_ref[...], wpp_ref[...])


def _layer_rows(p, layer, tm):
    return pl.BlockSpec((None, tm, p.shape[2]), lambda i: (layer, i, 0))


def _even_ffn(h, a, b, p, layer, wo, gf, wg, wu, wd, gp, wpg, wpp, tm=512, tf=256):
    n, d = h.shape
    row = lambda w: pl.BlockSpec((tm, w), lambda i: (i, 0))
    return pl.pallas_call(
        functools.partial(_even_ffn_kernel, tf=tf),
        grid=(n // tm,),
        in_specs=[row(d), row(a.shape[1]), row(b.shape[1]), _layer_rows(p, layer, tm),
                  _resident(wo.shape), _resident(gf.shape), _resident(wg.shape),
                  _resident(wu.shape), _resident(wd.shape), _resident(gp.shape),
                  _resident(wpg.shape), _resident(wpp.shape)],
        out_specs=row(d),
        out_shape=jax.ShapeDtypeStruct((n, d), F32),
        compiler_params=_params("arbitrary"),
        name="even_out_ffn_ple",
    )(h, a, b, p, wo, gf, wg, wu, wd, gp, wpg, wpp)


def _odd_proj_kernel(h_ref, g_ref, wq_ref, wk_ref, wvt_ref, q_ref, k_ref, vt_ref, kmean_ref, *, tm):
    xn = _rms(h_ref[...], g_ref[...]).astype(BF16)
    q_ref[...] = (_dot(xn, wq_ref[...]) * (MOBA_HEAD_DIM ** -0.5 * LOG2E)).astype(BF16)
    k = _dot(xn, wk_ref[...])
    k_ref[...] = k.astype(BF16)
    vt = _dot_nt(wvt_ref[...], xn).astype(BF16)
    ones = jnp.ones((MOBA_V_ROWS - MOBA_HEAD_DIM, tm), BF16)
    for hh in range(N_HEADS):
        vt_ref[hh * MOBA_V_ROWS:hh * MOBA_V_ROWS + MOBA_HEAD_DIM, :] = (
            vt[hh * MOBA_HEAD_DIM:(hh + 1) * MOBA_HEAD_DIM])
        vt_ref[hh * MOBA_V_ROWS + MOBA_HEAD_DIM:(hh + 1) * MOBA_V_ROWS, :] = ones
    for r in range(tm // MOBA_BLOCK):
        kmean_ref[r] = jnp.mean(k[r * MOBA_BLOCK:(r + 1) * MOBA_BLOCK, :], axis=0, keepdims=True)


def _odd_proj(h, g, wq, wk, wvt, tm=512):
    n, d = h.shape
    bpt = tm // MOBA_BLOCK
    row = pl.BlockSpec((tm, d), lambda i: (i, 0))
    return pl.pallas_call(
        functools.partial(_odd_proj_kernel, tm=tm),
        grid=(n // tm,),
        in_specs=[row, _resident(g.shape), _resident(wq.shape), _resident(wk.shape),
                  _resident(wvt.shape)],
        out_specs=[row, row, pl.BlockSpec((N_HEADS * MOBA_V_ROWS, tm), lambda i: (0, i)),
                   pl.BlockSpec((bpt, 1, d), lambda i: (i, 0, 0))],
        out_shape=[jax.ShapeDtypeStruct((n, d), BF16), jax.ShapeDtypeStruct((n, d), BF16),
                   jax.ShapeDtypeStruct((N_HEADS * MOBA_V_ROWS, n), BF16),
                   jax.ShapeDtypeStruct((n // MOBA_BLOCK, 1, d), F32)],
        compiler_params=_params("arbitrary"),
        name="odd_qkv",
    )(h, g, wq, wk, wvt)


def _moba_kernel(rb_ref, q_ref, k_ref, vt_ref, km_ref, tab_ref, out_ref,
                 sel_scr, adj_scr, m_scr, acc_scr, far_a_scr, far_b_scr, gmax_scr, *, nblk, hg):
    head0 = pl.program_id(1) * hg
    qi = pl.program_id(2)
    bs = MOBA_BLOCK
    hd = MOBA_HEAD_DIM
    heads = [slice(hh * hd, (hh + 1) * hd) for hh in range(hg)]
    head_values = [slice(hh * MOBA_V_ROWS, (hh + 1) * MOBA_V_ROWS) for hh in range(hg)]

    def select_blocks():
        blk = lax.broadcasted_iota(jnp.int32, (nblk, bs), 0)
        past = blk < qi
        for hh in range(hg):
            gate = lax.dot_general(km_ref[:, heads[hh]], q_ref[:, heads[hh]].astype(F32),
                                   (((1,), (1,)), ((), ())), preferred_element_type=F32,
                                   precision=lax.Precision.HIGHEST)
            g = jnp.where(past, gate, NEG_INF)
            sel = jnp.zeros(gate.shape, jnp.bool_)
            for _ in range(MOBA_TOPK):
                top = jnp.max(g, axis=0, keepdims=True)
                idx = jnp.min(jnp.where(g == top, blk, nblk), axis=0, keepdims=True)
                hit = blk == idx
                sel = sel | hit
                g = jnp.where(hit, NEG_INF, g)
            far_bias = rb_ref[N_BUCKETS - 1, head0 + hh] * LOG2E
            sel_scr[hh] = jnp.where(sel & (blk < qi - 1), far_bias, NEG_INF)
            adj_scr[hh] = jnp.max(jnp.where(sel & (blk == qi - 1), 0.0, NEG_INF), axis=0,
                                  keepdims=True)

    def score_dots(start, nb):
        return [_dot_nt(k_ref[pl.ds(start, nb * bs), heads[hh]], q_ref[:, heads[hh]])
                for hh in range(hg)]

    def softmax_update(scores, start, nb, tables, query_biases, first):
        probs, alphas = [], []
        for hh in range(hg):
            parts, s_max = [], None
            for r in range(nb):
                s_r = scores[hh][r * bs:(r + 1) * bs]
                if tables(hh, r) is not None:
                    s_r = s_r + tables(hh, r)
                col_max = jnp.max(s_r, axis=0, keepdims=True)
                if query_biases(hh, r) is not None:
                    col_max = col_max + query_biases(hh, r)
                parts.append(s_r)
                s_max = col_max if s_max is None else jnp.maximum(s_max, col_max)
            m_new = s_max if first else jnp.maximum(m_scr[hh], s_max)
            p = jnp.concatenate(
                [jnp.exp2(parts[r] - (m_new if query_biases(hh, r) is None
                                      else m_new - query_biases(hh, r))) for r in range(nb)], axis=0)
            if not first:
                alphas.append(jnp.exp2(m_scr[hh] - m_new))
            m_scr[hh] = m_new
            probs.append(p.astype(BF16))
        for hh in range(hg):
            acc_new = _dot(vt_ref[head_values[hh], pl.ds(start, nb * bs)], probs[hh])
            if not first:
                acc_new = alphas[hh] * acc_scr[hh] + acc_new
            acc_scr[hh] = acc_new

    @pl.when(qi == 0)
    def _():
        softmax_update(score_dots(0, 1), 0, 1, lambda hh, r: tab_ref[hh, :, bs:2 * bs],
                       lambda hh, r: None, first=True)

    far_rows = MOBA_FAR_GROUP * bs
    n_far = (qi - 1 + MOBA_FAR_GROUP - 1) // MOBA_FAR_GROUP
    far_bufs = (far_a_scr, far_b_scr)

    def far_start(g):
        return pl.multiple_of(g * far_rows, far_rows)

    def far_bias(hh, g, r):
        return sel_scr[hh, pl.ds(g * MOBA_FAR_GROUP + r, 1), :]

    def far_scores(g, buf):
        scores = score_dots(far_start(g), MOBA_FAR_GROUP)
        for hh in range(hg):
            far_bufs[buf][hh] = scores[hh]
            group_max = None
            for r in range(MOBA_FAR_GROUP):
                col_max = (jnp.max(scores[hh][r * bs:(r + 1) * bs], axis=0, keepdims=True)
                           + far_bias(hh, g, r))
                group_max = col_max if group_max is None else jnp.maximum(group_max, col_max)
            gmax_scr[buf, hh] = group_max

    def far_softmax(g, buf):
        probs, alphas = [], []
        for hh in range(hg):
            m_old = m_scr[hh]
            m_new = jnp.maximum(m_old, gmax_scr[buf, hh])
            alpha = jnp.exp2(m_old - m_new)
            p = jnp.concatenate(
                [jnp.exp2(far_bufs[buf][hh, r * bs:(r + 1) * bs, :] - (m_new - far_bias(hh, g, r)))
                 for r in range(MOBA_FAR_GROUP)], axis=0)
            m_scr[hh] = m_new
            probs.append(p.astype(BF16))
            alphas.append(alpha)
        for hh in range(hg):
            acc_scr[hh] = (alphas[hh] * acc_scr[hh]
                           + _dot(vt_ref[head_values[hh], pl.ds(far_start(g), far_rows)], probs[hh]))

    def far_step(g, buf, prefetch):
        if prefetch:
            far_scores(g + 1, 1 - buf)
        far_softmax(g, buf)

    @pl.when(qi > 0)
    def _():
        select_blocks()
        start = pl.multiple_of((qi - 1) * bs, bs)
        scores = score_dots(start, 2)
        far_scores(0, 0)
        softmax_update(scores, start, 2, lambda hh, r: tab_ref[hh, :, r * bs:(r + 1) * bs],
                       lambda hh, r: adj_scr[hh] if r == 0 else None, first=True)

    def far_pair(i, carry):
        far_step(2 * i, 0, True)
        far_step(2 * i + 1, 1, True)
        return carry

    lax.fori_loop(0, (n_far - 1) // 2, far_pair, 0)

    @pl.when((n_far > 0) & (n_far % 2 == 0))
    def _():
        far_step(n_far - 2, 0, True)
        far_step(n_far - 1, 1, False)

    @pl.when(n_far % 2 == 1)
    def _():
        far_step(n_far - 1, 0, False)

    for hh in range(hg):
        out_ref[:, heads[hh]] = (acc_scr[hh, 0:hd, :] / acc_scr[hh, hd:hd + 1, :]).T.astype(BF16)


def _moba(q, k, vt, kmean, tab, rel_bias, bsz, seq, hg=N_HEADS):
    n, d = q.shape
    nblk = seq // MOBA_BLOCK
    assert nblk % MOBA_FAR_GROUP == 0, "far-block groups must not run past the sequence"
    bs = MOBA_BLOCK
    w = hg * MOBA_HEAD_DIM
    return pl.pallas_call(
        functools.partial(_moba_kernel, nblk=nblk, hg=hg),
        grid=(bsz, N_HEADS // hg, nblk),
        in_specs=[pl.BlockSpec(memory_space=pltpu.SMEM),
                  pl.BlockSpec((bs, w), lambda b, g, i: (b * nblk + i, g)),
                  pl.BlockSpec((seq, w), lambda b, g, i: (b, g), pipeline_mode=pl.Buffered(1)),
                  pl.BlockSpec((hg * MOBA_V_ROWS, seq), lambda b, g, i: (g, b),
                               pipeline_mode=pl.Buffered(1)),
                  pl.BlockSpec((None, nblk, w), lambda b, g, i: (b, 0, g)),
                  pl.BlockSpec((hg, bs, 2 * bs), lambda b, g, i: (g, 0, 0),
                               pipeline_mode=pl.Buffered(1))],
        out_specs=pl.BlockSpec((bs, w), lambda b, g, i: (b * nblk + i, g)),
        out_shape=jax.ShapeDtypeStruct((n, d), BF16),
        scratch_shapes=[pltpu.VMEM((hg, nblk, bs), F32), pltpu.VMEM((hg, 1, bs), F32),
                        pltpu.VMEM((hg, 1, bs), F32), pltpu.VMEM((hg, MOBA_V_ROWS, bs), F32),
                        pltpu.VMEM((hg, MOBA_FAR_GROUP * bs, bs), F32),
                        pltpu.VMEM((hg, MOBA_FAR_GROUP * bs, bs), F32),
                        pltpu.VMEM((2, hg, 1, bs), F32)],
        compiler_params=_params("arbitrary", "arbitrary", "arbitrary"),
        name="moba",
    )(rel_bias, q, k, vt, kmean.reshape(bsz, nblk, d), tab)


def _odd_router_kernel(h_ref, c_ref, wo_ref, g_ref, wr_ref, h_out_ref, xn_ref, gates_ref, sel_ref,
                       cnt_ref):
    wr = wr_ref[...]
    wr_hi = wr.astype(BF16)
    wr_lo = (wr - wr_hi.astype(F32)).astype(BF16)
    n_parts = 2
    part = h_ref.shape[0] // n_parts
    rows = [slice(k * part, (k + 1) * part) for k in range(n_parts)]
    h1 = [h_ref[r, :] + _dot(c_ref[r, :], wo_ref[...]) for r in rows]
    xn = [_rms(h, g_ref[...]) for h in h1]
    xn_hi = [x.astype(BF16) for x in xn]
    xn_lo = [(x - hi.astype(F32)).astype(BF16) for x, hi in zip(xn, xn_hi)]
    logits = [_dot(hi, wr_hi) + (_dot(lo, wr_hi) + _dot(hi, wr_lo)) for hi, lo in zip(xn_hi, xn_lo)]
    count = jnp.zeros((1, N_EXPERTS), F32)
    for k, r in enumerate(rows):
        h_out_ref[r, :] = h1[k]
        xn_ref[r, :] = xn_hi[k]
        lane = lax.broadcasted_iota(jnp.int32, logits[k].shape, 1)
        v1 = jnp.max(logits[k], axis=-1, keepdims=True)
        i1 = jnp.min(jnp.where(logits[k] == v1, lane, N_EXPERTS), axis=-1, keepdims=True)
        rest = jnp.where(lane == i1, NEG_INF, logits[k])
        v2 = jnp.max(rest, axis=-1, keepdims=True)
        i2 = jnp.min(jnp.where(rest == v2, lane, N_EXPERTS), axis=-1, keepdims=True)
        e2 = jnp.exp(v2 - v1)
        w1 = 1.0 / (1.0 + e2)
        w2 = e2 / (1.0 + e2)
        gates_ref[r, :] = jnp.where(lane == i1, w1, 0.0) + jnp.where(lane == i2, w2, 0.0)
        sel = jnp.where((lane == i1) | (lane == i2), 1.0, 0.0)
        sel_ref[r, :] = sel
        count = count + jnp.sum(sel, axis=0, keepdims=True)
    cnt_ref[0] = count


def _odd_router(h, c, wo, g, wr, tm=MOE_TOKEN_TILE):
    n, d = h.shape
    row = lambda w: pl.BlockSpec((tm, w), lambda i: (i, 0))
    return pl.pallas_call(
        _odd_router_kernel,
        grid=(n // tm,),
        in_specs=[row(d), row(d), _resident(wo.shape), _resident(g.shape), _resident(wr.shape)],
        out_specs=[row(d), row(d), row(N_EXPERTS), row(N_EXPERTS),
                   pl.BlockSpec((1, 1, N_EXPERTS), lambda i: (i, 0, 0))],
        out_shape=[jax.ShapeDtypeStruct((n, d), F32), jax.ShapeDtypeStruct((n, d), BF16),
                   jax.ShapeDtypeStruct((n, N_EXPERTS), F32),
                   jax.ShapeDtypeStruct((n, N_EXPERTS), F32),
                   jax.ShapeDtypeStruct((n // tm, 1, N_EXPERTS), F32)],
        compiler_params=_params("arbitrary"),
        name="odd_out_router",
    )(h, c, wo, g, wr)


def _moe_plan(counts, n_slot_tiles):
    cnt = counts.reshape(-1, N_EXPERTS).astype(jnp.int32)
    grp = (cnt + MOE_ALIGN - 1) // MOE_ALIGN * MOE_ALIGN
    local = jnp.cumsum(grp, axis=1) - grp
    tot = jnp.sum(grp, axis=0)
    region = (tot + MOE_SLOT_TILE - 1) // MOE_SLOT_TILE * MOE_SLOT_TILE
    region_end = jnp.cumsum(region)
    region_start = region_end - region
    start = region_start[None, :] + jnp.cumsum(grp, axis=0) - grp
    n_used = region_end[-1] // MOE_SLOT_TILE
    tile_row = jnp.arange(n_slot_tiles, dtype=jnp.int32) * MOE_SLOT_TILE
    tile_expert = jnp.sum((tile_row[:, None] >= region_end[None, :]).astype(jnp.int32), axis=1)
    tile_expert = jnp.minimum(tile_expert, N_EXPERTS - 1)
    tile_expert = jnp.where(tile_row < region_end[-1], tile_expert, tile_expert[n_used - 1])
    return dict(cnt=cnt, local=local.astype(jnp.int32), start=start.astype(jnp.int32),
                tail_start=(region_start + tot).astype(jnp.int32),
                tail_len=(region - tot).astype(jnp.int32), tile_expert=tile_expert,
                n_used=n_used.reshape(1).astype(jnp.int32))


def _group_copies(cnt_ref, local_ref, start_ref, t, local_buf, slot_array, sem, to_slots):
    copies = []
    for e in range(N_EXPERTS):
        rows = (cnt_ref[t, e] + MOE_ALIGN - 1) // MOE_ALIGN * MOE_ALIGN
        size = MOE_TOKEN_TILE
        while size >= MOE_ALIGN:
            off = rows & ~(2 * size - 1)
            local = local_buf.at[pl.ds(pl.multiple_of(local_ref[t, e] + off, MOE_ALIGN), size)]
            slots = slot_array.at[pl.ds(pl.multiple_of(start_ref[t, e] + off, MOE_ALIGN), size)]
            src, dst = (local, slots) if to_slots else (slots, local)
            copies.append(((rows & size) != 0, pltpu.make_async_copy(src, dst, sem)))
            size //= 2
    return copies


def _start_all(copies):
    for cond, cp in copies:
        pl.when(cond)(cp.start)


def _wait_all(copies):
    for cond, cp in copies:
        pl.when(cond)(cp.wait)


def _moe_dispatch_kernel(cnt_ref, local_ref, start_ref, tail_start_ref, tail_len_ref, x_ref, sel_ref,
                         upper_ref, xs_ref, stage_ref, zero_ref, sem, tail_sem):
    t = pl.program_id(0)
    last = pl.num_programs(0) - 1
    tile = x_ref.shape[0]
    slot = t % 2
    eye = jnp.where(lax.broadcasted_iota(jnp.int32, (N_EXPERTS, N_EXPERTS), 0)
                    == lax.broadcasted_iota(jnp.int32, (N_EXPERTS, N_EXPERTS), 1), 1.0, 0.0)
    sel_t = _dot_nt(eye.astype(BF16), sel_ref[...].astype(BF16))
    rank_t = _dot(sel_t.astype(BF16), upper_ref[...])
    expert = lax.broadcasted_iota(jnp.int32, (N_EXPERTS, 1), 0)
    base = jnp.zeros((N_EXPERTS, 1), F32)
    for e in range(N_EXPERTS):
        base = jnp.where(expert == e, local_ref[t, e].astype(F32), base)
    pos = base + rank_t
    pos_lo = jnp.min(jnp.where(sel_t > 0.5, pos, float(MOE_LOCAL_ROWS)), axis=0, keepdims=True)
    pos_hi = jnp.max(jnp.where(sel_t > 0.5, pos, -1.0), axis=0, keepdims=True)
    row = lax.broadcasted_iota(jnp.int32, (MOE_LOCAL_ROWS, tile), 0).astype(F32)
    onehot = jnp.where(row == pos_lo, 1.0, jnp.where(row == pos_hi, 1.0, 0.0)).astype(BF16)
    stage_ref[slot] = _dot(onehot, x_ref[...]).astype(BF16)

    def copies(step):
        return _group_copies(cnt_ref, local_ref, start_ref, step, stage_ref.at[step % 2], xs_ref,
                             sem.at[step % 2], to_slots=True)

    _start_all(copies(t))

    @pl.when(t > 0)
    def _():
        _wait_all(copies(t - 1))

    @pl.when(t == last)
    def _():
        _wait_all(copies(t))
        zero_ref[...] = jnp.zeros(zero_ref.shape, BF16)
        tails = []
        for e in range(N_EXPERTS):
            size = MOE_SLOT_TILE // 2
            while size >= MOE_ALIGN:
                off = tail_len_ref[e] & ~(2 * size - 1)
                dst = pl.multiple_of(tail_start_ref[e] + off, MOE_ALIGN)
                tails.append(((tail_len_ref[e] & size) != 0, pltpu.make_async_copy(
                    zero_ref.at[pl.ds(0, size)], xs_ref.at[pl.ds(dst, size)], tail_sem)))
                size //= 2
        _start_all(tails)
        _wait_all(tails)

        used_rows = tail_start_ref[N_EXPERTS - 1] + tail_len_ref[N_EXPERTS - 1]
        piece = zero_ref.shape[0]

        def unused_copy(k):
            dst = pl.multiple_of(used_rows + k * piece, piece)
            return pltpu.make_async_copy(zero_ref, xs_ref.at[pl.ds(dst, piece)], tail_sem)

        n_pieces = (xs_ref.shape[0] - used_rows) // piece
        lax.fori_loop(0, n_pieces, lambda k, c: (unused_copy(k).start(), c)[1], 0)
        lax.fori_loop(0, n_pieces, lambda k, c: (unused_copy(k).wait(), c)[1], 0)


def _moe_dispatch(xn, sel, plan, n_slot_tiles, tm=MOE_TOKEN_TILE):
    n, d = xn.shape
    upper = jnp.asarray(np.triu(np.ones((tm, tm), np.float32), 1), BF16)
    smem = pl.BlockSpec(memory_space=pltpu.SMEM)
    return pl.pallas_call(
        _moe_dispatch_kernel,
        grid=(n // tm,),
        in_specs=[smem, smem, smem, smem, smem,
                  pl.BlockSpec((tm, d), lambda i: (i, 0)),
                  pl.BlockSpec((tm, N_EXPERTS), lambda i: (i, 0)),
                  _resident((tm, tm))],
        out_specs=pl.BlockSpec(memory_space=pl.ANY),
        out_shape=jax.ShapeDtypeStruct((n_slot_tiles * MOE_SLOT_TILE, d), BF16),
        scratch_shapes=[pltpu.VMEM((2, MOE_LOCAL_ROWS, d), BF16),
                        pltpu.VMEM((MOE_SLOT_TILE // 2, d), BF16),
                        pltpu.SemaphoreType.DMA((2,)), pltpu.SemaphoreType.DMA(())],
        compiler_params=_params("arbitrary"),
        name="moe_dispatch",
    )(plan["cnt"], plan["local"], plan["start"], plan["tail_start"], plan["tail_len"], xn, sel, upper)


def _moe_experts_kernel(te_ref, nu_ref, x_ref, wg_ref, wu_ref, wd_ref, y_ref, acc_ref):
    i = pl.program_id(0)
    f = pl.program_id(1)
    last = pl.num_programs(1) - 1
    used = i < nu_ref[0]

    @pl.when(used & (f == 0))
    def _():
        acc_ref[...] = jnp.zeros(acc_ref.shape, F32)

    @pl.when(used)
    def _():
        x = x_ref[...]
        acc = acc_ref[...]
        for c in range(wg_ref.shape[1] // MOE_FF_CHUNK):
            cols = slice(c * MOE_FF_CHUNK, (c + 1) * MOE_FF_CHUNK)
            gate = _dot(x, wg_ref[:, cols])
            up = _dot(x, wu_ref[:, cols])
            hid = (gate * jax.nn.sigmoid(gate) * up).astype(BF16)
            acc = acc + _dot(hid, wd_ref[cols, :])
        acc_ref[...] = acc

    @pl.when(used & (f == last))
    def _():
        y_ref[...] = acc_ref[...].astype(BF16)

    @pl.when(jnp.logical_not(used) & (f == last))
    def _():
        y_ref[...] = jnp.zeros(y_ref.shape, BF16)


def _moe_experts(xs, plan, wg, wu, wd, tf=1792):
    rows, d = xs.shape
    d_ff = wg.shape[2]
    nf = d_ff // tf
    tm = MOE_SLOT_TILE
    f_idx = lambda i, f, nu: jnp.where(i < nu[0], f, nf - 1)
    return pl.pallas_call(
        _moe_experts_kernel,
        grid_spec=pltpu.PrefetchScalarGridSpec(
            num_scalar_prefetch=2,
            grid=(rows // tm, nf),
            in_specs=[pl.BlockSpec((tm, d), lambda i, f, te, nu: (jnp.minimum(i, nu[0] - 1), 0)),
                      pl.BlockSpec((None, d, tf), lambda i, f, te, nu: (te[i], 0, f_idx(i, f, nu))),
                      pl.BlockSpec((None, d, tf), lambda i, f, te, nu: (te[i], 0, f_idx(i, f, nu))),
                      pl.BlockSpec((None, tf, d), lambda i, f, te, nu: (te[i], f_idx(i, f, nu), 0))],
            out_specs=pl.BlockSpec((tm, d), lambda i, f, te, nu: (i, 0)),
            scratch_shapes=[pltpu.VMEM((tm, d), F32)]),
        out_shape=jax.ShapeDtypeStruct((rows, d), BF16),
        compiler_params=_params("arbitrary", "arbitrary"),
        name="moe_experts",
    )(plan["tile_expert"], plan["n_used"], xs, wg, wu, wd)


def _moe_combine_kernel(cnt_ref, local_ref, start_ref, h_ref, gates_ref, sel_ref, lower_ref, p_ref,
                        gp_ref, wpg_ref, wpp_ref, gfin_ref, ys_ref, out_ref, ybuf_ref, sem):
    t = pl.program_id(0)
    tile = h_ref.shape[0]

    def copies(step):
        return _group_copies(cnt_ref, local_ref, start_ref, step, ybuf_ref.at[step % 2], ys_ref,
                             sem.at[step % 2], to_slots=False)

    @pl.when(t == 0)
    def _():
        ybuf_ref[...] = jnp.zeros(ybuf_ref.shape, BF16)
        _start_all(copies(t))

    @pl.when(t + 1 < pl.num_programs(0))
    def _():
        _start_all(copies(t + 1))

    sel = sel_ref[...] > 0.5
    gates = gates_ref[...]
    rank = _dot(lower_ref[...], sel_ref[...].astype(BF16))
    expert = lax.broadcasted_iota(jnp.int32, (1, N_EXPERTS), 1)
    base = jnp.zeros((1, N_EXPERTS), F32)
    for e in range(N_EXPERTS):
        base = jnp.where(expert == e, local_ref[t, e].astype(F32), base)
    pos = base + rank
    pos_lo = jnp.min(jnp.where(sel, pos, float(MOE_LOCAL_ROWS)), axis=1, keepdims=True)
    pos_hi = jnp.max(jnp.where(sel, pos, -1.0), axis=1, keepdims=True)
    gate_lo = jnp.sum(jnp.where(sel & (pos == pos_lo), gates, 0.0), axis=1, keepdims=True)
    gate_hi = jnp.sum(jnp.where(sel & (pos == pos_hi), gates, 0.0), axis=1, keepdims=True)
    col = lax.broadcasted_iota(jnp.int32, (tile, MOE_LOCAL_ROWS), 1).astype(F32)
    onehot_lo = jnp.where(col == pos_lo, 1.0, 0.0).astype(BF16)
    onehot_hi = jnp.where(col == pos_hi, 1.0, 0.0).astype(BF16)

    _wait_all(copies(t))
    y = ybuf_ref[t % 2]
    h = h_ref[...] + gate_lo * _dot(onehot_lo, y) + gate_hi * _dot(onehot_hi, y)
    h = _ple(h, p_ref[...], gp_ref[...], wpg_ref[...], wpp_ref[...])
    out_ref[...] = _rms(h, gfin_ref[...])


def _moe_combine(h, gates, sel, ys, plan, p, layer, gp, wpg, wpp, gfin, tm=MOE_TOKEN_TILE):
    n, d = h.shape
    lower = jnp.asarray(np.tril(np.ones((tm, tm), np.float32), -1), BF16)
    smem = pl.BlockSpec(memory_space=pltpu.SMEM)
    row = lambda w: pl.BlockSpec((tm, w), lambda i: (i, 0))
    return pl.pallas_call(
        _moe_combine_kernel,
        grid=(n // tm,),
        in_specs=[smem, smem, smem, row(d), row(N_EXPERTS), row(N_EXPERTS), _resident((tm, tm)),
                  _layer_rows(p, layer, tm), _resident(gp.shape), _resident(wpg.shape),
                  _resident(wpp.shape), _resident(gfin.shape), pl.BlockSpec(memory_space=pl.ANY)],
        out_specs=row(d),
        out_shape=jax.ShapeDtypeStruct((n, d), F32),
        scratch_shapes=[pltpu.VMEM((2, MOE_LOCAL_ROWS, d), BF16), pltpu.SemaphoreType.DMA((2,))],
        compiler_params=_params("arbitrary"),
        name="moe_combine_ple_norm",
    )(plan["cnt"], plan["local"], plan["start"], h, gates, sel, lower, p, gp, wpg, wpp, gfin, ys)


def kernel(x, p, rel_bias, final_norm, e_norm_mix, e_w_in, e_conv_w, e_sinks, e_w_out, e_norm_ffn, e_ffn_gate, e_ffn_up, e_ffn_down, e_norm_ple, e_ple_gate, e_ple_proj, o_norm_mix, o_w_qkv, o_w_o, o_norm_ffn, o_router, o_exp_gate, o_exp_up, o_exp_down, o_norm_ple, o_ple_gate, o_ple_proj):
    bsz, seq, d = x.shape
    n = bsz * seq
    bf = lambda w: w.astype(BF16)
    h = x.reshape(n, d)
    p2 = p.reshape(p.shape[0], n, p.shape[-1])

    kk = np.arange(2 * SWA_BLOCK)[:, None]
    qq = np.arange(SWA_BLOCK)[None, :]
    dist = qq - kk + SWA_BLOCK
    swa_bucket = np.where((dist >= 0) & (dist < SWA_BLOCK), _t5_bucket_np(dist), -1)
    swa_tab = _bias_table(rel_bias, swa_bucket, mult=LOG2E)
    kk = np.arange(MOBA_BLOCK)[:, None]
    qq = np.arange(MOBA_BLOCK)[None, :]
    own_bucket = np.where(qq >= kk, _t5_bucket_np(qq - kk), -1)
    moba_bucket = np.concatenate([_t5_bucket_np(qq - kk + MOBA_BLOCK), own_bucket], axis=1)
    moba_tab = _bias_table(rel_bias, moba_bucket, mult=LOG2E)

    w_in = e_w_in[0]
    q0 = 3 * SC_WIDTH
    k0 = q0 + SWA_Q_WIDTH
    v0 = k0 + SWA_KV_WIDTH
    half = N_HEADS // SWA_KV_HEADS
    q_cols = np.concatenate([np.arange(h * SWA_HEAD_DIM, (h + 1) * SWA_HEAD_DIM)
                             for j in range(half) for h in (j, half + j)])
    w_main = jnp.concatenate([w_in[:, :q0], w_in[:, q0:k0][:, q_cols], w_in[:, k0:v0]], axis=1)
    a_out, q0_, k0_, vt0 = _even_proj(h, e_norm_mix[0:1], bf(w_main), bf(w_in[:, v0:].T),
                                      e_conv_w[0], seq)
    b_out = _swa(q0_, k0_, vt0, e_sinks[0], swa_tab, bsz, seq)
    h = _even_ffn(h, a_out, b_out, p2, 0, bf(e_w_out[0]), e_norm_ffn[0:1],
                  bf(e_ffn_gate[0]), bf(e_ffn_up[0]), bf(e_ffn_down[0]),
                  e_norm_ple[0:1], bf(e_ple_gate[0]), bf(e_ple_proj[0]))

    w_qkv = o_w_qkv[0]
    q1, k1, vt1, kmean = _odd_proj(h, o_norm_mix[0:1], bf(w_qkv[:, :d]), bf(w_qkv[:, d:2 * d]),
                                   bf(w_qkv[:, 2 * d:].T))
    c_out = _moba(q1, k1, vt1, kmean, moba_tab, rel_bias, bsz, seq)
    h, xn, gates, sel, counts = _odd_router(h, c_out, bf(o_w_o[0]), o_norm_ffn[0:1], o_router[0])
    n_groups = (n // MOE_TOKEN_TILE) * N_EXPERTS
    max_rows = 2 * n + n_groups * (MOE_ALIGN - 1) + N_EXPERTS * (MOE_SLOT_TILE - 1)
    n_slot_tiles = -(-max_rows // MOE_SLOT_TILE) + 1
    plan = _moe_plan(counts, n_slot_tiles)
    xs = _moe_dispatch(xn, sel, plan, n_slot_tiles)
    ys = _moe_experts(xs, plan, bf(o_exp_gate[0]), bf(o_exp_up[0]), bf(o_exp_down[0]))
    out = _moe_combine(h, gates, sel, ys, plan, p2, 1, o_norm_ple[0:1], bf(o_ple_gate[0]),
                       bf(o_ple_proj[0]), final_norm.reshape(1, d))
    return out.reshape(bsz, seq, d)
```

```python
import functools
import math

import jax
import jax.numpy as jnp
import numpy as np
from jax import lax
from jax.experimental import pallas as pl
from jax.experimental.pallas import tpu as pltpu

F32 = jnp.float32
BF16 = jnp.bfloat16

EPS = 1e-6
N_HEADS = 8
N_BUCKETS = 32
MAX_DISTANCE = 128

SC_WIDTH = 512
CONV_WIDTH = 3
SWA_KV_HEADS = 2
SWA_HEAD_DIM = 64
SWA_BLOCK = 128
SWA_Q_WIDTH = N_HEADS * SWA_HEAD_DIM
SWA_KV_WIDTH = SWA_KV_HEADS * SWA_HEAD_DIM

MOBA_HEAD_DIM = 128
MOBA_BLOCK = 256
MOBA_TOPK = 3
MOBA_FAR_GROUP = 2
MOBA_HEAD_CHUNK = 1
MOBA_V_ROWS = MOBA_HEAD_DIM + 16

N_EXPERTS = 8
MOE_TOKEN_TILE = 512
MOE_SLOT_TILE = 512
MOE_ALIGN = 16
MOE_FF_CHUNK = 256
MOE_LOCAL_ROWS = -(-(2 * MOE_TOKEN_TILE + N_EXPERTS * (MOE_ALIGN - 1)) // 128) * 128

VMEM_LIMIT_BYTES = 56 * 1024 * 1024
NEG_INF = float("-inf")
LOG2E = math.log2(math.e)


def _params(*semantics):
    return pltpu.CompilerParams(dimension_semantics=semantics,
                                vmem_limit_bytes=VMEM_LIMIT_BYTES)


def _resident(shape):
    zeros = (0,) * len(shape)
    return pl.BlockSpec(shape, lambda *_: zeros, pipeline_mode=pl.Buffered(1))


def _rms(x, g):
    return x * lax.rsqrt(jnp.mean(x * x, axis=-1, keepdims=True) + EPS) * g


def _dot(a, b):
    return jnp.dot(a, b, preferred_element_type=F32)


def _dot_nt(a, b):
    return lax.dot_general(a, b, (((1,), (1,)), ((), ())), preferred_element_type=F32)


def _t5_bucket_np(dist):
    d = np.maximum(dist, 0).astype(np.int32)
    max_exact = N_BUCKETS // 2
    scaled = (np.log(np.maximum(d, max_exact).astype(np.float32) / np.float32(max_exact))
              / np.float32(math.log(MAX_DISTANCE / max_exact)))
    large = np.minimum(max_exact + (scaled * (N_BUCKETS - max_exact)).astype(np.int32),
                       N_BUCKETS - 1)
    return np.where(d < max_exact, d, large).astype(np.int32)


def _bias_table_kernel(rb_ref, bkt_ref, out_ref, *, mult):
    h = pl.program_id(0)
    bkt = bkt_ref[...]
    acc = jnp.full(bkt.shape, NEG_INF, F32)
    for b in range(N_BUCKETS):
        acc = jnp.where(bkt == b, rb_ref[b, h], acc)
    out_ref[...] = acc * mult


def _bias_table(rel_bias, bucket, mult=1.0):
    rows, cols = bucket.shape
    return pl.pallas_call(
        functools.partial(_bias_table_kernel, mult=mult),
        grid=(N_HEADS,),
        in_specs=[pl.BlockSpec(memory_space=pltpu.SMEM),
                  pl.BlockSpec((rows, cols), lambda h: (0, 0))],
        out_specs=pl.BlockSpec((None, rows, cols), lambda h: (h, 0, 0)),
        out_shape=jax.ShapeDtypeStruct((N_HEADS, rows, cols), F32),
        compiler_params=_params("arbitrary"),
        name="bias_table",
    )(rel_bias, jnp.asarray(bucket))


def _even_proj_kernel(x_ref, g_ref, w_ref, wvt_ref, cw_ref, a_ref, q_ref, k_ref, vt_ref, cu_scr, *,
                      tm, tiles_per_seq):
    i = pl.program_id(0)
    cu_scr[0:8, :] = jnp.where(i % tiles_per_seq == 0, 0.0, cu_scr[tm:tm + 8, :])
    xn = _rms(x_ref[...], g_ref[...]).astype(BF16)
    q0 = 3 * SC_WIDTH
    k0 = q0 + SWA_Q_WIDTH
    c_gate = _dot(xn, w_ref[:, SC_WIDTH:2 * SC_WIDTH])
    u = _dot(xn, w_ref[:, 2 * SC_WIDTH:3 * SC_WIDTH])
    q = _dot(xn, w_ref[:, q0:k0])
    k = _dot(xn, w_ref[:, k0:])
    vt = _dot_nt(wvt_ref[...], xn)
    b_gate = _dot(xn, w_ref[:, 0:SC_WIDTH])
    cu = c_gate * u
    cu_scr[8:8 + tm, :] = cu
    y = (cw_ref[0:1, :] * cu_scr[6:6 + tm, :] + cw_ref[1:2, :] * cu_scr[7:7 + tm, :]
         + cw_ref[2:3, :] * cu)
    q_ref[...] = (q * (SWA_HEAD_DIM ** -0.5 * LOG2E)).astype(BF16)
    k_ref[...] = k.astype(BF16)
    vt_ref[...] = vt.astype(BF16)
    a_ref[...] = (b_gate * y).astype(BF16)


def _even_proj(x, g, w, wvt, conv_w, seq, tm=512):
    n, d = x.shape
    row = lambda width: pl.BlockSpec((tm, width), lambda i: (i, 0))
    return pl.pallas_call(
        functools.partial(_even_proj_kernel, tm=tm, tiles_per_seq=seq // tm),
        grid=(n // tm,),
        in_specs=[row(d), _resident((1, d)), _resident(w.shape), _resident(wvt.shape),
                  _resident((CONV_WIDTH, SC_WIDTH))],
        out_specs=[row(SC_WIDTH), row(SWA_Q_WIDTH), row(SWA_KV_WIDTH),
                   pl.BlockSpec((SWA_KV_WIDTH, tm), lambda i: (0, i))],
        out_shape=[jax.ShapeDtypeStruct((n, SC_WIDTH), BF16),
                   jax.ShapeDtypeStruct((n, SWA_Q_WIDTH), BF16),
                   jax.ShapeDtypeStruct((n, SWA_KV_WIDTH), BF16),
                   jax.ShapeDtypeStruct((SWA_KV_WIDTH, n), BF16)],
        scratch_shapes=[pltpu.VMEM((tm + 8, SC_WIDTH), F32)],
        compiler_params=_params("arbitrary"),
        name="even_proj_conv",
    )(x, g, w, wvt, conv_w)


def _swa_kernel(sink_ref, q_ref, k_ref, kp_ref, vt_ref, vtp_ref, tab_ref, out_ref, *, nq):
    i = pl.program_id(1)
    blk = SWA_BLOCK
    hd = SWA_HEAD_DIM
    half = N_HEADS // SWA_KV_HEADS
    lane = lax.broadcasted_iota(jnp.int32, (blk, 2 * hd), 1)
    k_all = jnp.concatenate([kp_ref[...], k_ref[...]], axis=0)
    vt_all = jnp.concatenate([vtp_ref[...], vt_ref[...]], axis=1)

    scores = {}
    for s in range(nq):
        keys = k_all[s * blk:(s + 2) * blk]
        for j in range(half):
            q_pair = q_ref[s * blk:(s + 1) * blk, j * 2 * hd:(j + 1) * 2 * hd]
            scores[s, j] = _dot_nt(keys, jnp.where(lane < hd, q_pair, 0))
            scores[s, half + j] = _dot_nt(keys, jnp.where(lane >= hd, q_pair, 0))

    probs, denoms = {}, {}
    for s in range(nq):
        for h in range(N_HEADS):
            tab = tab_ref[h]
            if s == 0:
                tab = jnp.concatenate([jnp.where(i > 0, tab[:blk], NEG_INF), tab[blk:]], axis=0)
            sc = scores[s, h] + tab
            sink = sink_ref[h] * LOG2E
            m = jnp.maximum(jnp.max(sc, axis=0, keepdims=True), sink)
            p = jnp.exp2(sc - m)
            denoms[s, h] = jnp.sum(p, axis=0, keepdims=True) + jnp.exp2(sink - m)
            probs[s, h] = p.astype(BF16)

    for s in range(nq):
        outs = []
        for h in range(N_HEADS):
            g = h // half
            vt = vt_all[g * hd:(g + 1) * hd, s * blk:(s + 2) * blk]
            outs.append(_dot(vt, probs[s, h]) / denoms[s, h])
        out_ref[s * blk:(s + 1) * blk, :] = jnp.concatenate(outs, axis=0).T.astype(BF16)


def _swa(q, k, vt, sinks, tab, bsz, seq, nq=4):
    n = q.shape[0]
    nb = seq // SWA_BLOCK
    steps = nb // nq
    prev = lambda b, i: b * nb + jnp.maximum(i * nq - 1, 0)
    return pl.pallas_call(
        functools.partial(_swa_kernel, nq=nq),
        grid=(bsz, steps),
        in_specs=[pl.BlockSpec(memory_space=pltpu.SMEM),
                  pl.BlockSpec((nq * SWA_BLOCK, SWA_Q_WIDTH), lambda b, i: (b * steps + i, 0)),
                  pl.BlockSpec((nq * SWA_BLOCK, SWA_KV_WIDTH), lambda b, i: (b * steps + i, 0)),
                  pl.BlockSpec((SWA_BLOCK, SWA_KV_WIDTH), lambda b, i: (prev(b, i), 0)),
                  pl.BlockSpec((SWA_KV_WIDTH, nq * SWA_BLOCK), lambda b, i: (0, b * steps + i)),
                  pl.BlockSpec((SWA_KV_WIDTH, SWA_BLOCK), lambda b, i: (0, prev(b, i))),
                  _resident(tab.shape)],
        out_specs=pl.BlockSpec((nq * SWA_BLOCK, SWA_Q_WIDTH), lambda b, i: (b * steps + i, 0)),
        out_shape=jax.ShapeDtypeStruct((n, SWA_Q_WIDTH), BF16),
        compiler_params=_params("arbitrary", "arbitrary"),
        name="swa",
    )(sinks, q, k, k, vt, vt, tab)


def _ple(h, p, g, w_gate, w_proj):
    gate = jax.nn.sigmoid(_dot(_rms(h, g).astype(BF16), w_gate))
    return h + gate * _dot(p.astype(BF16), w_proj)


def _even_ffn_kernel(h_ref, a_ref, b_ref, p_ref, wo_ref, gf_ref, wg_ref, wu_ref, wd_ref,
                     gp_ref, wpg_ref, wpp_ref, out_ref, *, tf):
    h1 = (h_ref[...] + _dot(a_ref[...], wo_ref[0:SC_WIDTH, :])
          + _dot(b_ref[...], wo_ref[SC_WIDTH:, :]))
    xn = _rms(h1, gf_ref[...]).astype(BF16)
    d_ff = wg_ref.shape[1]
    acc = jnp.zeros(h1.shape, F32)
    for c in range(d_ff // tf):
        gate = _dot(xn, wg_ref[:, c * tf:(c + 1) * tf])
        up = _dot(xn, wu_ref[:, c * tf:(c + 1) * tf])
        hid = (gate * jax.nn.sigmoid(gate) * up).astype(BF16)
        acc = acc + _dot(hid, wd_ref[c * tf:(c + 1) * tf, :])
    h2 = h1 + acc
    out_ref[...] = _ple(h2, p_ref[...], gp_ref[...], wpg_ref[...], wpp_ref[...])


def _layer_rows(p, layer, tm):
    return pl.BlockSpec((None, tm, p.shape[2]), lambda i: (layer, i, 0))


def _even_ffn(h, a, b, p, layer, wo, gf, wg, wu, wd, gp, wpg, wpp, tm=512, tf=256):
    n, d = h.shape
    row = lambda w: pl.BlockSpec((tm, w), lambda i: (i, 0))
    return pl.pallas_call(
        functools.partial(_even_ffn_kernel, tf=tf),
        grid=(n // tm,),
        in_specs=[row(d), row(a.shape[1]), row(b.shape[1]), _layer_rows(p, layer, tm),
                  _resident(wo.shape), _resident(gf.shape), _resident(wg.shape),
                  _resident(wu.shape), _resident(wd.shape), _resident(gp.shape),
                  _resident(wpg.shape), _resident(wpp.shape)],
        out_specs=row(d),
        out_shape=jax.ShapeDtypeStruct((n, d), F32),
        compiler_params=_params("arbitrary"),
        name="even_out_ffn_ple",
    )(h, a, b, p, wo, gf, wg, wu, wd, gp, wpg, wpp)


def _odd_proj_kernel(h_ref, g_ref, wq_ref, wk_ref, wvt_ref, q_ref, k_ref, vt_ref, kmean_ref, *, tm):
    xn = _rms(h_ref[...], g_ref[...]).astype(BF16)
    q_ref[...] = (_dot(xn, wq_ref[...]) * (MOBA_HEAD_DIM ** -0.5 * LOG2E)).astype(BF16)
    k = _dot(xn, wk_ref[...])
    k_ref[...] = k.astype(BF16)
    vt = _dot_nt(wvt_ref[...], xn).astype(BF16)
    ones = jnp.ones((MOBA_V_ROWS - MOBA_HEAD_DIM, tm), BF16)
    for hh in range(N_HEADS):
        vt_ref[hh * MOBA_V_ROWS:hh * MOBA_V_ROWS + MOBA_HEAD_DIM, :] = (
            vt[hh * MOBA_HEAD_DIM:(hh + 1) * MOBA_HEAD_DIM])
        vt_ref[hh * MOBA_V_ROWS + MOBA_HEAD_DIM:(hh + 1) * MOBA_V_ROWS, :] = ones
    for r in range(tm // MOBA_BLOCK):
        kmean_ref[r] = jnp.mean(k[r * MOBA_BLOCK:(r + 1) * MOBA_BLOCK, :], axis=0, keepdims=True)


def _odd_proj(h, g, wq, wk, wvt, tm=512):
    n, d = h.shape
    bpt = tm // MOBA_BLOCK
    row = pl.BlockSpec((tm, d), lambda i: (i, 0))
    return pl.pallas_call(
        functools.partial(_odd_proj_kernel, tm=tm),
        grid=(n // tm,),
        in_specs=[row, _resident(g.shape), _resident(wq.shape), _resident(wk.shape),
                  _resident(wvt.shape)],
        out_specs=[row, row, pl.BlockSpec((N_HEADS * MOBA_V_ROWS, tm), lambda i: (0, i)),
                   pl.BlockSpec((bpt, 1, d), lambda i: (i, 0, 0))],
        out_shape=[jax.ShapeDtypeStruct((n, d), BF16), jax.ShapeDtypeStruct((n, d), BF16),
                   jax.ShapeDtypeStruct((N_HEADS * MOBA_V_ROWS, n), BF16),
                   jax.ShapeDtypeStruct((n // MOBA_BLOCK, 1, d), F32)],
        compiler_params=_params("arbitrary"),
        name="odd_qkv",
    )(h, g, wq, wk, wvt)


def _moba_kernel(rb_ref, q_ref, k_ref, vt_ref, km_ref, tab_ref, out_ref,
                 sel_scr, adj_scr, m_scr, acc_scr, far_a_scr, far_b_scr, gmax_scr, *, nblk, hg):
    head0 = pl.program_id(1) * hg
    qi = pl.program_id(2)
    bs = MOBA_BLOCK
    hd = MOBA_HEAD_DIM
    heads = [slice(hh * hd, (hh + 1) * hd) for hh in range(hg)]
    head_values = [slice(hh * MOBA_V_ROWS, (hh + 1) * MOBA_V_ROWS) for hh in range(hg)]

    def select_blocks():
        blk = lax.broadcasted_iota(jnp.int32, (nblk, bs), 0)
        past = blk < qi
        for hh in range(hg):
            gate = lax.dot_general(km_ref[:, heads[hh]], q_ref[:, heads[hh]].astype(F32),
                                   (((1,), (1,)), ((), ())), preferred_element_type=F32,
                                   precision=lax.Precision.HIGHEST)
            g = jnp.where(past, gate, NEG_INF)
            sel = jnp.zeros(gate.shape, jnp.bool_)
            for _ in range(MOBA_TOPK):
                top = jnp.max(g, axis=0, keepdims=True)
                idx = jnp.min(jnp.where(g == top, blk, nblk), axis=0, keepdims=True)
                hit = blk == idx
                sel = sel | hit
                g = jnp.where(hit, NEG_INF, g)
            far_bias = rb_ref[N_BUCKETS - 1, head0 + hh] * LOG2E
            sel_scr[hh] = jnp.where(sel & (blk < qi - 1), far_bias, NEG_INF)
            adj_scr[hh] = jnp.max(jnp.where(sel & (blk == qi - 1), 0.0, NEG_INF), axis=0,
                                  keepdims=True)

    def head_chunks():
        return [range(h0, min(h0 + MOBA_HEAD_CHUNK, hg)) for h0 in range(0, hg, MOBA_HEAD_CHUNK)]

    def near_update(start, nb, head_ids, tables, query_biases, between=lambda: None):
        scores = {hh: _dot_nt(k_ref[pl.ds(start, nb * bs), heads[hh]], q_ref[:, heads[hh]])
                  for hh in head_ids}
        between()
        probs = {}
        for hh in head_ids:
            parts, m_new = [], None
            for r in range(nb):
                s_r = scores[hh][r * bs:(r + 1) * bs] + tables(hh, r)
                col_max = jnp.max(s_r, axis=0, keepdims=True)
                if query_biases(hh, r) is not None:
                    col_max = col_max + query_biases(hh, r)
                parts.append(s_r)
                m_new = col_max if m_new is None else jnp.maximum(m_new, col_max)
            p = jnp.concatenate(
                [jnp.exp2(parts[r] - (m_new if query_biases(hh, r) is None
                                      else m_new - query_biases(hh, r))) for r in range(nb)], axis=0)
            m_scr[hh] = m_new
            probs[hh] = p.astype(BF16)
        for hh in head_ids:
            acc_scr[hh] = _dot(vt_ref[head_values[hh], pl.ds(start, nb * bs)], probs[hh])

    @pl.when(qi == 0)
    def _():
        near_update(0, 1, range(hg), lambda hh, r: tab_ref[hh, :, bs:2 * bs], lambda hh, r: None)

    far_rows = MOBA_FAR_GROUP * bs
    n_far = (qi - 1 + MOBA_FAR_GROUP - 1) // MOBA_FAR_GROUP
    far_bufs = (far_a_scr, far_b_scr)

    def far_start(g):
        return pl.multiple_of(g * far_rows, far_rows)

    def far_bias(hh, g, r):
        return sel_scr[hh, pl.ds(g * MOBA_FAR_GROUP + r, 1), :]

    def far_scores(g, buf, head_ids):
        keys = k_ref.at[pl.ds(far_start(g), far_rows)]
        scores = {hh: _dot_nt(keys[:, heads[hh]], q_ref[:, heads[hh]]) for hh in head_ids}
        for hh in head_ids:
            far_bufs[buf][hh] = scores[hh]
            group_max = None
            for r in range(MOBA_FAR_GROUP):
                col_max = (jnp.max(scores[hh][r * bs:(r + 1) * bs], axis=0, keepdims=True)
                           + far_bias(hh, g, r))
                group_max = col_max if group_max is None else jnp.maximum(group_max, col_max)
            gmax_scr[buf, hh] = group_max

    def far_softmax(g, buf, head_ids):
        probs, alphas = {}, {}
        for hh in head_ids:
            m_old = m_scr[hh]
            m_new = jnp.maximum(m_old, gmax_scr[buf, hh])
            alphas[hh] = jnp.exp2(m_old - m_new)
            p = jnp.concatenate(
                [jnp.exp2(far_bufs[buf][hh, r * bs:(r + 1) * bs, :] - (m_new - far_bias(hh, g, r)))
                 for r in range(MOBA_FAR_GROUP)], axis=0)
            m_scr[hh] = m_new
            probs[hh] = p.astype(BF16)
        for hh in head_ids:
            acc_scr[hh] = (alphas[hh] * acc_scr[hh]
                           + _dot(vt_ref[head_values[hh], pl.ds(far_start(g), far_rows)], probs[hh]))

    def far_step(g, buf, prefetch):
        for head_ids in head_chunks():
            if prefetch:
                far_scores(g + 1, 1 - buf, head_ids)
            far_softmax(g, buf, head_ids)

    @pl.when(qi > 0)
    def _():
        select_blocks()
        start = pl.multiple_of((qi - 1) * bs, bs)
        near_update(start, 2, range(hg), lambda hh, r: tab_ref[hh, :, r * bs:(r + 1) * bs],
                    lambda hh, r: adj_scr[hh] if r == 0 else None,
                    between=lambda: far_scores(0, 0, range(hg)))

    def far_pair(i, carry):
        far_step(2 * i, 0, True)
        far_step(2 * i + 1, 1, True)
        return carry

    lax.fori_loop(0, (n_far - 1) // 2, far_pair, 0)

    @pl.when((n_far > 0) & (n_far % 2 == 0))
    def _():
        far_step(n_far - 2, 0, True)
        far_step(n_far - 1, 1, False)

    @pl.when(n_far % 2 == 1)
    def _():
        far_step(n_far - 1, 0, False)

    for hh in range(hg):
        out_ref[:, heads[hh]] = (acc_scr[hh, 0:hd, :] / acc_scr[hh, hd:hd + 1, :]).T.astype(BF16)


def _moba(q, k, vt, kmean, tab, rel_bias, bsz, seq, hg=N_HEADS):
    n, d = q.shape
    nblk = seq // MOBA_BLOCK
    assert nblk % MOBA_FAR_GROUP == 0, "far-block groups must not run past the sequence"
    bs = MOBA_BLOCK
    w = hg * MOBA_HEAD_DIM
    return pl.pallas_call(
        functools.partial(_moba_kernel, nblk=nblk, hg=hg),
        grid=(bsz, N_HEADS // hg, nblk),
        in_specs=[pl.BlockSpec(memory_space=pltpu.SMEM),
                  pl.BlockSpec((bs, w), lambda b, g, i: (b * nblk + i, g)),
                  pl.BlockSpec((seq, w), lambda b, g, i: (b, g), pipeline_mode=pl.Buffered(1)),
                  pl.BlockSpec((hg * MOBA_V_ROWS, seq), lambda b, g, i: (g, b),
                               pipeline_mode=pl.Buffered(1)),
                  pl.BlockSpec((None, nblk, w), lambda b, g, i: (b, 0, g)),
                  pl.BlockSpec((hg, bs, 2 * bs), lambda b, g, i: (g, 0, 0),
                               pipeline_mode=pl.Buffered(1))],
        out_specs=pl.BlockSpec((bs, w), lambda b, g, i: (b * nblk + i, g)),
        out_shape=jax.ShapeDtypeStruct((n, d), BF16),
        scratch_shapes=[pltpu.VMEM((hg, nblk, bs), F32), pltpu.VMEM((hg, 1, bs), F32),
                        pltpu.VMEM((hg, 1, bs), F32), pltpu.VMEM((hg, MOBA_V_ROWS, bs), F32),
                        pltpu.VMEM((hg, MOBA_FAR_GROUP * bs, bs), F32),
                        pltpu.VMEM((hg, MOBA_FAR_GROUP * bs, bs), F32),
                        pltpu.VMEM((2, hg, 1, bs), F32)],
        compiler_params=_params("arbitrary", "arbitrary", "arbitrary"),
        name="moba",
    )(rel_bias, q, k, vt, kmean.reshape(bsz, nblk, d), tab)


def _odd_router_kernel(h_ref, c_ref, wo_ref, g_ref, wr_ref, h_out_ref, xn_ref, gates_ref, sel_ref,
                       cnt_ref):
    wr = wr_ref[...]
    wr_hi = wr.astype(BF16)
    wr_lo = (wr - wr_hi.astype(F32)).astype(BF16)
    n_parts = 2
    part = h_ref.shape[0] // n_parts
    rows = [slice(k * part, (k + 1) * part) for k in range(n_parts)]
    h1 = [h_ref[r, :] + _dot(c_ref[r, :], wo_ref[...]) for r in rows]
    xn = [_rms(h, g_ref[...]) for h in h1]
    xn_hi = [x.astype(BF16) for x in xn]
    xn_lo = [(x - hi.astype(F32)).astype(BF16) for x, hi in zip(xn, xn_hi)]
    logits = [_dot(hi, wr_hi) + (_dot(lo, wr_hi) + _dot(hi, wr_lo)) for hi, lo in zip(xn_hi, xn_lo)]
    count = jnp.zeros((1, N_EXPERTS), F32)
    for k, r in enumerate(rows):
        h_out_ref[r, :] = h1[k]
        xn_ref[r, :] = xn_hi[k]
        lane = lax.broadcasted_iota(jnp.int32, logits[k].shape, 1)
        v1 = jnp.max(logits[k], axis=-1, keepdims=True)
        i1 = jnp.min(jnp.where(logits[k] == v1, lane, N_EXPERTS), axis=-1, keepdims=True)
        rest = jnp.where(lane == i1, NEG_INF, logits[k])
        v2 = jnp.max(rest, axis=-1, keepdims=True)
        i2 = jnp.min(jnp.where(rest == v2, lane, N_EXPERTS), axis=-1, keepdims=True)
        e2 = jnp.exp(v2 - v1)
        w1 = 1.0 / (1.0 + e2)
        w2 = e2 / (1.0 + e2)
        gates_ref[r, :] = jnp.where(lane == i1, w1, 0.0) + jnp.where(lane == i2, w2, 0.0)
        sel = jnp.where((lane == i1) | (lane == i2), 1.0, 0.0)
        sel_ref[r, :] = sel
        count = count + jnp.sum(sel, axis=0, keepdims=True)
    cnt_ref[0] = count


def _odd_router(h, c, wo, g, wr, tm=MOE_TOKEN_TILE):
    n, d = h.shape
    row = lambda w: pl.BlockSpec((tm, w), lambda i: (i, 0))
    return pl.pallas_call(
        _odd_router_kernel,
        grid=(n // tm,),
        in_specs=[row(d), row(d), _resident(wo.shape), _resident(g.shape), _resident(wr.shape)],
        out_specs=[row(d), row(d), row(N_EXPERTS), row(N_EXPERTS),
                   pl.BlockSpec((1, 1, N_EXPERTS), lambda i: (i, 0, 0))],
        out_shape=[jax.ShapeDtypeStruct((n, d), F32), jax.ShapeDtypeStruct((n, d), BF16),
                   jax.ShapeDtypeStruct((n, N_EXPERTS), F32),
                   jax.ShapeDtypeStruct((n, N_EXPERTS), F32),
                   jax.ShapeDtypeStruct((n // tm, 1, N_EXPERTS), F32)],
        compiler_params=_params("arbitrary"),
        name="odd_out_router",
    )(h, c, wo, g, wr)


def _moe_plan(counts, n_slot_tiles):
    cnt = counts.reshape(-1, N_EXPERTS).astype(jnp.int32)
    grp = (cnt + MOE_ALIGN - 1) // MOE_ALIGN * MOE_ALIGN
    local = jnp.cumsum(grp, axis=1) - grp
    tot = jnp.sum(grp, axis=0)
    region = (tot + MOE_SLOT_TILE - 1) // MOE_SLOT_TILE * MOE_SLOT_TILE
    region_end = jnp.cumsum(region)
    region_start = region_end - region
    start = region_start[None, :] + jnp.cumsum(grp, axis=0) - grp
    n_used = region_end[-1] // MOE_SLOT_TILE
    tile_row = jnp.arange(n_slot_tiles, dtype=jnp.int32) * MOE_SLOT_TILE
    tile_expert = jnp.sum((tile_row[:, None] >= region_end[None, :]).astype(jnp.int32), axis=1)
    tile_expert = jnp.minimum(tile_expert, N_EXPERTS - 1)
    tile_expert = jnp.where(tile_row < region_end[-1], tile_expert, tile_expert[n_used - 1])
    return dict(cnt=cnt, local=local.astype(jnp.int32), start=start.astype(jnp.int32),
                tail_start=(region_start + tot).astype(jnp.int32),
                tail_len=(region - tot).astype(jnp.int32), tile_expert=tile_expert,
                n_used=n_used.reshape(1).astype(jnp.int32))


def _group_copies(cnt_ref, local_ref, start_ref, t, local_buf, slot_array, sem, to_slots):
    copies = []
    for e in range(N_EXPERTS):
        rows = (cnt_ref[t, e] + MOE_ALIGN - 1) // MOE_ALIGN * MOE_ALIGN
        size = MOE_TOKEN_TILE
        while size >= MOE_ALIGN:
            off = rows & ~(2 * size - 1)
            local = local_buf.at[pl.ds(pl.multiple_of(local_ref[t, e] + off, MOE_ALIGN), size)]
            slots = slot_array.at[pl.ds(pl.multiple_of(start_ref[t, e] + off, MOE_ALIGN), size)]
            src, dst = (local, slots) if to_slots else (slots, local)
            copies.append(((rows & size) != 0, pltpu.make_async_copy(src, dst, sem)))
            size //= 2
    return copies


def _start_all(copies):
    for cond, cp in copies:
        pl.when(cond)(cp.start)


def _wait_all(copies):
    for cond, cp in copies:
        pl.when(cond)(cp.wait)


def _moe_dispatch_kernel(cnt_ref, local_ref, start_ref, tail_start_ref, tail_len_ref, x_ref, sel_ref,
                         upper_ref, xs_ref, stage_ref, zero_ref, sem, tail_sem):
    t = pl.program_id(0)
    last = pl.num_programs(0) - 1
    tile = x_ref.shape[0]
    slot = t % 2
    eye = jnp.where(lax.broadcasted_iota(jnp.int32, (N_EXPERTS, N_EXPERTS), 0)
                    == lax.broadcasted_iota(jnp.int32, (N_EXPERTS, N_EXPERTS), 1), 1.0, 0.0)
    sel_t = _dot_nt(eye.astype(BF16), sel_ref[...].astype(BF16))
    rank_t = _dot(sel_t.astype(BF16), upper_ref[...])
    expert = lax.broadcasted_iota(jnp.int32, (N_EXPERTS, 1), 0)
    base = jnp.zeros((N_EXPERTS, 1), F32)
    for e in range(N_EXPERTS):
        base = jnp.where(expert == e, local_ref[t, e].astype(F32), base)
    pos = base + rank_t
    pos_lo = jnp.min(jnp.where(sel_t > 0.5, pos, float(MOE_LOCAL_ROWS)), axis=0, keepdims=True)
    pos_hi = jnp.max(jnp.where(sel_t > 0.5, pos, -1.0), axis=0, keepdims=True)
    row = lax.broadcasted_iota(jnp.int32, (MOE_LOCAL_ROWS, tile), 0).astype(F32)
    onehot = jnp.where(row == pos_lo, 1.0, jnp.where(row == pos_hi, 1.0, 0.0)).astype(BF16)
    stage_ref[slot] = _dot(onehot, x_ref[...]).astype(BF16)

    def copies(step):
        return _group_copies(cnt_ref, local_ref, start_ref, step, stage_ref.at[step % 2], xs_ref,
                             sem.at[step % 2], to_slots=True)

    _start_all(copies(t))

    @pl.when(t > 0)
    def _():
        _wait_all(copies(t - 1))

    @pl.when(t == last)
    def _():
        _wait_all(copies(t))
        zero_ref[...] = jnp.zeros(zero_ref.shape, BF16)
        tails = []
        for e in range(N_EXPERTS):
            size = MOE_SLOT_TILE // 2
            while size >= MOE_ALIGN:
                off = tail_len_ref[e] & ~(2 * size - 1)
                dst = pl.multiple_of(tail_start_ref[e] + off, MOE_ALIGN)
                tails.append(((tail_len_ref[e] & size) != 0, pltpu.make_async_copy(
                    zero_ref.at[pl.ds(0, size)], xs_ref.at[pl.ds(dst, size)], tail_sem)))
                size //= 2
        _start_all(tails)
        _wait_all(tails)

        used_rows = tail_start_ref[N_EXPERTS - 1] + tail_len_ref[N_EXPERTS - 1]
        piece = zero_ref.shape[0]

        def unused_copy(k):
            dst = pl.multiple_of(used_rows + k * piece, piece)
            return pltpu.make_async_copy(zero_ref, xs_ref.at[pl.ds(dst, piece)], tail_sem)

        n_pieces = (xs_ref.shape[0] - used_rows) // piece
        lax.fori_loop(0, n_pieces, lambda k, c: (unused_copy(k).start(), c)[1], 0)
        lax.fori_loop(0, n_pieces, lambda k, c: (unused_copy(k).wait(), c)[1], 0)


def _moe_dispatch(xn, sel, plan, n_slot_tiles, tm=MOE_TOKEN_TILE):
    n, d = xn.shape
    upper = jnp.asarray(np.triu(np.ones((tm, tm), np.float32), 1), BF16)
    smem = pl.BlockSpec(memory_space=pltpu.SMEM)
    return pl.pallas_call(
        _moe_dispatch_kernel,
        grid=(n // tm,),
        in_specs=[smem, smem, smem, smem, smem,
                  pl.BlockSpec((tm, d), lambda i: (i, 0)),
                  pl.BlockSpec((tm, N_EXPERTS), lambda i: (i, 0)),
                  _resident((tm, tm))],
        out_specs=pl.BlockSpec(memory_space=pl.ANY),
        out_shape=jax.ShapeDtypeStruct((n_slot_tiles * MOE_SLOT_TILE, d), BF16),
        scratch_shapes=[pltpu.VMEM((2, MOE_LOCAL_ROWS, d), BF16),
                        pltpu.VMEM((MOE_SLOT_TILE // 2, d), BF16),
                        pltpu.SemaphoreType.DMA((2,)), pltpu.SemaphoreType.DMA(())],
        compiler_params=_params("arbitrary"),
        name="moe_dispatch",
    )(plan["cnt"], plan["local"], plan["start"], plan["tail_start"], plan["tail_len"], xn, sel, upper)


def _moe_experts_kernel(te_ref, nu_ref, x_ref, wg_ref, wu_ref, wd_ref, y_ref, acc_ref):
    i = pl.program_id(0)
    f = pl.program_id(1)
    last = pl.num_programs(1) - 1
    used = i < nu_ref[0]

    @pl.when(used & (f == 0))
    def _():
        acc_ref[...] = jnp.zeros(acc_ref.shape, F32)

    @pl.when(used)
    def _():
        x = x_ref[...]
        acc = acc_ref[...]
        for c in range(wg_ref.shape[1] // MOE_FF_CHUNK):
            cols = slice(c * MOE_FF_CHUNK, (c + 1) * MOE_FF_CHUNK)
            gate = _dot(x, wg_ref[:, cols])
            up = _dot(x, wu_ref[:, cols])
            hid = (gate * jax.nn.sigmoid(gate) * up).astype(BF16)
            acc = acc + _dot(hid, wd_ref[cols, :])
        acc_ref[...] = acc

    @pl.when(used & (f == last))
    def _():
        y_ref[...] = acc_ref[...].astype(BF16)

    @pl.when(jnp.logical_not(used) & (f == last))
    def _():
        y_ref[...] = jnp.zeros(y_ref.shape, BF16)


def _moe_experts(xs, plan, wg, wu, wd, tf=1792):
    rows, d = xs.shape
    d_ff = wg.shape[2]
    nf = d_ff // tf
    tm = MOE_SLOT_TILE
    f_idx = lambda i, f, nu: jnp.where(i < nu[0], f, nf - 1)
    return pl.pallas_call(
        _moe_experts_kernel,
        grid_spec=pltpu.PrefetchScalarGridSpec(
            num_scalar_prefetch=2,
            grid=(rows // tm, nf),
            in_specs=[pl.BlockSpec((tm, d), lambda i, f, te, nu: (jnp.minimum(i, nu[0] - 1), 0)),
                      pl.BlockSpec((None, d, tf), lambda i, f, te, nu: (te[i], 0, f_idx(i, f, nu))),
                      pl.BlockSpec((None, d, tf), lambda i, f, te, nu: (te[i], 0, f_idx(i, f, nu))),
                      pl.BlockSpec((None, tf, d), lambda i, f, te, nu: (te[i], f_idx(i, f, nu), 0))],
            out_specs=pl.BlockSpec((tm, d), lambda i, f, te, nu: (i, 0)),
            scratch_shapes=[pltpu.VMEM((tm, d), F32)]),
        out_shape=jax.ShapeDtypeStruct((rows, d), BF16),
        compiler_params=_params("arbitrary", "arbitrary"),
        name="moe_experts",
    )(plan["tile_expert"], plan["n_used"], xs, wg, wu, wd)


def _moe_combine_kernel(cnt_ref, local_ref, start_ref, h_ref, gates_ref, sel_ref, lower_ref, p_ref,
                        gp_ref, wpg_ref, wpp_ref, gfin_ref, ys_ref, out_ref, ybuf_ref, sem):
    t = pl.program_id(0)
    tile = h_ref.shape[0]

    def copies(step):
        return _group_copies(cnt_ref, local_ref, start_ref, step, ybuf_ref.at[step % 2], ys_ref,
                             sem.at[step % 2], to_slots=False)

    @pl.when(t == 0)
    def _():
        ybuf_ref[...] = jnp.zeros(ybuf_ref.shape, BF16)
        _start_all(copies(t))

    @pl.when(t + 1 < pl.num_programs(0))
    def _():
        _start_all(copies(t + 1))

    sel = sel_ref[...] > 0.5
    gates = gates_ref[...]
    rank = _dot(lower_ref[...], sel_ref[...].astype(BF16))
    expert = lax.broadcasted_iota(jnp.int32, (1, N_EXPERTS), 1)
    base = jnp.zeros((1, N_EXPERTS), F32)
    for e in range(N_EXPERTS):
        base = jnp.where(expert == e, local_ref[t, e].astype(F32), base)
    pos = base + rank
    pos_lo = jnp.min(jnp.where(sel, pos, float(MOE_LOCAL_ROWS)), axis=1, keepdims=True)
    pos_hi = jnp.max(jnp.where(sel, pos, -1.0), axis=1, keepdims=True)
    gate_lo = jnp.sum(jnp.where(sel & (pos == pos_lo), gates, 0.0), axis=1, keepdims=True)
    gate_hi = jnp.sum(jnp.where(sel & (pos == pos_hi), gates, 0.0), axis=1, keepdims=True)
    col = lax.broadcasted_iota(jnp.int32, (tile, MOE_LOCAL_ROWS), 1).astype(F32)
    onehot_lo = jnp.where(col == pos_lo, 1.0, 0.0).astype(BF16)
    onehot_hi = jnp.where(col == pos_hi, 1.0, 0.0).astype(BF16)

    _wait_all(copies(t))
    y = ybuf_ref[t % 2]
    h = h_ref[...] + gate_lo * _dot(onehot_lo, y) + gate_hi * _dot(onehot_hi, y)
    h = _ple(h, p_ref[...], gp_ref[...], wpg_ref[...], wpp_ref[...])
    out_ref[...] = _rms(h, gfin_ref[...])


def _moe_combine(h, gates, sel, ys, plan, p, layer, gp, wpg, wpp, gfin, tm=MOE_TOKEN_TILE):
    n, d = h.shape
    lower = jnp.asarray(np.tril(np.ones((tm, tm), np.float32), -1), BF16)
    smem = pl.BlockSpec(memory_space=pltpu.SMEM)
    row = lambda w: pl.BlockSpec((tm, w), lambda i: (i, 0))
    return pl.pallas_call(
        _moe_combine_kernel,
        grid=(n // tm,),
        in_specs=[smem, smem, smem, row(d), row(N_EXPERTS), row(N_EXPERTS), _resident((tm, tm)),
                  _layer_rows(p, layer, tm), _resident(gp.shape), _resident(wpg.shape),
                  _resident(wpp.shape), _resident(gfin.shape), pl.BlockSpec(memory_space=pl.ANY)],
        out_specs=row(d),
        out_shape=jax.ShapeDtypeStruct((n, d), F32),
        scratch_shapes=[pltpu.VMEM((2, MOE_LOCAL_ROWS, d), BF16), pltpu.SemaphoreType.DMA((2,))],
        compiler_params=_params("arbitrary"),
        name="moe_combine_ple_norm",
    )(plan["cnt"], plan["local"], plan["start"], h, gates, sel, lower, p, gp, wpg, wpp, gfin, ys)


def kernel(x, p, rel_bias, final_norm, e_norm_mix, e_w_in, e_conv_w, e_sinks, e_w_out, e_norm_ffn, e_ffn_gate, e_ffn_up, e_ffn_down, e_norm_ple, e_ple_gate, e_ple_proj, o_norm_mix, o_w_qkv, o_w_o, o_norm_ffn, o_router, o_exp_gate, o_exp_up, o_exp_down, o_norm_ple, o_ple_gate, o_ple_proj):
    bsz, seq, d = x.shape
    n = bsz * seq
    bf = lambda w: w.astype(BF16)
    h = x.reshape(n, d)
    p2 = p.reshape(p.shape[0], n, p.shape[-1])

    kk = np.arange(2 * SWA_BLOCK)[:, None]
    qq = np.arange(SWA_BLOCK)[None, :]
    dist = qq - kk + SWA_BLOCK
    swa_bucket = np.where((dist >= 0) & (dist < SWA_BLOCK), _t5_bucket_np(dist), -1)
    swa_tab = _bias_table(rel_bias, swa_bucket, mult=LOG2E)
    kk = np.arange(MOBA_BLOCK)[:, None]
    qq = np.arange(MOBA_BLOCK)[None, :]
    own_bucket = np.where(qq >= kk, _t5_bucket_np(qq - kk), -1)
    moba_bucket = np.concatenate([_t5_bucket_np(qq - kk + MOBA_BLOCK), own_bucket], axis=1)
    moba_tab = _bias_table(rel_bias, moba_bucket, mult=LOG2E)

    w_in = e_w_in[0]
    q0 = 3 * SC_WIDTH
    k0 = q0 + SWA_Q_WIDTH
    v0 = k0 + SWA_KV_WIDTH
    half = N_HEADS // SWA_KV_HEADS
    q_cols = np.concatenate([np.arange(h * SWA_HEAD_DIM, (h + 1) * SWA_HEAD_DIM)
                             for j in range(half) for h in (j, half + j)])
    w_main = jnp.concatenate([w_in[:, :q0], w_in[:, q0:k0][:, q_cols], w_in[:, k0:v0]], axis=1)
    a_out, q0_, k0_, vt0 = _even_proj(h, e_norm_mix[0:1], bf(w_main), bf(w_in[:, v0:].T),
                                      e_conv_w[0], seq)
    b_out = _swa(q0_, k0_, vt0, e_sinks[0], swa_tab, bsz, seq)
    h = _even_ffn(h, a_out, b_out, p2, 0, bf(e_w_out[0]), e_norm_ffn[0:1],
                  bf(e_ffn_gate[0]), bf(e_ffn_up[0]), bf(e_ffn_down[0]),
                  e_norm_ple[0:1], bf(e_ple_gate[0]), bf(e_ple_proj[0]))

    w_qkv = o_w_qkv[0]
    q1, k1, vt1, kmean = _odd_proj(h, o_norm_mix[0:1], bf(w_qkv[:, :d]), bf(w_qkv[:, d:2 * d]),
                                   bf(w_qkv[:, 2 * d:].T))
    c_out = _moba(q1, k1, vt1, kmean, moba_tab, rel_bias, bsz, seq)
    h, xn, gates, sel, counts = _odd_router(h, c_out, bf(o_w_o[0]), o_norm_ffn[0:1], o_router[0])
    n_groups = (n // MOE_TOKEN_TILE) * N_EXPERTS
    max_rows = 2 * n + n_groups * (MOE_ALIGN - 1) + N_EXPERTS * (MOE_SLOT_TILE - 1)
    n_slot_tiles = -(-max_rows // MOE_SLOT_TILE) + 1
    plan = _moe_plan(counts, n_slot_tiles)
    xs = _moe_dispatch(xn, sel, plan, n_slot_tiles)
    ys = _moe_experts(xs, plan, bf(o_exp_gate[0]), bf(o_exp_up[0]), bf(o_exp_down[0]))
    out = _moe_combine(h, gates, sel, ys, plan, p2, 1, o_norm_ple[0:1], bf(o_ple_gate[0]),
                       bf(o_ple_proj[0]), final_norm.reshape(1, d))
    return out.reshape(bsz, seq, d)
```

```python
import functools
import math

import jax
import jax.numpy as jnp
import numpy as np
from jax import lax
from jax.experimental import pallas as pl
from jax.experimental.pallas import tpu as pltpu

F32 = jnp.float32
BF16 = jnp.bfloat16

EPS = 1e-6
N_HEADS = 8
N_BUCKETS = 32
MAX_DISTANCE = 128

SC_WIDTH = 512
CONV_WIDTH = 3
SWA_KV_HEADS = 2
SWA_HEAD_DIM = 64
SWA_BLOCK = 128
SWA_Q_WIDTH = N_HEADS * SWA_HEAD_DIM
SWA_KV_WIDTH = SWA_KV_HEADS * SWA_HEAD_DIM

MOBA_HEAD_DIM = 128
MOBA_BLOCK = 256
MOBA_TOPK = 3
MOBA_FAR_GROUP = 2
MOBA_HEAD_CHUNK = 1
MOBA_V_ROWS = MOBA_HEAD_DIM + 16

N_EXPERTS = 8
MOE_TOKEN_TILE = 512
MOE_SLOT_TILE = 512
MOE_ALIGN = 16
MOE_FF_CHUNK = 256
MOE_LOCAL_ROWS = -(-(2 * MOE_TOKEN_TILE + N_EXPERTS * (MOE_ALIGN - 1)) // 128) * 128

VMEM_LIMIT_BYTES = 56 * 1024 * 1024
NEG_INF = float("-inf")
LOG2E = math.log2(math.e)


def _params(*semantics):
    return pltpu.CompilerParams(dimension_semantics=semantics,
                                vmem_limit_bytes=VMEM_LIMIT_BYTES)


def _resident(shape):
    zeros = (0,) * len(shape)
    return pl.BlockSpec(shape, lambda *_: zeros, pipeline_mode=pl.Buffered(1))


def _rms(x, g):
    return x * lax.rsqrt(jnp.mean(x * x, axis=-1, keepdims=True) + EPS) * g


def _dot(a, b):
    return jnp.dot(a, b, preferred_element_type=F32)


def _swiglu(x, wg_ref, wu_ref, wd_ref, chunk, acc):
    for c in range(wg_ref.shape[1] // chunk):
        cols = slice(c * chunk, (c + 1) * chunk)
        gate = _dot(x, wg_ref[:, cols])
        up = _dot(x, wu_ref[:, cols])
        hid = (gate * jax.nn.sigmoid(gate) * up).astype(BF16)
        acc = acc + _dot(hid, wd_ref[cols, :])
    return acc


def _dot_nt(a, b):
    return lax.dot_general(a, b, (((1,), (1,)), ((), ())), preferred_element_type=F32)


def _t5_bucket_np(dist):
    d = np.maximum(dist, 0).astype(np.int32)
    max_exact = N_BUCKETS // 2
    scaled = (np.log(np.maximum(d, max_exact).astype(np.float32) / np.float32(max_exact))
              / np.float32(math.log(MAX_DISTANCE / max_exact)))
    large = np.minimum(max_exact + (scaled * (N_BUCKETS - max_exact)).astype(np.int32),
                       N_BUCKETS - 1)
    return np.where(d < max_exact, d, large).astype(np.int32)


def _bias_table_kernel(rb_ref, bkt_ref, out_ref, *, mult):
    h = pl.program_id(0)
    bkt = bkt_ref[...]
    acc = jnp.full(bkt.shape, NEG_INF, F32)
    for b in range(N_BUCKETS):
        acc = jnp.where(bkt == b, rb_ref[b, h], acc)
    out_ref[...] = acc * mult


def _bias_table(rel_bias, bucket, mult=1.0):
    rows, cols = bucket.shape
    return pl.pallas_call(
        functools.partial(_bias_table_kernel, mult=mult),
        grid=(N_HEADS,),
        in_specs=[pl.BlockSpec(memory_space=pltpu.SMEM),
                  pl.BlockSpec((rows, cols), lambda h: (0, 0))],
        out_specs=pl.BlockSpec((None, rows, cols), lambda h: (h, 0, 0)),
        out_shape=jax.ShapeDtypeStruct((N_HEADS, rows, cols), F32),
        compiler_params=_params("arbitrary"),
        name="bias_table",
    )(rel_bias, jnp.asarray(bucket))


def _even_proj_kernel(x_ref, g_ref, w_ref, wvt_ref, cw_ref, a_ref, q_ref, k_ref, vt_ref, cu_scr, *,
                      tm, tiles_per_seq):
    i = pl.program_id(0)
    cu_scr[0:8, :] = jnp.where(i % tiles_per_seq == 0, 0.0, cu_scr[tm:tm + 8, :])
    xn = _rms(x_ref[...], g_ref[...]).astype(BF16)
    q0 = 3 * SC_WIDTH
    k0 = q0 + SWA_Q_WIDTH
    c_gate = _dot(xn, w_ref[:, SC_WIDTH:2 * SC_WIDTH])
    u = _dot(xn, w_ref[:, 2 * SC_WIDTH:3 * SC_WIDTH])
    q = _dot(xn, w_ref[:, q0:k0])
    k = _dot(xn, w_ref[:, k0:])
    vt = _dot_nt(wvt_ref[...], xn)
    b_gate = _dot(xn, w_ref[:, 0:SC_WIDTH])
    cu = c_gate * u
    cu_scr[8:8 + tm, :] = cu
    y = (cw_ref[0:1, :] * cu_scr[6:6 + tm, :] + cw_ref[1:2, :] * cu_scr[7:7 + tm, :]
         + cw_ref[2:3, :] * cu)
    q_ref[...] = (q * (SWA_HEAD_DIM ** -0.5 * LOG2E)).astype(BF16)
    k_ref[...] = k.astype(BF16)
    vt_ref[...] = vt.astype(BF16)
    a_ref[...] = (b_gate * y).astype(BF16)


def _even_proj(x, g, w, wvt, conv_w, seq, tm=512):
    n, d = x.shape
    row = lambda width: pl.BlockSpec((tm, width), lambda i: (i, 0))
    return pl.pallas_call(
        functools.partial(_even_proj_kernel, tm=tm, tiles_per_seq=seq // tm),
        grid=(n // tm,),
        in_specs=[row(d), _resident((1, d)), _resident(w.shape), _resident(wvt.shape),
                  _resident((CONV_WIDTH, SC_WIDTH))],
        out_specs=[row(SC_WIDTH), row(SWA_Q_WIDTH), row(SWA_KV_WIDTH),
                   pl.BlockSpec((SWA_KV_WIDTH, tm), lambda i: (0, i))],
        out_shape=[jax.ShapeDtypeStruct((n, SC_WIDTH), BF16),
                   jax.ShapeDtypeStruct((n, SWA_Q_WIDTH), BF16),
                   jax.ShapeDtypeStruct((n, SWA_KV_WIDTH), BF16),
                   jax.ShapeDtypeStruct((SWA_KV_WIDTH, n), BF16)],
        scratch_shapes=[pltpu.VMEM((tm + 8, SC_WIDTH), F32)],
        compiler_params=_params("arbitrary"),
        name="even_proj_conv",
    )(x, g, w, wvt, conv_w)


def _swa_kernel(sink_ref, q_ref, k_ref, kp_ref, vt_ref, vtp_ref, tab_ref, out_ref, *, nq):
    i = pl.program_id(1)
    blk = SWA_BLOCK
    hd = SWA_HEAD_DIM
    half = N_HEADS // SWA_KV_HEADS
    lane = lax.broadcasted_iota(jnp.int32, (blk, 2 * hd), 1)
    k_all = jnp.concatenate([kp_ref[...], k_ref[...]], axis=0)
    vt_all = jnp.concatenate([vtp_ref[...], vt_ref[...]], axis=1)

    scores = {}
    for s in range(nq):
        keys = k_all[s * blk:(s + 2) * blk]
        for j in range(half):
            q_pair = q_ref[s * blk:(s + 1) * blk, j * 2 * hd:(j + 1) * 2 * hd]
            scores[s, j] = _dot_nt(keys, jnp.where(lane < hd, q_pair, 0))
            scores[s, half + j] = _dot_nt(keys, jnp.where(lane >= hd, q_pair, 0))

    probs, denoms = {}, {}
    for s in range(nq):
        for h in range(N_HEADS):
            tab = tab_ref[h]
            if s == 0:
                tab = jnp.concatenate([jnp.where(i > 0, tab[:blk], NEG_INF), tab[blk:]], axis=0)
            sc = scores[s, h] + tab
            sink = sink_ref[h] * LOG2E
            m = jnp.maximum(jnp.max(sc, axis=0, keepdims=True), sink)
            p = jnp.exp2(sc - m)
            denoms[s, h] = jnp.sum(p, axis=0, keepdims=True) + jnp.exp2(sink - m)
            probs[s, h] = p.astype(BF16)

    for s in range(nq):
        outs = []
        for h in range(N_HEADS):
            g = h // half
            vt = vt_all[g * hd:(g + 1) * hd, s * blk:(s + 2) * blk]
            outs.append(_dot(vt, probs[s, h]) / denoms[s, h])
        out_ref[s * blk:(s + 1) * blk, :] = jnp.concatenate(outs, axis=0).T.astype(BF16)


def _swa(q, k, vt, sinks, tab, bsz, seq, nq=4):
    n = q.shape[0]
    nb = seq // SWA_BLOCK
    steps = nb // nq
    prev = lambda b, i: b * nb + jnp.maximum(i * nq - 1, 0)
    return pl.pallas_call(
        functools.partial(_swa_kernel, nq=nq),
        grid=(bsz, steps),
        in_specs=[pl.BlockSpec(memory_space=pltpu.SMEM),
                  pl.BlockSpec((nq * SWA_BLOCK, SWA_Q_WIDTH), lambda b, i: (b * steps + i, 0)),
                  pl.BlockSpec((nq * SWA_BLOCK, SWA_KV_WIDTH), lambda b, i: (b * steps + i, 0)),
                  pl.BlockSpec((SWA_BLOCK, SWA_KV_WIDTH), lambda b, i: (prev(b, i), 0)),
                  pl.BlockSpec((SWA_KV_WIDTH, nq * SWA_BLOCK), lambda b, i: (0, b * steps + i)),
                  pl.BlockSpec((SWA_KV_WIDTH, SWA_BLOCK), lambda b, i: (0, prev(b, i))),
                  _resident(tab.shape)],
        out_specs=pl.BlockSpec((nq * SWA_BLOCK, SWA_Q_WIDTH), lambda b, i: (b * steps + i, 0)),
        out_shape=jax.ShapeDtypeStruct((n, SWA_Q_WIDTH), BF16),
        compiler_params=_params("arbitrary", "arbitrary"),
        name="swa",
    )(sinks, q, k, k, vt, vt, tab)


def _ple(h, p, g, w_gate, w_proj):
    gate = jax.nn.sigmoid(_dot(_rms(h, g).astype(BF16), w_gate))
    return h + gate * _dot(p.astype(BF16), w_proj)


def _even_ffn_kernel(h_ref, a_ref, b_ref, p_ref, wo_ref, gf_ref, wg_ref, wu_ref, wd_ref,
                     gp_ref, wpg_ref, wpp_ref, out_ref, *, tf):
    h1 = (h_ref[...] + _dot(a_ref[...], wo_ref[0:SC_WIDTH, :])
          + _dot(b_ref[...], wo_ref[SC_WIDTH:, :]))
    xn = _rms(h1, gf_ref[...]).astype(BF16)
    h2 = h1 + _swiglu(xn, wg_ref, wu_ref, wd_ref, tf, jnp.zeros(h1.shape, F32))
    out_ref[...] = _ple(h2, p_ref[...], gp_ref[...], wpg_ref[...], wpp_ref[...])


def _layer_rows(p, layer, tm):
    return pl.BlockSpec((None, tm, p.shape[2]), lambda i: (layer, i, 0))


def _even_ffn(h, a, b, p, layer, wo, gf, wg, wu, wd, gp, wpg, wpp, tm=512, tf=256):
    n, d = h.shape
    row = lambda w: pl.BlockSpec((tm, w), lambda i: (i, 0))
    return pl.pallas_call(
        functools.partial(_even_ffn_kernel, tf=tf),
        grid=(n // tm,),
        in_specs=[row(d), row(a.shape[1]), row(b.shape[1]), _layer_rows(p, layer, tm),
                  _resident(wo.shape), _resident(gf.shape), _resident(wg.shape),
                  _resident(wu.shape), _resident(wd.shape), _resident(gp.shape),
                  _resident(wpg.shape), _resident(wpp.shape)],
        out_specs=row(d),
        out_shape=jax.ShapeDtypeStruct((n, d), F32),
        compiler_params=_params("arbitrary"),
        name="even_out_ffn_ple",
    )(h, a, b, p, wo, gf, wg, wu, wd, gp, wpg, wpp)


def _odd_proj_kernel(h_ref, g_ref, wq_ref, wk_ref, wvt_ref, q_ref, k_ref, vt_ref, kmean_ref, *, tm):
    xn = _rms(h_ref[...], g_ref[...]).astype(BF16)
    q_ref[...] = (_dot(xn, wq_ref[...]) * (MOBA_HEAD_DIM ** -0.5 * LOG2E)).astype(BF16)
    k = _dot(xn, wk_ref[...])
    k_ref[...] = k.astype(BF16)
    vt = _dot_nt(wvt_ref[...], xn).astype(BF16)
    ones = jnp.ones((MOBA_V_ROWS - MOBA_HEAD_DIM, tm), BF16)
    for hh in range(N_HEADS):
        vt_ref[hh * MOBA_V_ROWS:hh * MOBA_V_ROWS + MOBA_HEAD_DIM, :] = (
            vt[hh * MOBA_HEAD_DIM:(hh + 1) * MOBA_HEAD_DIM])
        vt_ref[hh * MOBA_V_ROWS + MOBA_HEAD_DIM:(hh + 1) * MOBA_V_ROWS, :] = ones
    for r in range(tm // MOBA_BLOCK):
        kmean_ref[r] = jnp.mean(k[r * MOBA_BLOCK:(r + 1) * MOBA_BLOCK, :], axis=0, keepdims=True)


def _odd_proj(h, g, wq, wk, wvt, tm=512):
    n, d = h.shape
    bpt = tm // MOBA_BLOCK
    row = pl.BlockSpec((tm, d), lambda i: (i, 0))
    return pl.pallas_call(
        functools.partial(_odd_proj_kernel, tm=tm),
        grid=(n // tm,),
        in_specs=[row, _resident(g.shape), _resident(wq.shape), _resident(wk.shape),
                  _resident(wvt.shape)],
        out_specs=[row, row, pl.BlockSpec((N_HEADS * MOBA_V_ROWS, tm), lambda i: (0, i)),
                   pl.BlockSpec((bpt, 1, d), lambda i: (i, 0, 0))],
        out_shape=[jax.ShapeDtypeStruct((n, d), BF16), jax.ShapeDtypeStruct((n, d), BF16),
                   jax.ShapeDtypeStruct((N_HEADS * MOBA_V_ROWS, n), BF16),
                   jax.ShapeDtypeStruct((n // MOBA_BLOCK, 1, d), F32)],
        compiler_params=_params("arbitrary"),
        name="odd_qkv",
    )(h, g, wq, wk, wvt)


def _moba_kernel(rb_ref, q_ref, k_ref, vt_ref, km_ref, tab_ref, out_ref,
                 sel_scr, adj_scr, m_scr, acc_scr, far_a_scr, far_b_scr, gmax_scr, *, nblk, hg):
    head0 = pl.program_id(1) * hg
    qi = pl.program_id(2)
    bs = MOBA_BLOCK
    hd = MOBA_HEAD_DIM
    heads = [slice(hh * hd, (hh + 1) * hd) for hh in range(hg)]
    head_values = [slice(hh * MOBA_V_ROWS, (hh + 1) * MOBA_V_ROWS) for hh in range(hg)]

    def select_blocks():
        blk = lax.broadcasted_iota(jnp.int32, (nblk, bs), 0)
        past = blk < qi
        for hh in range(hg):
            km = km_ref[:, heads[hh]]
            km_hi = km.astype(BF16)
            km_mid = (km - km_hi.astype(F32)).astype(BF16)
            km_lo = (km - km_hi.astype(F32) - km_mid.astype(F32)).astype(BF16)
            q = q_ref[:, heads[hh]]
            gate = _dot_nt(km_hi, q) + (_dot_nt(km_mid, q) + _dot_nt(km_lo, q))
            g = jnp.where(past, gate, NEG_INF)
            sel = jnp.zeros(gate.shape, jnp.bool_)
            for _ in range(MOBA_TOPK):
                top = jnp.max(g, axis=0, keepdims=True)
                idx = jnp.min(jnp.where(g == top, blk, nblk), axis=0, keepdims=True)
                hit = blk == idx
                sel = sel | hit
                g = jnp.where(hit, NEG_INF, g)
            far_bias = rb_ref[N_BUCKETS - 1, head0 + hh] * LOG2E
            sel_scr[hh] = jnp.where(sel & (blk < qi - 1), far_bias, NEG_INF)
            adj_scr[hh] = jnp.max(jnp.where(sel & (blk == qi - 1), 0.0, NEG_INF), axis=0,
                                  keepdims=True)

    def head_chunks():
        return [range(h0, min(h0 + MOBA_HEAD_CHUNK, hg)) for h0 in range(0, hg, MOBA_HEAD_CHUNK)]

    def near_update(start, nb, tables, query_biases, between=lambda hh: None):
        def bias(hh, r):
            return query_biases(hh, r)

        for hh in range(hg):
            s = _dot_nt(k_ref[pl.ds(start, nb * bs), heads[hh]], q_ref[:, heads[hh]])
            between(hh)
            m_new = None
            for r in range(nb):
                s_r = s[r * bs:(r + 1) * bs] + tables(hh, r)
                far_b_scr[hh, r * bs:(r + 1) * bs, :] = s_r
                col_max = jnp.max(s_r, axis=0, keepdims=True)
                if bias(hh, r) is not None:
                    col_max = col_max + bias(hh, r)
                m_new = col_max if m_new is None else jnp.maximum(m_new, col_max)
            m_scr[hh] = m_new
        for hh in range(hg):
            m_new = m_scr[hh]
            p = jnp.concatenate(
                [jnp.exp2(far_b_scr[hh, r * bs:(r + 1) * bs, :]
                          - (m_new if bias(hh, r) is None else m_new - bias(hh, r)))
                 for r in range(nb)], axis=0)
            acc_scr[hh] = _dot(vt_ref[head_values[hh], pl.ds(start, nb * bs)], p.astype(BF16))

    @pl.when(qi == 0)
    def _():
        near_update(0, 1, lambda hh, r: tab_ref[hh, :, bs:2 * bs], lambda hh, r: None)

    far_rows = MOBA_FAR_GROUP * bs
    n_far = (qi - 1 + MOBA_FAR_GROUP - 1) // MOBA_FAR_GROUP
    far_bufs = (far_a_scr, far_b_scr)

    def far_start(g):
        return pl.multiple_of(g * far_rows, far_rows)

    def far_bias(hh, g, r):
        return sel_scr[hh, pl.ds(g * MOBA_FAR_GROUP + r, 1), :]

    def far_scores(g, buf, head_ids):
        keys = k_ref.at[pl.ds(far_start(g), far_rows)]
        scores = {hh: _dot_nt(keys[:, heads[hh]], q_ref[:, heads[hh]]) for hh in head_ids}
        for hh in head_ids:
            far_bufs[buf][hh] = scores[hh]
            group_max = None
            for r in range(MOBA_FAR_GROUP):
                col_max = (jnp.max(scores[hh][r * bs:(r + 1) * bs], axis=0, keepdims=True)
                           + far_bias(hh, g, r))
                group_max = col_max if group_max is None else jnp.maximum(group_max, col_max)
            gmax_scr[buf, hh] = group_max

    def far_softmax(g, buf, head_ids):
        probs, alphas = {}, {}
        for hh in head_ids:
            m_old = m_scr[hh]
            m_new = jnp.maximum(m_old, gmax_scr[buf, hh])
            alphas[hh] = jnp.exp2(m_old - m_new)
            p = jnp.concatenate(
                [jnp.exp2(far_bufs[buf][hh, r * bs:(r + 1) * bs, :] - (m_new - far_bias(hh, g, r)))
                 for r in range(MOBA_FAR_GROUP)], axis=0)
            m_scr[hh] = m_new
            probs[hh] = p.astype(BF16)
        for hh in head_ids:
            acc_scr[hh] = (alphas[hh] * acc_scr[hh]
                           + _dot(vt_ref[head_values[hh], pl.ds(far_start(g), far_rows)], probs[hh]))

    def far_step(g, buf, prefetch):
        for head_ids in head_chunks():
            if prefetch:
                far_scores(g + 1, 1 - buf, head_ids)
            far_softmax(g, buf, head_ids)

    @pl.when(qi > 0)
    def _():
        select_blocks()
        start = pl.multiple_of((qi - 1) * bs, bs)
        near_update(start, 2, lambda hh, r: tab_ref[hh, :, r * bs:(r + 1) * bs],
                    lambda hh, r: adj_scr[hh] if r == 0 else None,
                    between=lambda hh: far_scores(0, 0, [hh]))

    def far_pair(i, carry):
        far_step(2 * i, 0, True)
        far_step(2 * i + 1, 1, True)
        return carry

    lax.fori_loop(0, (n_far - 1) // 2, far_pair, 0)

    @pl.when((n_far > 0) & (n_far % 2 == 0))
    def _():
        far_step(n_far - 2, 0, True)
        far_step(n_far - 1, 1, False)

    @pl.when(n_far % 2 == 1)
    def _():
        far_step(n_far - 1, 0, False)

    for hh in range(hg):
        out_ref[:, heads[hh]] = (acc_scr[hh, 0:hd, :] / acc_scr[hh, hd:hd + 1, :]).T.astype(BF16)


def _moba(q, k, vt, kmean, tab, rel_bias, bsz, seq, hg=N_HEADS):
    n, d = q.shape
    nblk = seq // MOBA_BLOCK
    assert nblk % MOBA_FAR_GROUP == 0, "far-block groups must not run past the sequence"
    assert MOBA_FAR_GROUP >= 2, "the far score buffer also parks the two near blocks"
    bs = MOBA_BLOCK
    w = hg * MOBA_HEAD_DIM
    return pl.pallas_call(
        functools.partial(_moba_kernel, nblk=nblk, hg=hg),
        grid=(bsz, N_HEADS // hg, nblk),
        in_specs=[pl.BlockSpec(memory_space=pltpu.SMEM),
                  pl.BlockSpec((bs, w), lambda b, g, i: (b * nblk + i, g)),
                  pl.BlockSpec((seq, w), lambda b, g, i: (b, g), pipeline_mode=pl.Buffered(1)),
                  pl.BlockSpec((hg * MOBA_V_ROWS, seq), lambda b, g, i: (g, b),
                               pipeline_mode=pl.Buffered(1)),
                  pl.BlockSpec((None, nblk, w), lambda b, g, i: (b, 0, g)),
                  pl.BlockSpec((hg, bs, 2 * bs), lambda b, g, i: (g, 0, 0),
                               pipeline_mode=pl.Buffered(1))],
        out_specs=pl.BlockSpec((bs, w), lambda b, g, i: (b * nblk + i, g)),
        out_shape=jax.ShapeDtypeStruct((n, d), BF16),
        scratch_shapes=[pltpu.VMEM((hg, nblk, bs), F32), pltpu.VMEM((hg, 1, bs), F32),
                        pltpu.VMEM((hg, 1, bs), F32), pltpu.VMEM((hg, MOBA_V_ROWS, bs), F32),
                        pltpu.VMEM((hg, MOBA_FAR_GROUP * bs, bs), F32),
                        pltpu.VMEM((hg, MOBA_FAR_GROUP * bs, bs), F32),
                        pltpu.VMEM((2, hg, 1, bs), F32)],
        compiler_params=_params("arbitrary", "arbitrary", "arbitrary"),
        name="moba",
    )(rel_bias, q, k, vt, kmean.reshape(bsz, nblk, d), tab)


def _odd_router_kernel(h_ref, c_ref, wo_ref, g_ref, wr_ref, h_out_ref, xn_ref, gates_ref, sel_ref,
                       cnt_ref):
    wr = wr_ref[...]
    wr_hi = wr.astype(BF16)
    wr_lo = (wr - wr_hi.astype(F32)).astype(BF16)
    n_parts = 2
    part = h_ref.shape[0] // n_parts
    rows = [slice(k * part, (k + 1) * part) for k in range(n_parts)]
    h1 = [h_ref[r, :] + _dot(c_ref[r, :], wo_ref[...]) for r in rows]
    xn = [_rms(h, g_ref[...]) for h in h1]
    xn_hi = [x.astype(BF16) for x in xn]
    xn_lo = [(x - hi.astype(F32)).astype(BF16) for x, hi in zip(xn, xn_hi)]
    logits = [_dot(hi, wr_hi) + (_dot(lo, wr_hi) + _dot(hi, wr_lo)) for hi, lo in zip(xn_hi, xn_lo)]
    count = jnp.zeros((1, N_EXPERTS), F32)
    for k, r in enumerate(rows):
        h_out_ref[r, :] = h1[k]
        xn_ref[r, :] = xn_hi[k]
        lane = lax.broadcasted_iota(jnp.int32, logits[k].shape, 1)
        v1 = jnp.max(logits[k], axis=-1, keepdims=True)
        i1 = jnp.min(jnp.where(logits[k] == v1, lane, N_EXPERTS), axis=-1, keepdims=True)
        rest = jnp.where(lane == i1, NEG_INF, logits[k])
        v2 = jnp.max(rest, axis=-1, keepdims=True)
        i2 = jnp.min(jnp.where(rest == v2, lane, N_EXPERTS), axis=-1, keepdims=True)
        e2 = jnp.exp(v2 - v1)
        w1 = 1.0 / (1.0 + e2)
        w2 = e2 / (1.0 + e2)
        gates_ref[r, :] = jnp.where(lane == i1, w1, 0.0) + jnp.where(lane == i2, w2, 0.0)
        sel = jnp.where((lane == i1) | (lane == i2), 1.0, 0.0)
        sel_ref[r, :] = sel
        count = count + jnp.sum(sel, axis=0, keepdims=True)
    cnt_ref[0] = count


def _odd_router(h, c, wo, g, wr, tm=MOE_TOKEN_TILE):
    n, d = h.shape
    row = lambda w: pl.BlockSpec((tm, w), lambda i: (i, 0))
    return pl.pallas_call(
        _odd_router_kernel,
        grid=(n // tm,),
        in_specs=[row(d), row(d), _resident(wo.shape), _resident(g.shape), _resident(wr.shape)],
        out_specs=[row(d), row(d), row(N_EXPERTS), row(N_EXPERTS),
                   pl.BlockSpec((1, 1, N_EXPERTS), lambda i: (i, 0, 0))],
        out_shape=[jax.ShapeDtypeStruct((n, d), F32), jax.ShapeDtypeStruct((n, d), BF16),
                   jax.ShapeDtypeStruct((n, N_EXPERTS), F32),
                   jax.ShapeDtypeStruct((n, N_EXPERTS), F32),
                   jax.ShapeDtypeStruct((n // tm, 1, N_EXPERTS), F32)],
        compiler_params=_params("arbitrary"),
        name="odd_out_router",
    )(h, c, wo, g, wr)


def _moe_plan(counts, n_slot_tiles):
    cnt = counts.reshape(-1, N_EXPERTS).astype(jnp.int32)
    grp = (cnt + MOE_ALIGN - 1) // MOE_ALIGN * MOE_ALIGN
    local = jnp.cumsum(grp, axis=1) - grp
    tot = jnp.sum(grp, axis=0)
    region = (tot + MOE_SLOT_TILE - 1) // MOE_SLOT_TILE * MOE_SLOT_TILE
    region_end = jnp.cumsum(region)
    region_start = region_end - region
    start = region_start[None, :] + jnp.cumsum(grp, axis=0) - grp
    n_used = region_end[-1] // MOE_SLOT_TILE
    tile_row = jnp.arange(n_slot_tiles, dtype=jnp.int32) * MOE_SLOT_TILE
    tile_expert = jnp.sum((tile_row[:, None] >= region_end[None, :]).astype(jnp.int32), axis=1)
    tile_expert = jnp.minimum(tile_expert, N_EXPERTS - 1)
    tile_expert = jnp.where(tile_row < region_end[-1], tile_expert, tile_expert[n_used - 1])
    return dict(cnt=cnt, local=local.astype(jnp.int32), start=start.astype(jnp.int32),
                tail_start=(region_start + tot).astype(jnp.int32),
                tail_len=(region - tot).astype(jnp.int32), tile_expert=tile_expert,
                n_used=n_used.reshape(1).astype(jnp.int32))


def _group_copies(cnt_ref, local_ref, start_ref, t, local_buf, slot_array, sem, to_slots):
    copies = []
    for e in range(N_EXPERTS):
        rows = (cnt_ref[t, e] + MOE_ALIGN - 1) // MOE_ALIGN * MOE_ALIGN
        size = MOE_TOKEN_TILE
        while size >= MOE_ALIGN:
            off = rows & ~(2 * size - 1)
            local = local_buf.at[pl.ds(pl.multiple_of(local_ref[t, e] + off, MOE_ALIGN), size)]
            slots = slot_array.at[pl.ds(pl.multiple_of(start_ref[t, e] + off, MOE_ALIGN), size)]
            src, dst = (local, slots) if to_slots else (slots, local)
            copies.append(((rows & size) != 0, pltpu.make_async_copy(src, dst, sem)))
            size //= 2
    return copies


def _start_all(copies):
    for cond, cp in copies:
        pl.when(cond)(cp.start)


def _wait_all(copies):
    for cond, cp in copies:
        pl.when(cond)(cp.wait)


def _moe_dispatch_kernel(cnt_ref, local_ref, start_ref, tail_start_ref, tail_len_ref, x_ref, sel_ref,
                         upper_ref, xs_ref, stage_ref, zero_ref, sem, tail_sem):
    t = pl.program_id(0)
    last = pl.num_programs(0) - 1
    tile = x_ref.shape[0]
    slot = t % 2
    eye = jnp.where(lax.broadcasted_iota(jnp.int32, (N_EXPERTS, N_EXPERTS), 0)
                    == lax.broadcasted_iota(jnp.int32, (N_EXPERTS, N_EXPERTS), 1), 1.0, 0.0)
    sel_t = _dot_nt(eye.astype(BF16), sel_ref[...].astype(BF16))
    rank_t = _dot(sel_t.astype(BF16), upper_ref[...])
    expert = lax.broadcasted_iota(jnp.int32, (N_EXPERTS, 1), 0)
    base = jnp.zeros((N_EXPERTS, 1), F32)
    for e in range(N_EXPERTS):
        base = jnp.where(expert == e, local_ref[t, e].astype(F32), base)
    pos = base + rank_t
    pos_lo = jnp.min(jnp.where(sel_t > 0.5, pos, float(MOE_LOCAL_ROWS)), axis=0, keepdims=True)
    pos_hi = jnp.max(jnp.where(sel_t > 0.5, pos, -1.0), axis=0, keepdims=True)
    row = lax.broadcasted_iota(jnp.int32, (MOE_LOCAL_ROWS, tile), 0).astype(F32)
    onehot = jnp.where(row == pos_lo, 1.0, jnp.where(row == pos_hi, 1.0, 0.0)).astype(BF16)
    stage_ref[slot] = _dot(onehot, x_ref[...]).astype(BF16)

    def copies(step):
        return _group_copies(cnt_ref, local_ref, start_ref, step, stage_ref.at[step % 2], xs_ref,
                             sem.at[step % 2], to_slots=True)

    _start_all(copies(t))

    @pl.when(t > 0)
    def _():
        _wait_all(copies(t - 1))

    @pl.when(t == last)
    def _():
        _wait_all(copies(t))
        zero_ref[...] = jnp.zeros(zero_ref.shape, BF16)
        tails = []
        for e in range(N_EXPERTS):
            size = MOE_SLOT_TILE // 2
            while size >= MOE_ALIGN:
                off = tail_len_ref[e] & ~(2 * size - 1)
                dst = pl.multiple_of(tail_start_ref[e] + off, MOE_ALIGN)
                tails.append(((tail_len_ref[e] & size) != 0, pltpu.make_async_copy(
                    zero_ref.at[pl.ds(0, size)], xs_ref.at[pl.ds(dst, size)], tail_sem)))
                size //= 2
        _start_all(tails)
        _wait_all(tails)

        used_rows = tail_start_ref[N_EXPERTS - 1] + tail_len_ref[N_EXPERTS - 1]
        piece = zero_ref.shape[0]

        def unused_copy(k):
            dst = pl.multiple_of(used_rows + k * piece, piece)
            return pltpu.make_async_copy(zero_ref, xs_ref.at[pl.ds(dst, piece)], tail_sem)

        n_pieces = (xs_ref.shape[0] - used_rows) // piece
        lax.fori_loop(0, n_pieces, lambda k, c: (unused_copy(k).start(), c)[1], 0)
        lax.fori_loop(0, n_pieces, lambda k, c: (unused_copy(k).wait(), c)[1], 0)


def _moe_dispatch(xn, sel, plan, n_slot_tiles, tm=MOE_TOKEN_TILE):
    n, d = xn.shape
    upper = jnp.asarray(np.triu(np.ones((tm, tm), np.float32), 1), BF16)
    smem = pl.BlockSpec(memory_space=pltpu.SMEM)
    return pl.pallas_call(
        _moe_dispatch_kernel,
        grid=(n // tm,),
        in_specs=[smem, smem, smem, smem, smem,
                  pl.BlockSpec((tm, d), lambda i: (i, 0)),
                  pl.BlockSpec((tm, N_EXPERTS), lambda i: (i, 0)),
                  _resident((tm, tm))],
        out_specs=pl.BlockSpec(memory_space=pl.ANY),
        out_shape=jax.ShapeDtypeStruct((n_slot_tiles * MOE_SLOT_TILE, d), BF16),
        scratch_shapes=[pltpu.VMEM((2, MOE_LOCAL_ROWS, d), BF16),
                        pltpu.VMEM((MOE_SLOT_TILE // 2, d), BF16),
                        pltpu.SemaphoreType.DMA((2,)), pltpu.SemaphoreType.DMA(())],
        compiler_params=_params("arbitrary"),
        name="moe_dispatch",
    )(plan["cnt"], plan["local"], plan["start"], plan["tail_start"], plan["tail_len"], xn, sel, upper)


def _moe_experts_kernel(te_ref, nu_ref, x_ref, wg_ref, wu_ref, wd_ref, y_ref, acc_ref):
    i = pl.program_id(0)
    f = pl.program_id(1)
    last = pl.num_programs(1) - 1
    used = i < nu_ref[0]

    @pl.when(used & (f == 0))
    def _():
        acc_ref[...] = jnp.zeros(acc_ref.shape, F32)

    @pl.when(used)
    def _():
        x = x_ref[...]
        acc_ref[...] = _swiglu(x, wg_ref, wu_ref, wd_ref, MOE_FF_CHUNK, acc_ref[...])

    @pl.when(used & (f == last))
    def _():
        y_ref[...] = acc_ref[...].astype(BF16)

    @pl.when(jnp.logical_not(used) & (f == last))
    def _():
        y_ref[...] = jnp.zeros(y_ref.shape, BF16)


def _moe_experts(xs, plan, wg, wu, wd, tf=1792):
    rows, d = xs.shape
    d_ff = wg.shape[2]
    nf = d_ff // tf
    tm = MOE_SLOT_TILE
    f_idx = lambda i, f, nu: jnp.where(i < nu[0], f, nf - 1)
    return pl.pallas_call(
        _moe_experts_kernel,
        grid_spec=pltpu.PrefetchScalarGridSpec(
            num_scalar_prefetch=2,
            grid=(rows // tm, nf),
            in_specs=[pl.BlockSpec((tm, d), lambda i, f, te, nu: (jnp.minimum(i, nu[0] - 1), 0)),
                      pl.BlockSpec((None, d, tf), lambda i, f, te, nu: (te[i], 0, f_idx(i, f, nu))),
                      pl.BlockSpec((None, d, tf), lambda i, f, te, nu: (te[i], 0, f_idx(i, f, nu))),
                      pl.BlockSpec((None, tf, d), lambda i, f, te, nu: (te[i], f_idx(i, f, nu), 0))],
            out_specs=pl.BlockSpec((tm, d), lambda i, f, te, nu: (i, 0)),
            scratch_shapes=[pltpu.VMEM((tm, d), F32)]),
        out_shape=jax.ShapeDtypeStruct((rows, d), BF16),
        compiler_params=_params("arbitrary", "arbitrary"),
        name="moe_experts",
    )(plan["tile_expert"], plan["n_used"], xs, wg, wu, wd)


def _moe_combine_kernel(cnt_ref, local_ref, start_ref, h_ref, gates_ref, sel_ref, lower_ref, p_ref,
                        gp_ref, wpg_ref, wpp_ref, gfin_ref, ys_ref, out_ref, ybuf_ref, sem):
    t = pl.program_id(0)
    tile = h_ref.shape[0]

    def copies(step):
        return _group_copies(cnt_ref, local_ref, start_ref, step, ybuf_ref.at[step % 2], ys_ref,
                             sem.at[step % 2], to_slots=False)

    @pl.when(t == 0)
    def _():
        ybuf_ref[...] = jnp.zeros(ybuf_ref.shape, BF16)
        _start_all(copies(t))

    @pl.when(t + 1 < pl.num_programs(0))
    def _():
        _start_all(copies(t + 1))

    sel = sel_ref[...] > 0.5
    gates = gates_ref[...]
    rank = _dot(lower_ref[...], sel_ref[...].astype(BF16))
    expert = lax.broadcasted_iota(jnp.int32, (1, N_EXPERTS), 1)
    base = jnp.zeros((1, N_EXPERTS), F32)
    for e in range(N_EXPERTS):
        base = jnp.where(expert == e, local_ref[t, e].astype(F32), base)
    pos = base + rank
    pos_lo = jnp.min(jnp.where(sel, pos, float(MOE_LOCAL_ROWS)), axis=1, keepdims=True)
    pos_hi = jnp.max(jnp.where(sel, pos, -1.0), axis=1, keepdims=True)
    gate_lo = jnp.sum(jnp.where(sel & (pos == pos_lo), gates, 0.0), axis=1, keepdims=True)
    gate_hi = jnp.sum(jnp.where(sel & (pos == pos_hi), gates, 0.0), axis=1, keepdims=True)
    col = lax.broadcasted_iota(jnp.int32, (tile, MOE_LOCAL_ROWS), 1).astype(F32)
    onehot_lo = jnp.where(col == pos_lo, 1.0, 0.0).astype(BF16)
    onehot_hi = jnp.where(col == pos_hi, 1.0, 0.0).astype(BF16)

    _wait_all(copies(t))
    y = ybuf_ref[t % 2]
    h = h_ref[...] + gate_lo * _dot(onehot_lo, y) + gate_hi * _dot(onehot_hi, y)
    h = _ple(h, p_ref[...], gp_ref[...], wpg_ref[...], wpp_ref[...])
    out_ref[...] = _rms(h, gfin_ref[...])


def _moe_combine(h, gates, sel, ys, plan, p, layer, gp, wpg, wpp, gfin, tm=MOE_TOKEN_TILE):
    n, d = h.shape
    lower = jnp.asarray(np.tril(np.ones((tm, tm), np.float32), -1), BF16)
    smem = pl.BlockSpec(memory_space=pltpu.SMEM)
    row = lambda w: pl.BlockSpec((tm, w), lambda i: (i, 0))
    return pl.pallas_call(
        _moe_combine_kernel,
        grid=(n // tm,),
        in_specs=[smem, smem, smem, row(d), row(N_EXPERTS), row(N_EXPERTS), _resident((tm, tm)),
                  _layer_rows(p, layer, tm), _resident(gp.shape), _resident(wpg.shape),
                  _resident(wpp.shape), _resident(gfin.shape), pl.BlockSpec(memory_space=pl.ANY)],
        out_specs=row(d),
        out_shape=jax.ShapeDtypeStruct((n, d), F32),
        scratch_shapes=[pltpu.VMEM((2, MOE_LOCAL_ROWS, d), BF16), pltpu.SemaphoreType.DMA((2,))],
        compiler_params=_params("arbitrary"),
        name="moe_combine_ple_norm",
    )(plan["cnt"], plan["local"], plan["start"], h, gates, sel, lower, p, gp, wpg, wpp, gfin, ys)


def kernel(x, p, rel_bias, final_norm, e_norm_mix, e_w_in, e_conv_w, e_sinks, e_w_out, e_norm_ffn, e_ffn_gate, e_ffn_up, e_ffn_down, e_norm_ple, e_ple_gate, e_ple_proj, o_norm_mix, o_w_qkv, o_w_o, o_norm_ffn, o_router, o_exp_gate, o_exp_up, o_exp_down, o_norm_ple, o_ple_gate, o_ple_proj):
    bsz, seq, d = x.shape
    n = bsz * seq
    bf = lambda w: w.astype(BF16)
    h = x.reshape(n, d)
    p2 = p.reshape(p.shape[0], n, p.shape[-1])

    kk = np.arange(2 * SWA_BLOCK)[:, None]
    qq = np.arange(SWA_BLOCK)[None, :]
    dist = qq - kk + SWA_BLOCK
    swa_bucket = np.where((dist >= 0) & (dist < SWA_BLOCK), _t5_bucket_np(dist), -1)
    swa_tab = _bias_table(rel_bias, swa_bucket, mult=LOG2E)
    kk = np.arange(MOBA_BLOCK)[:, None]
    qq = np.arange(MOBA_BLOCK)[None, :]
    own_bucket = np.where(qq >= kk, _t5_bucket_np(qq - kk), -1)
    moba_bucket = np.concatenate([_t5_bucket_np(qq - kk + MOBA_BLOCK), own_bucket], axis=1)
    moba_tab = _bias_table(rel_bias, moba_bucket, mult=LOG2E)

    w_in = e_w_in[0]
    q0 = 3 * SC_WIDTH
    k0 = q0 + SWA_Q_WIDTH
    v0 = k0 + SWA_KV_WIDTH
    half = N_HEADS // SWA_KV_HEADS
    q_cols = np.concatenate([np.arange(h * SWA_HEAD_DIM, (h + 1) * SWA_HEAD_DIM)
                             for j in range(half) for h in (j, half + j)])
    w_main = jnp.concatenate([w_in[:, :q0], w_in[:, q0:k0][:, q_cols], w_in[:, k0:v0]], axis=1)
    a_out, q0_, k0_, vt0 = _even_proj(h, e_norm_mix[0:1], bf(w_main), bf(w_in[:, v0:].T),
                                      e_conv_w[0], seq)
    b_out = _swa(q0_, k0_, vt0, e_sinks[0], swa_tab, bsz, seq)
    h = _even_ffn(h, a_out, b_out, p2, 0, bf(e_w_out[0]), e_norm_ffn[0:1],
                  bf(e_ffn_gate[0]), bf(e_ffn_up[0]), bf(e_ffn_down[0]),
                  e_norm_ple[0:1], bf(e_ple_gate[0]), bf(e_ple_proj[0]))

    w_qkv = o_w_qkv[0]
    q1, k1, vt1, kmean = _odd_proj(h, o_norm_mix[0:1], bf(w_qkv[:, :d]), bf(w_qkv[:, d:2 * d]),
                                   bf(w_qkv[:, 2 * d:].T))
    c_out = _moba(q1, k1, vt1, kmean, moba_tab, rel_bias, bsz, seq)
    h, xn, gates, sel, counts = _odd_router(h, c_out, bf(o_w_o[0]), o_norm_ffn[0:1], o_router[0])
    n_groups = (n // MOE_TOKEN_TILE) * N_EXPERTS
    max_rows = 2 * n + n_groups * (MOE_ALIGN - 1) + N_EXPERTS * (MOE_SLOT_TILE - 1)
    n_slot_tiles = -(-max_rows // MOE_SLOT_TILE) + 1
    plan = _moe_plan(counts, n_slot_tiles)
    xs = _moe_dispatch(xn, sel, plan, n_slot_tiles)
    ys = _moe_experts(xs, plan, bf(o_exp_gate[0]), bf(o_exp_up[0]), bf(o_exp_down[0]))
    out = _moe_combine(h, gates, sel, ys, plan, p2, 1, o_norm_ple[0:1], bf(o_ple_gate[0]),
                       bf(o_ple_proj[0]), final_norm.reshape(1, d))
    return out.reshape(bsz, seq, d)
```

```python
import functools
import math

import jax
import jax.numpy as jnp
import numpy as np
from jax import lax
from jax.experimental import pallas as pl
from jax.experimental.pallas import tpu as pltpu

F32 = jnp.float32
BF16 = jnp.bfloat16

EPS = 1e-6
N_HEADS = 8
N_BUCKETS = 32
MAX_DISTANCE = 128

SC_WIDTH = 512
CONV_WIDTH = 3
SWA_KV_HEADS = 2
SWA_HEAD_DIM = 64
SWA_BLOCK = 128
SWA_Q_WIDTH = N_HEADS * SWA_HEAD_DIM
SWA_KV_WIDTH = SWA_KV_HEADS * SWA_HEAD_DIM

MOBA_HEAD_DIM = 128
MOBA_BLOCK = 256
MOBA_TOPK = 3
MOBA_FAR_GROUP = 2
MOBA_HEAD_CHUNK = 1
MOBA_V_ROWS = MOBA_HEAD_DIM + 16

N_EXPERTS = 8
MOE_TOKEN_TILE = 512
MOE_SLOT_TILE = 512
MOE_ALIGN = 16
MOE_FF_CHUNK = 256
MOE_LOCAL_ROWS = -(-(2 * MOE_TOKEN_TILE + N_EXPERTS * (MOE_ALIGN - 1)) // 128) * 128

VMEM_LIMIT_BYTES = 56 * 1024 * 1024
NEG_INF = float("-inf")
LOG2E = math.log2(math.e)


def _params(*semantics):
    return pltpu.CompilerParams(dimension_semantics=semantics,
                                vmem_limit_bytes=VMEM_LIMIT_BYTES)


def _resident(shape):
    zeros = (0,) * len(shape)
    return pl.BlockSpec(shape, lambda *_: zeros, pipeline_mode=pl.Buffered(1))


def _rms(x, g):
    return x * lax.rsqrt(jnp.mean(x * x, axis=-1, keepdims=True) + EPS) * g


def _dot(a, b):
    return jnp.dot(a, b, preferred_element_type=F32)


def _swiglu(x, wg_ref, wu_ref, wd_ref, chunk, acc):
    for c in range(wg_ref.shape[1] // chunk):
        cols = slice(c * chunk, (c + 1) * chunk)
        gate = _dot(x, wg_ref[:, cols])
        up = _dot(x, wu_ref[:, cols])
        hid = (gate * jax.nn.sigmoid(gate) * up).astype(BF16)
        acc = acc + _dot(hid, wd_ref[cols, :])
    return acc


def _dot_nt(a, b):
    return lax.dot_general(a, b, (((1,), (1,)), ((), ())), preferred_element_type=F32)


def _t5_bucket_np(dist):
    d = np.maximum(dist, 0).astype(np.int32)
    max_exact = N_BUCKETS // 2
    scaled = (np.log(np.maximum(d, max_exact).astype(np.float32) / np.float32(max_exact))
              / np.float32(math.log(MAX_DISTANCE / max_exact)))
    large = np.minimum(max_exact + (scaled * (N_BUCKETS - max_exact)).astype(np.int32),
                       N_BUCKETS - 1)
    return np.where(d < max_exact, d, large).astype(np.int32)


def _bias_table_kernel(rb_ref, bkt_ref, out_ref, *, mult):
    h = pl.program_id(0)
    bkt = bkt_ref[...]
    acc = jnp.full(bkt.shape, NEG_INF, F32)
    for b in range(N_BUCKETS):
        acc = jnp.where(bkt == b, rb_ref[b, h], acc)
    out_ref[...] = acc * mult


def _bias_table(rel_bias, bucket, mult=1.0):
    rows, cols = bucket.shape
    return pl.pallas_call(
        functools.partial(_bias_table_kernel, mult=mult),
        grid=(N_HEADS,),
        in_specs=[pl.BlockSpec(memory_space=pltpu.SMEM),
                  pl.BlockSpec((rows, cols), lambda h: (0, 0))],
        out_specs=pl.BlockSpec((None, rows, cols), lambda h: (h, 0, 0)),
        out_shape=jax.ShapeDtypeStruct((N_HEADS, rows, cols), F32),
        compiler_params=_params("arbitrary"),
        name="bias_table",
    )(rel_bias, jnp.asarray(bucket))


def _even_proj_kernel(x_ref, g_ref, w_ref, wvt_ref, cw_ref, a_ref, q_ref, k_ref, vt_ref, cu_scr, *,
                      tm, tiles_per_seq):
    i = pl.program_id(0)
    cu_scr[0:8, :] = jnp.where(i % tiles_per_seq == 0, 0.0, cu_scr[tm:tm + 8, :])
    xn = _rms(x_ref[...], g_ref[...]).astype(BF16)
    q0 = 3 * SC_WIDTH
    k0 = q0 + SWA_Q_WIDTH
    c_gate = _dot(xn, w_ref[:, SC_WIDTH:2 * SC_WIDTH])
    u = _dot(xn, w_ref[:, 2 * SC_WIDTH:3 * SC_WIDTH])
    q = _dot(xn, w_ref[:, q0:k0])
    k = _dot(xn, w_ref[:, k0:])
    vt = _dot_nt(wvt_ref[...], xn)
    b_gate = _dot(xn, w_ref[:, 0:SC_WIDTH])
    cu = c_gate * u
    cu_scr[8:8 + tm, :] = cu
    y = (cw_ref[0:1, :] * cu_scr[6:6 + tm, :] + cw_ref[1:2, :] * cu_scr[7:7 + tm, :]
         + cw_ref[2:3, :] * cu)
    q_ref[...] = (q * (SWA_HEAD_DIM ** -0.5 * LOG2E)).astype(BF16)
    k_ref[...] = k.astype(BF16)
    vt_ref[...] = vt.astype(BF16)
    a_ref[...] = (b_gate * y).astype(BF16)


def _even_proj(x, g, w, wvt, conv_w, seq, tm=512):
    n, d = x.shape
    row = lambda width: pl.BlockSpec((tm, width), lambda i: (i, 0))
    return pl.pallas_call(
        functools.partial(_even_proj_kernel, tm=tm, tiles_per_seq=seq // tm),
        grid=(n // tm,),
        in_specs=[row(d), _resident((1, d)), _resident(w.shape), _resident(wvt.shape),
                  _resident((CONV_WIDTH, SC_WIDTH))],
        out_specs=[row(SC_WIDTH), row(SWA_Q_WIDTH), row(SWA_KV_WIDTH),
                   pl.BlockSpec((SWA_KV_WIDTH, tm), lambda i: (0, i))],
        out_shape=[jax.ShapeDtypeStruct((n, SC_WIDTH), BF16),
                   jax.ShapeDtypeStruct((n, SWA_Q_WIDTH), BF16),
                   jax.ShapeDtypeStruct((n, SWA_KV_WIDTH), BF16),
                   jax.ShapeDtypeStruct((SWA_KV_WIDTH, n), BF16)],
        scratch_shapes=[pltpu.VMEM((tm + 8, SC_WIDTH), F32)],
        compiler_params=_params("arbitrary"),
        name="even_proj_conv",
    )(x, g, w, wvt, conv_w)


def _swa_kernel(sink_ref, q_ref, k_ref, kp_ref, vt_ref, vtp_ref, tab_ref, out_ref, *, nq):
    i = pl.program_id(1)
    blk = SWA_BLOCK
    hd = SWA_HEAD_DIM
    half = N_HEADS // SWA_KV_HEADS
    lane = lax.broadcasted_iota(jnp.int32, (blk, 2 * hd), 1)
    k_all = jnp.concatenate([kp_ref[...], k_ref[...]], axis=0)
    vt_all = jnp.concatenate([vtp_ref[...], vt_ref[...]], axis=1)

    scores = {}
    for s in range(nq):
        keys = k_all[s * blk:(s + 2) * blk]
        for j in range(half):
            q_pair = q_ref[s * blk:(s + 1) * blk, j * 2 * hd:(j + 1) * 2 * hd]
            scores[s, j] = _dot_nt(keys, jnp.where(lane < hd, q_pair, 0))
            scores[s, half + j] = _dot_nt(keys, jnp.where(lane >= hd, q_pair, 0))

    probs, denoms = {}, {}
    for s in range(nq):
        for h in range(N_HEADS):
            tab = tab_ref[h]
            if s == 0:
                tab = jnp.concatenate([jnp.where(i > 0, tab[:blk], NEG_INF), tab[blk:]], axis=0)
            sc = scores[s, h] + tab
            sink = sink_ref[h] * LOG2E
            m = jnp.maximum(jnp.max(sc, axis=0, keepdims=True), sink)
            p = jnp.exp2(sc - m)
            denoms[s, h] = jnp.sum(p, axis=0, keepdims=True) + jnp.exp2(sink - m)
            probs[s, h] = p.astype(BF16)

    for s in range(nq):
        outs = []
        for h in range(N_HEADS):
            g = h // half
            vt = vt_all[g * hd:(g + 1) * hd, s * blk:(s + 2) * blk]
            outs.append(_dot(vt, probs[s, h]) / denoms[s, h])
        out_ref[s * blk:(s + 1) * blk, :] = jnp.concatenate(outs, axis=0).T.astype(BF16)


def _swa(q, k, vt, sinks, tab, bsz, seq, nq=4):
    n = q.shape[0]
    nb = seq // SWA_BLOCK
    steps = nb // nq
    prev = lambda b, i: b * nb + jnp.maximum(i * nq - 1, 0)
    return pl.pallas_call(
        functools.partial(_swa_kernel, nq=nq),
        grid=(bsz, steps),
        in_specs=[pl.BlockSpec(memory_space=pltpu.SMEM),
                  pl.BlockSpec((nq * SWA_BLOCK, SWA_Q_WIDTH), lambda b, i: (b * steps + i, 0)),
                  pl.BlockSpec((nq * SWA_BLOCK, SWA_KV_WIDTH), lambda b, i: (b * steps + i, 0)),
                  pl.BlockSpec((SWA_BLOCK, SWA_KV_WIDTH), lambda b, i: (prev(b, i), 0)),
                  pl.BlockSpec((SWA_KV_WIDTH, nq * SWA_BLOCK), lambda b, i: (0, b * steps + i)),
                  pl.BlockSpec((SWA_KV_WIDTH, SWA_BLOCK), lambda b, i: (0, prev(b, i))),
                  _resident(tab.shape)],
        out_specs=pl.BlockSpec((nq * SWA_BLOCK, SWA_Q_WIDTH), lambda b, i: (b * steps + i, 0)),
        out_shape=jax.ShapeDtypeStruct((n, SWA_Q_WIDTH), BF16),
        compiler_params=_params("arbitrary", "arbitrary"),
        name="swa",
    )(sinks, q, k, k, vt, vt, tab)


def _ple(h, p, g, w_gate, w_proj):
    gate = jax.nn.sigmoid(_dot(_rms(h, g).astype(BF16), w_gate))
    return h + gate * _dot(p.astype(BF16), w_proj)


def _even_ffn_kernel(h_ref, a_ref, b_ref, p_ref, wo_ref, gf_ref, wg_ref, wu_ref, wd_ref,
                     gp_ref, wpg_ref, wpp_ref, out_ref, *, tf):
    h1 = (h_ref[...] + _dot(a_ref[...], wo_ref[0:SC_WIDTH, :])
          + _dot(b_ref[...], wo_ref[SC_WIDTH:, :]))
    xn = _rms(h1, gf_ref[...]).astype(BF16)
    h2 = h1 + _swiglu(xn, wg_ref, wu_ref, wd_ref, tf, jnp.zeros(h1.shape, F32))
    out_ref[...] = _ple(h2, p_ref[...], gp_ref[...], wpg_ref[...], wpp_ref[...])


def _layer_rows(p, layer, tm):
    return pl.BlockSpec((None, tm, p.shape[2]), lambda i: (layer, i, 0))


def _even_ffn(h, a, b, p, layer, wo, gf, wg, wu, wd, gp, wpg, wpp, tm=512, tf=256):
    n, d = h.shape
    row = lambda w: pl.BlockSpec((tm, w), lambda i: (i, 0))
    return pl.pallas_call(
        functools.partial(_even_ffn_kernel, tf=tf),
        grid=(n // tm,),
        in_specs=[row(d), row(a.shape[1]), row(b.shape[1]), _layer_rows(p, layer, tm),
                  _resident(wo.shape), _resident(gf.shape), _resident(wg.shape),
                  _resident(wu.shape), _resident(wd.shape), _resident(gp.shape),
                  _resident(wpg.shape), _resident(wpp.shape)],
        out_specs=row(d),
        out_shape=jax.ShapeDtypeStruct((n, d), F32),
        compiler_params=_params("arbitrary"),
        name="even_out_ffn_ple",
    )(h, a, b, p, wo, gf, wg, wu, wd, gp, wpg, wpp)


def _odd_proj_kernel(h_ref, g_ref, wq_ref, wk_ref, wvt_ref, q_ref, k_ref, vt_ref, kmean_ref, *, tm):
    xn = _rms(h_ref[...], g_ref[...]).astype(BF16)
    q_ref[...] = (_dot(xn, wq_ref[...]) * (MOBA_HEAD_DIM ** -0.5 * LOG2E)).astype(BF16)
    k = _dot(xn, wk_ref[...])
    k_ref[...] = k.astype(BF16)
    vt = _dot_nt(wvt_ref[...], xn).astype(BF16)
    ones = jnp.ones((MOBA_V_ROWS - MOBA_HEAD_DIM, tm), BF16)
    for hh in range(N_HEADS):
        vt_ref[hh * MOBA_V_ROWS:hh * MOBA_V_ROWS + MOBA_HEAD_DIM, :] = (
            vt[hh * MOBA_HEAD_DIM:(hh + 1) * MOBA_HEAD_DIM])
        vt_ref[hh * MOBA_V_ROWS + MOBA_HEAD_DIM:(hh + 1) * MOBA_V_ROWS, :] = ones
    for r in range(tm // MOBA_BLOCK):
        kmean_ref[r] = jnp.mean(k[r * MOBA_BLOCK:(r + 1) * MOBA_BLOCK, :], axis=0, keepdims=True)


def _odd_proj(h, g, wq, wk, wvt, tm=512):
    n, d = h.shape
    bpt = tm // MOBA_BLOCK
    row = pl.BlockSpec((tm, d), lambda i: (i, 0))
    return pl.pallas_call(
        functools.partial(_odd_proj_kernel, tm=tm),
        grid=(n // tm,),
        in_specs=[row, _resident(g.shape), _resident(wq.shape), _resident(wk.shape),
                  _resident(wvt.shape)],
        out_specs=[row, row, pl.BlockSpec((N_HEADS * MOBA_V_ROWS, tm), lambda i: (0, i)),
                   pl.BlockSpec((bpt, 1, d), lambda i: (i, 0, 0))],
        out_shape=[jax.ShapeDtypeStruct((n, d), BF16), jax.ShapeDtypeStruct((n, d), BF16),
                   jax.ShapeDtypeStruct((N_HEADS * MOBA_V_ROWS, n), BF16),
                   jax.ShapeDtypeStruct((n // MOBA_BLOCK, 1, d), F32)],
        compiler_params=_params("arbitrary"),
        name="odd_qkv",
    )(h, g, wq, wk, wvt)


def _moba_kernel(rb_ref, q_ref, k_ref, vt_ref, km_ref, tab_ref, out_ref,
                 sel_scr, adj_scr, m_scr, acc_scr, far_a_scr, far_b_scr, gmax_scr, *, nblk, hg):
    head0 = pl.program_id(1) * hg
    qi = pl.program_id(2)
    bs = MOBA_BLOCK
    hd = MOBA_HEAD_DIM
    heads = [slice(hh * hd, (hh + 1) * hd) for hh in range(hg)]
    head_values = [slice(hh * MOBA_V_ROWS, (hh + 1) * MOBA_V_ROWS) for hh in range(hg)]

    def select_blocks():
        blk = lax.broadcasted_iota(jnp.int32, (nblk, bs), 0)
        past = blk < qi
        for hh in range(hg):
            km = km_ref[:, heads[hh]]
            km_hi = km.astype(BF16)
            km_mid = (km - km_hi.astype(F32)).astype(BF16)
            km_lo = (km - km_hi.astype(F32) - km_mid.astype(F32)).astype(BF16)
            q = q_ref[:, heads[hh]]
            gate = _dot_nt(km_hi, q) + (_dot_nt(km_mid, q) + _dot_nt(km_lo, q))
            g = jnp.where(past, gate, NEG_INF)
            sel = jnp.zeros(gate.shape, jnp.bool_)
            for _ in range(MOBA_TOPK):
                top = jnp.max(g, axis=0, keepdims=True)
                idx = jnp.min(jnp.where(g == top, blk, nblk), axis=0, keepdims=True)
                hit = blk == idx
                sel = sel | hit
                g = jnp.where(hit, NEG_INF, g)
            far_bias = rb_ref[N_BUCKETS - 1, head0 + hh] * LOG2E
            sel_scr[hh] = jnp.where(sel & (blk < qi - 1), far_bias, NEG_INF)
            adj_scr[hh] = jnp.max(jnp.where(sel & (blk == qi - 1), 0.0, NEG_INF), axis=0,
                                  keepdims=True)

    def head_chunks():
        return [range(h0, min(h0 + MOBA_HEAD_CHUNK, hg)) for h0 in range(0, hg, MOBA_HEAD_CHUNK)]

    def near_update(start, nb, tables, query_biases, between=lambda hh: None):
        def bias(hh, r):
            return query_biases(hh, r)

        for hh in range(hg):
            s = _dot_nt(k_ref[pl.ds(start, nb * bs), heads[hh]], q_ref[:, heads[hh]])
            between(hh)
            m_new = None
            for r in range(nb):
                s_r = s[r * bs:(r + 1) * bs] + tables(hh, r)
                far_b_scr[hh, r * bs:(r + 1) * bs, :] = s_r
                col_max = jnp.max(s_r, axis=0, keepdims=True)
                if bias(hh, r) is not None:
                    col_max = col_max + bias(hh, r)
                m_new = col_max if m_new is None else jnp.maximum(m_new, col_max)
            m_scr[hh] = m_new
        for hh in range(hg):
            m_new = m_scr[hh]
            p = jnp.concatenate(
                [jnp.exp2(far_b_scr[hh, r * bs:(r + 1) * bs, :]
                          - (m_new if bias(hh, r) is None else m_new - bias(hh, r)))
                 for r in range(nb)], axis=0)
            acc_scr[hh] = _dot(vt_ref[head_values[hh], pl.ds(start, nb * bs)], p.astype(BF16))

    @pl.when(qi == 0)
    def _():
        near_update(0, 1, lambda hh, r: tab_ref[hh, :, bs:2 * bs], lambda hh, r: None)

    far_rows = MOBA_FAR_GROUP * bs
    n_far = (qi - 1 + MOBA_FAR_GROUP - 1) // MOBA_FAR_GROUP
    far_bufs = (far_a_scr, far_b_scr)

    def far_start(g):
        return pl.multiple_of(g * far_rows, far_rows)

    def far_bias(hh, g, r):
        return sel_scr[hh, pl.ds(g * MOBA_FAR_GROUP + r, 1), :]

    def far_scores(g, buf, head_ids):
        keys = k_ref.at[pl.ds(far_start(g), far_rows)]
        scores = {hh: _dot_nt(keys[:, heads[hh]], q_ref[:, heads[hh]]) for hh in head_ids}
        for hh in head_ids:
            far_bufs[buf][hh] = scores[hh]
            group_max = None
            for r in range(MOBA_FAR_GROUP):
                col_max = (jnp.max(scores[hh][r * bs:(r + 1) * bs], axis=0, keepdims=True)
                           + far_bias(hh, g, r))
                group_max = col_max if group_max is None else jnp.maximum(group_max, col_max)
            gmax_scr[buf, hh] = group_max

    def far_softmax(g, buf, head_ids):
        probs, alphas = {}, {}
        for hh in head_ids:
            m_old = m_scr[hh]
            m_new = jnp.maximum(m_old, gmax_scr[buf, hh])
            alphas[hh] = jnp.exp2(m_old - m_new)
            p = jnp.concatenate(
                [jnp.exp2(far_bufs[buf][hh, r * bs:(r + 1) * bs, :] - (m_new - far_bias(hh, g, r)))
                 for r in range(MOBA_FAR_GROUP)], axis=0)
            m_scr[hh] = m_new
            probs[hh] = p.astype(BF16)
        for hh in head_ids:
            acc_scr[hh] = (alphas[hh] * acc_scr[hh]
                           + _dot(vt_ref[head_values[hh], pl.ds(far_start(g), far_rows)], probs[hh]))

    def far_step(g, buf, prefetch):
        for head_ids in head_chunks():
            if prefetch:
                far_scores(g + 1, 1 - buf, head_ids)
            far_softmax(g, buf, head_ids)

    @pl.when(qi > 0)
    def _():
        select_blocks()
        start = pl.multiple_of((qi - 1) * bs, bs)
        near_update(start, 2, lambda hh, r: tab_ref[hh, :, r * bs:(r + 1) * bs],
                    lambda hh, r: adj_scr[hh] if r == 0 else None,
                    between=lambda hh: far_scores(0, 0, [hh]))

    def far_pair(i, carry):
        far_step(2 * i, 0, True)
        far_step(2 * i + 1, 1, True)
        return carry

    lax.fori_loop(0, (n_far - 1) // 2, far_pair, 0)

    @pl.when((n_far > 0) & (n_far % 2 == 0))
    def _():
        far_step(n_far - 2, 0, True)
        far_step(n_far - 1, 1, False)

    @pl.when(n_far % 2 == 1)
    def _():
        far_step(n_far - 1, 0, False)

    for hh in range(hg):
        out_ref[:, heads[hh]] = (acc_scr[hh, 0:hd, :] / acc_scr[hh, hd:hd + 1, :]).T.astype(BF16)


def _moba(q, k, vt, kmean, tab, rel_bias, bsz, seq, hg=N_HEADS):
    n, d = q.shape
    nblk = seq // MOBA_BLOCK
    assert nblk % MOBA_FAR_GROUP == 0, "far-block groups must not run past the sequence"
    assert MOBA_FAR_GROUP >= 2, "the far score buffer also parks the two near blocks"
    bs = MOBA_BLOCK
    w = hg * MOBA_HEAD_DIM
    return pl.pallas_call(
        functools.partial(_moba_kernel, nblk=nblk, hg=hg),
        grid=(bsz, N_HEADS // hg, nblk),
        in_specs=[pl.BlockSpec(memory_space=pltpu.SMEM),
                  pl.BlockSpec((bs, w), lambda b, g, i: (b * nblk + i, g)),
                  pl.BlockSpec((seq, w), lambda b, g, i: (b, g), pipeline_mode=pl.Buffered(1)),
                  pl.BlockSpec((hg * MOBA_V_ROWS, seq), lambda b, g, i: (g, b),
                               pipeline_mode=pl.Buffered(1)),
                  pl.BlockSpec((None, nblk, w), lambda b, g, i: (b, 0, g)),
                  pl.BlockSpec((hg, bs, 2 * bs), lambda b, g, i: (g, 0, 0),
                               pipeline_mode=pl.Buffered(1))],
        out_specs=pl.BlockSpec((bs, w), lambda b, g, i: (b * nblk + i, g)),
        out_shape=jax.ShapeDtypeStruct((n, d), BF16),
        scratch_shapes=[pltpu.VMEM((hg, nblk, bs), F32), pltpu.VMEM((hg, 1, bs), F32),
                        pltpu.VMEM((hg, 1, bs), F32), pltpu.VMEM((hg, MOBA_V_ROWS, bs), F32),
                        pltpu.VMEM((hg, MOBA_FAR_GROUP * bs, bs), F32),
                        pltpu.VMEM((hg, MOBA_FAR_GROUP * bs, bs), F32),
                        pltpu.VMEM((2, hg, 1, bs), F32)],
        compiler_params=_params("arbitrary", "arbitrary", "arbitrary"),
        name="moba",
    )(rel_bias, q, k, vt, kmean.reshape(bsz, nblk, d), tab)


def _odd_router_kernel(h_ref, c_ref, wo_ref, g_ref, wr_ref, h_out_ref, xn_ref, gates_ref, sel_ref,
                       cnt_ref):
    wr = wr_ref[...]
    wr_hi = wr.astype(BF16)
    wr_lo = (wr - wr_hi.astype(F32)).astype(BF16)
    n_parts = 2
    part = h_ref.shape[0] // n_parts
    rows = [slice(k * part, (k + 1) * part) for k in range(n_parts)]
    h1 = [h_ref[r, :] + _dot(c_ref[r, :], wo_ref[...]) for r in rows]
    xn = [_rms(h, g_ref[...]) for h in h1]
    xn_hi = [x.astype(BF16) for x in xn]
    xn_lo = [(x - hi.astype(F32)).astype(BF16) for x, hi in zip(xn, xn_hi)]
    logits = [_dot(hi, wr_hi) + (_dot(lo, wr_hi) + _dot(hi, wr_lo)) for hi, lo in zip(xn_hi, xn_lo)]
    count = jnp.zeros((1, N_EXPERTS), F32)
    for k, r in enumerate(rows):
        h_out_ref[r, :] = h1[k]
        xn_ref[r, :] = xn_hi[k]
        lane = lax.broadcasted_iota(jnp.int32, logits[k].shape, 1)
        v1 = jnp.max(logits[k], axis=-1, keepdims=True)
        i1 = jnp.min(jnp.where(logits[k] == v1, lane, N_EXPERTS), axis=-1, keepdims=True)
        rest = jnp.where(lane == i1, NEG_INF, logits[k])
        v2 = jnp.max(rest, axis=-1, keepdims=True)
        i2 = jnp.min(jnp.where(rest == v2, lane, N_EXPERTS), axis=-1, keepdims=True)
        e2 = jnp.exp(v2 - v1)
        w1 = 1.0 / (1.0 + e2)
        w2 = e2 / (1.0 + e2)
        gates_ref[r, :] = jnp.where(lane == i1, w1, 0.0) + jnp.where(lane == i2, w2, 0.0)
        sel = jnp.where((lane == i1) | (lane == i2), 1.0, 0.0)
        sel_ref[r, :] = sel
        count = count + jnp.sum(sel, axis=0, keepdims=True)
    cnt_ref[0] = count


def _odd_router(h, c, wo, g, wr, tm=MOE_TOKEN_TILE):
    n, d = h.shape
    row = lambda w: pl.BlockSpec((tm, w), lambda i: (i, 0))
    return pl.pallas_call(
        _odd_router_kernel,
        grid=(n // tm,),
        in_specs=[row(d), row(d), _resident(wo.shape), _resident(g.shape), _resident(wr.shape)],
        out_specs=[row(d), row(d), row(N_EXPERTS), row(N_EXPERTS),
                   pl.BlockSpec((1, 1, N_EXPERTS), lambda i: (i, 0, 0))],
        out_shape=[jax.ShapeDtypeStruct((n, d), F32), jax.ShapeDtypeStruct((n, d), BF16),
                   jax.ShapeDtypeStruct((n, N_EXPERTS), F32),
                   jax.ShapeDtypeStruct((n, N_EXPERTS), F32),
                   jax.ShapeDtypeStruct((n // tm, 1, N_EXPERTS), F32)],
        compiler_params=_params("arbitrary"),
        name="odd_out_router",
    )(h, c, wo, g, wr)


def _moe_plan(counts, n_slot_tiles):
    cnt = counts.reshape(-1, N_EXPERTS).astype(jnp.int32)
    grp = (cnt + MOE_ALIGN - 1) // MOE_ALIGN * MOE_ALIGN
    local = jnp.cumsum(grp, axis=1) - grp
    tot = jnp.sum(grp, axis=0)
    region = (tot + MOE_SLOT_TILE - 1) // MOE_SLOT_TILE * MOE_SLOT_TILE
    region_end = jnp.cumsum(region)
    region_start = region_end - region
    start = region_start[None, :] + jnp.cumsum(grp, axis=0) - grp
    n_used = region_end[-1] // MOE_SLOT_TILE
    tile_row = jnp.arange(n_slot_tiles, dtype=jnp.int32) * MOE_SLOT_TILE
    tile_expert = jnp.sum((tile_row[:, None] >= region_end[None, :]).astype(jnp.int32), axis=1)
    tile_expert = jnp.minimum(tile_expert, N_EXPERTS - 1)
    tile_expert = jnp.where(tile_row < region_end[-1], tile_expert, tile_expert[n_used - 1])
    tile_end = tile_row + MOE_SLOT_TILE
    tile_half = jnp.any((tile_end[:, None] == region_end[None, :])
                        & ((region - tot)[None, :] >= MOE_SLOT_TILE // 2)
                        & (region[None, :] > 0), axis=1).astype(jnp.int32)
    return dict(cnt=cnt, local=local.astype(jnp.int32), start=start.astype(jnp.int32),
                tile_half=tile_half,
                tail_start=(region_start + tot).astype(jnp.int32),
                tail_len=(region - tot).astype(jnp.int32), tile_expert=tile_expert,
                n_used=n_used.reshape(1).astype(jnp.int32))


def _group_copies(cnt_ref, local_ref, start_ref, t, local_buf, slot_array, sem, to_slots):
    copies = []
    for e in range(N_EXPERTS):
        rows = (cnt_ref[t, e] + MOE_ALIGN - 1) // MOE_ALIGN * MOE_ALIGN
        size = MOE_TOKEN_TILE
        while size >= MOE_ALIGN:
            off = rows & ~(2 * size - 1)
            local = local_buf.at[pl.ds(pl.multiple_of(local_ref[t, e] + off, MOE_ALIGN), size)]
            slots = slot_array.at[pl.ds(pl.multiple_of(start_ref[t, e] + off, MOE_ALIGN), size)]
            src, dst = (local, slots) if to_slots else (slots, local)
            copies.append(((rows & size) != 0, pltpu.make_async_copy(src, dst, sem)))
            size //= 2
    return copies


def _start_all(copies):
    for cond, cp in copies:
        pl.when(cond)(cp.start)


def _wait_all(copies):
    for cond, cp in copies:
        pl.when(cond)(cp.wait)


def _moe_dispatch_kernel(cnt_ref, local_ref, start_ref, tail_start_ref, tail_len_ref, x_ref, sel_ref,
                         upper_ref, xs_ref, stage_ref, zero_ref, sem, tail_sem):
    t = pl.program_id(0)
    last = pl.num_programs(0) - 1
    tile = x_ref.shape[0]
    slot = t % 2
    eye = jnp.where(lax.broadcasted_iota(jnp.int32, (N_EXPERTS, N_EXPERTS), 0)
                    == lax.broadcasted_iota(jnp.int32, (N_EXPERTS, N_EXPERTS), 1), 1.0, 0.0)
    sel_t = _dot_nt(eye.astype(BF16), sel_ref[...].astype(BF16))
    rank_t = _dot(sel_t.astype(BF16), upper_ref[...])
    expert = lax.broadcasted_iota(jnp.int32, (N_EXPERTS, 1), 0)
    base = jnp.zeros((N_EXPERTS, 1), F32)
    for e in range(N_EXPERTS):
        base = jnp.where(expert == e, local_ref[t, e].astype(F32), base)
    pos = base + rank_t
    pos_lo = jnp.min(jnp.where(sel_t > 0.5, pos, float(MOE_LOCAL_ROWS)), axis=0, keepdims=True)
    pos_hi = jnp.max(jnp.where(sel_t > 0.5, pos, -1.0), axis=0, keepdims=True)
    row = lax.broadcasted_iota(jnp.int32, (MOE_LOCAL_ROWS, tile), 0).astype(F32)
    onehot = jnp.where(row == pos_lo, 1.0, jnp.where(row == pos_hi, 1.0, 0.0)).astype(BF16)
    stage_ref[slot] = _dot(onehot, x_ref[...]).astype(BF16)

    def copies(step):
        return _group_copies(cnt_ref, local_ref, start_ref, step, stage_ref.at[step % 2], xs_ref,
                             sem.at[step % 2], to_slots=True)

    _start_all(copies(t))

    @pl.when(t > 0)
    def _():
        _wait_all(copies(t - 1))

    @pl.when(t == last)
    def _():
        _wait_all(copies(t))
        zero_ref[...] = jnp.zeros(zero_ref.shape, BF16)
        tails = []
        for e in range(N_EXPERTS):
            size = MOE_SLOT_TILE // 2
            while size >= MOE_ALIGN:
                off = tail_len_ref[e] & ~(2 * size - 1)
                dst = pl.multiple_of(tail_start_ref[e] + off, MOE_ALIGN)
                tails.append(((tail_len_ref[e] & size) != 0, pltpu.make_async_copy(
                    zero_ref.at[pl.ds(0, size)], xs_ref.at[pl.ds(dst, size)], tail_sem)))
                size //= 2
        _start_all(tails)
        _wait_all(tails)

        used_rows = tail_start_ref[N_EXPERTS - 1] + tail_len_ref[N_EXPERTS - 1]
        piece = zero_ref.shape[0]

        def unused_copy(k):
            dst = pl.multiple_of(used_rows + k * piece, piece)
            return pltpu.make_async_copy(zero_ref, xs_ref.at[pl.ds(dst, piece)], tail_sem)

        n_pieces = (xs_ref.shape[0] - used_rows) // piece
        lax.fori_loop(0, n_pieces, lambda k, c: (unused_copy(k).start(), c)[1], 0)
        lax.fori_loop(0, n_pieces, lambda k, c: (unused_copy(k).wait(), c)[1], 0)


def _moe_dispatch(xn, sel, plan, n_slot_tiles, tm=MOE_TOKEN_TILE):
    n, d = xn.shape
    upper = jnp.asarray(np.triu(np.ones((tm, tm), np.float32), 1), BF16)
    smem = pl.BlockSpec(memory_space=pltpu.SMEM)
    return pl.pallas_call(
        _moe_dispatch_kernel,
        grid=(n // tm,),
        in_specs=[smem, smem, smem, smem, smem,
                  pl.BlockSpec((tm, d), lambda i: (i, 0)),
                  pl.BlockSpec((tm, N_EXPERTS), lambda i: (i, 0)),
                  _resident((tm, tm))],
        out_specs=pl.BlockSpec(memory_space=pl.ANY),
        out_shape=jax.ShapeDtypeStruct((n_slot_tiles * MOE_SLOT_TILE, d), BF16),
        scratch_shapes=[pltpu.VMEM((2, MOE_LOCAL_ROWS, d), BF16),
                        pltpu.VMEM((MOE_SLOT_TILE // 2, d), BF16),
                        pltpu.SemaphoreType.DMA((2,)), pltpu.SemaphoreType.DMA(())],
        compiler_params=_params("arbitrary"),
        name="moe_dispatch",
    )(plan["cnt"], plan["local"], plan["start"], plan["tail_start"], plan["tail_len"], xn, sel, upper)


def _moe_experts_kernel(te_ref, nu_ref, half_ref, x_ref, wg_ref, wu_ref, wd_ref, y_ref, acc_ref):
    i = pl.program_id(0)
    f = pl.program_id(1)
    last = pl.num_programs(1) - 1
    used = i < nu_ref[0]
    half = half_ref[i] == 1
    hrows = x_ref.shape[0] // 2

    @pl.when(used & (f == 0))
    def _():
        acc_ref[...] = jnp.zeros(acc_ref.shape, F32)

    @pl.when(used & jnp.logical_not(half))
    def _():
        acc_ref[...] = _swiglu(x_ref[...], wg_ref, wu_ref, wd_ref, MOE_FF_CHUNK, acc_ref[...])

    @pl.when(used & half)
    def _():
        acc_ref[0:hrows, :] = _swiglu(x_ref[0:hrows, :], wg_ref, wu_ref, wd_ref, MOE_FF_CHUNK,
                                      acc_ref[0:hrows, :])

    @pl.when(used & (f == last))
    def _():
        y_ref[...] = acc_ref[...].astype(BF16)

    @pl.when(jnp.logical_not(used) & (f == last))
    def _():
        y_ref[...] = jnp.zeros(y_ref.shape, BF16)


def _moe_experts(xs, plan, wg, wu, wd, tf=1792):
    rows, d = xs.shape
    d_ff = wg.shape[2]
    nf = d_ff // tf
    tm = MOE_SLOT_TILE
    f_idx = lambda i, f, nu: jnp.where(i < nu[0], f, nf - 1)
    return pl.pallas_call(
        _moe_experts_kernel,
        grid_spec=pltpu.PrefetchScalarGridSpec(
            num_scalar_prefetch=3,
            grid=(rows // tm, nf),
            in_specs=[pl.BlockSpec((tm, d), lambda i, f, te, nu, hf: (jnp.minimum(i, nu[0] - 1), 0)),
                      pl.BlockSpec((None, d, tf), lambda i, f, te, nu, hf: (te[i], 0, f_idx(i, f, nu))),
                      pl.BlockSpec((None, d, tf), lambda i, f, te, nu, hf: (te[i], 0, f_idx(i, f, nu))),
                      pl.BlockSpec((None, tf, d), lambda i, f, te, nu, hf: (te[i], f_idx(i, f, nu), 0))],
            out_specs=pl.BlockSpec((tm, d), lambda i, f, te, nu, hf: (i, 0)),
            scratch_shapes=[pltpu.VMEM((tm, d), F32)]),
        out_shape=jax.ShapeDtypeStruct((rows, d), BF16),
        compiler_params=_params("arbitrary", "arbitrary"),
        name="moe_experts",
    )(plan["tile_expert"], plan["n_used"], plan["tile_half"], xs, wg, wu, wd)


def _moe_combine_kernel(cnt_ref, local_ref, start_ref, h_ref, gates_ref, sel_ref, lower_ref, p_ref,
                        gp_ref, wpg_ref, wpp_ref, gfin_ref, ys_ref, out_ref, ybuf_ref, sem):
    t = pl.program_id(0)
    tile = h_ref.shape[0]

    def copies(step):
        return _group_copies(cnt_ref, local_ref, start_ref, step, ybuf_ref.at[step % 2], ys_ref,
                             sem.at[step % 2], to_slots=False)

    @pl.when(t == 0)
    def _():
        ybuf_ref[...] = jnp.zeros(ybuf_ref.shape, BF16)
        _start_all(copies(t))

    @pl.when(t + 1 < pl.num_programs(0))
    def _():
        _start_all(copies(t + 1))

    sel = sel_ref[...] > 0.5
    gates = gates_ref[...]
    rank = _dot(lower_ref[...], sel_ref[...].astype(BF16))
    expert = lax.broadcasted_iota(jnp.int32, (1, N_EXPERTS), 1)
    base = jnp.zeros((1, N_EXPERTS), F32)
    for e in range(N_EXPERTS):
        base = jnp.where(expert == e, local_ref[t, e].astype(F32), base)
    pos = base + rank
    pos_lo = jnp.min(jnp.where(sel, pos, float(MOE_LOCAL_ROWS)), axis=1, keepdims=True)
    pos_hi = jnp.max(jnp.where(sel, pos, -1.0), axis=1, keepdims=True)
    gate_lo = jnp.sum(jnp.where(sel & (pos == pos_lo), gates, 0.0), axis=1, keepdims=True)
    gate_hi = jnp.sum(jnp.where(sel & (pos == pos_hi), gates, 0.0), axis=1, keepdims=True)
    col = lax.broadcasted_iota(jnp.int32, (tile, MOE_LOCAL_ROWS), 1).astype(F32)
    onehot_lo = jnp.where(col == pos_lo, 1.0, 0.0).astype(BF16)
    onehot_hi = jnp.where(col == pos_hi, 1.0, 0.0).astype(BF16)

    _wait_all(copies(t))
    y = ybuf_ref[t % 2]
    h = h_ref[...] + gate_lo * _dot(onehot_lo, y) + gate_hi * _dot(onehot_hi, y)
    h = _ple(h, p_ref[...], gp_ref[...], wpg_ref[...], wpp_ref[...])
    out_ref[...] = _rms(h, gfin_ref[...])


def _moe_combine(h, gates, sel, ys, plan, p, layer, gp, wpg, wpp, gfin, tm=MOE_TOKEN_TILE):
    n, d = h.shape
    lower = jnp.asarray(np.tril(np.ones((tm, tm), np.float32), -1), BF16)
    smem = pl.BlockSpec(memory_space=pltpu.SMEM)
    row = lambda w: pl.BlockSpec((tm, w), lambda i: (i, 0))
    return pl.pallas_call(
        _moe_combine_kernel,
        grid=(n // tm,),
        in_specs=[smem, smem, smem, row(d), row(N_EXPERTS), row(N_EXPERTS), _resident((tm, tm)),
                  _layer_rows(p, layer, tm), _resident(gp.shape), _resident(wpg.shape),
                  _resident(wpp.shape), _resident(gfin.shape), pl.BlockSpec(memory_space=pl.ANY)],
        out_specs=row(d),
        out_shape=jax.ShapeDtypeStruct((n, d), F32),
        scratch_shapes=[pltpu.VMEM((2, MOE_LOCAL_ROWS, d), BF16), pltpu.SemaphoreType.DMA((2,))],
        compiler_params=_params("arbitrary"),
        name="moe_combine_ple_norm",
    )(plan["cnt"], plan["local"], plan["start"], h, gates, sel, lower, p, gp, wpg, wpp, gfin, ys)


def kernel(x, p, rel_bias, final_norm, e_norm_mix, e_w_in, e_conv_w, e_sinks, e_w_out, e_norm_ffn, e_ffn_gate, e_ffn_up, e_ffn_down, e_norm_ple, e_ple_gate, e_ple_proj, o_norm_mix, o_w_qkv, o_w_o, o_norm_ffn, o_router, o_exp_gate, o_exp_up, o_exp_down, o_norm_ple, o_ple_gate, o_ple_proj):
    bsz, seq, d = x.shape
    n = bsz * seq
    bf = lambda w: w.astype(BF16)
    h = x.reshape(n, d)
    p2 = p.reshape(p.shape[0], n, p.shape[-1])

    kk = np.arange(2 * SWA_BLOCK)[:, None]
    qq = np.arange(SWA_BLOCK)[None, :]
    dist = qq - kk + SWA_BLOCK
    swa_bucket = np.where((dist >= 0) & (dist < SWA_BLOCK), _t5_bucket_np(dist), -1)
    swa_tab = _bias_table(rel_bias, swa_bucket, mult=LOG2E)
    kk = np.arange(MOBA_BLOCK)[:, None]
    qq = np.arange(MOBA_BLOCK)[None, :]
    own_bucket = np.where(qq >= kk, _t5_bucket_np(qq - kk), -1)
    moba_bucket = np.concatenate([_t5_bucket_np(qq - kk + MOBA_BLOCK), own_bucket], axis=1)
    moba_tab = _bias_table(rel_bias, moba_bucket, mult=LOG2E)

    w_in = e_w_in[0]
    q0 = 3 * SC_WIDTH
    k0 = q0 + SWA_Q_WIDTH
    v0 = k0 + SWA_KV_WIDTH
    half = N_HEADS // SWA_KV_HEADS
    q_cols = np.concatenate([np.arange(h * SWA_HEAD_DIM, (h + 1) * SWA_HEAD_DIM)
                             for j in range(half) for h in (j, half + j)])
    w_main = jnp.concatenate([w_in[:, :q0], w_in[:, q0:k0][:, q_cols], w_in[:, k0:v0]], axis=1)
    a_out, q0_, k0_, vt0 = _even_proj(h, e_norm_mix[0:1], bf(w_main), bf(w_in[:, v0:].T),
                                      e_conv_w[0], seq)
    b_out = _swa(q0_, k0_, vt0, e_sinks[0], swa_tab, bsz, seq)
    h = _even_ffn(h, a_out, b_out, p2, 0, bf(e_w_out[0]), e_norm_ffn[0:1],
                  bf(e_ffn_gate[0]), bf(e_ffn_up[0]), bf(e_ffn_down[0]),
                  e_norm_ple[0:1], bf(e_ple_gate[0]), bf(e_ple_proj[0]))

    w_qkv = o_w_qkv[0]
    q1, k1, vt1, kmean = _odd_proj(h, o_norm_mix[0:1], bf(w_qkv[:, :d]), bf(w_qkv[:, d:2 * d]),
                                   bf(w_qkv[:, 2 * d:].T))
    c_out = _moba(q1, k1, vt1, kmean, moba_tab, rel_bias, bsz, seq)
    h, xn, gates, sel, counts = _odd_router(h, c_out, bf(o_w_o[0]), o_norm_ffn[0:1], o_router[0])
    n_groups = (n // MOE_TOKEN_TILE) * N_EXPERTS
    max_rows = 2 * n + n_groups * (MOE_ALIGN - 1) + N_EXPERTS * (MOE_SLOT_TILE - 1)
    n_slot_tiles = -(-max_rows // MOE_SLOT_TILE) + 1
    plan = _moe_plan(counts, n_slot_tiles)
    xs = _moe_dispatch(xn, sel, plan, n_slot_tiles)
    ys = _moe_experts(xs, plan, bf(o_exp_gate[0]), bf(o_exp_up[0]), bf(o_exp_down[0]))
    out = _moe_combine(h, gates, sel, ys, plan, p2, 1, o_norm_ple[0:1], bf(o_ple_gate[0]),
                       bf(o_ple_proj[0]), final_norm.reshape(1, d))
    return out.reshape(bsz, seq, d)
```

```python
import functools
import math

import jax
import jax.numpy as jnp
import numpy as np
from jax import lax
from jax.experimental import pallas as pl
from jax.experimental.pallas import tpu as pltpu

F32 = jnp.float32
BF16 = jnp.bfloat16

EPS = 1e-6
N_HEADS = 8
N_BUCKETS = 32
MAX_DISTANCE = 128

SC_WIDTH = 512
CONV_WIDTH = 3
SWA_KV_HEADS = 2
SWA_HEAD_DIM = 64
SWA_BLOCK = 128
SWA_Q_WIDTH = N_HEADS * SWA_HEAD_DIM
SWA_KV_WIDTH = SWA_KV_HEADS * SWA_HEAD_DIM

MOBA_HEAD_DIM = 128
MOBA_BLOCK = 256
MOBA_TOPK = 3
MOBA_FAR_GROUP = 2
MOBA_HEAD_CHUNK = 1
MOBA_V_ROWS = MOBA_HEAD_DIM + 16

N_EXPERTS = 8
MOE_TOKEN_TILE = 512
MOE_SLOT_TILE = 512
MOE_ALIGN = 16
CAST_ROWS = 256
MOE_FF_CHUNK = 256
MOE_LOCAL_ROWS = -(-(2 * MOE_TOKEN_TILE + N_EXPERTS * (MOE_ALIGN - 1)) // 128) * 128

VMEM_LIMIT_BYTES = 56 * 1024 * 1024
NEG_INF = float("-inf")
LOG2E = math.log2(math.e)


def _params(*semantics):
    return pltpu.CompilerParams(dimension_semantics=semantics,
                                vmem_limit_bytes=VMEM_LIMIT_BYTES)


def _resident(shape):
    zeros = (0,) * len(shape)
    return pl.BlockSpec(shape, lambda *_: zeros, pipeline_mode=pl.Buffered(1))


def _rms(x, g):
    return x * lax.rsqrt(jnp.mean(x * x, axis=-1, keepdims=True) + EPS) * g


def _dot(a, b):
    return jnp.dot(a, b, preferred_element_type=F32)


def _swiglu(x, wg_ref, wu_ref, wd_ref, chunk, acc):
    for c in range(wg_ref.shape[1] // chunk):
        cols = slice(c * chunk, (c + 1) * chunk)
        gate = _dot(x, wg_ref[:, cols])
        up = _dot(x, wu_ref[:, cols])
        hid = (gate * jax.nn.sigmoid(gate) * up).astype(BF16)
        acc = acc + _dot(hid, wd_ref[cols, :])
    return acc


def _dot_nt(a, b):
    return lax.dot_general(a, b, (((1,), (1,)), ((), ())), preferred_element_type=F32)


def _t5_bucket_np(dist):
    d = np.maximum(dist, 0).astype(np.int32)
    max_exact = N_BUCKETS // 2
    scaled = (np.log(np.maximum(d, max_exact).astype(np.float32) / np.float32(max_exact))
              / np.float32(math.log(MAX_DISTANCE / max_exact)))
    large = np.minimum(max_exact + (scaled * (N_BUCKETS - max_exact)).astype(np.int32),
                       N_BUCKETS - 1)
    return np.where(d < max_exact, d, large).astype(np.int32)


def _bias_table_kernel(rb_ref, bkt_ref, out_ref, *, mult):
    h = pl.program_id(0)
    bkt = bkt_ref[...]
    acc = jnp.full(bkt.shape, NEG_INF, F32)
    for b in range(N_BUCKETS):
        acc = jnp.where(bkt == b, rb_ref[b, h], acc)
    out_ref[...] = acc * mult


def _bias_table(rel_bias, bucket, mult=1.0):
    rows, cols = bucket.shape
    return pl.pallas_call(
        functools.partial(_bias_table_kernel, mult=mult),
        grid=(N_HEADS,),
        in_specs=[pl.BlockSpec(memory_space=pltpu.SMEM),
                  pl.BlockSpec((rows, cols), lambda h: (0, 0))],
        out_specs=pl.BlockSpec((None, rows, cols), lambda h: (h, 0, 0)),
        out_shape=jax.ShapeDtypeStruct((N_HEADS, rows, cols), F32),
        compiler_params=_params("arbitrary"),
        name="bias_table",
    )(rel_bias, jnp.asarray(bucket))


def _even_proj_kernel(x_ref, g_ref, w_ref, wvt_ref, cw_ref, a_ref, q_ref, k_ref, vt_ref, cu_scr, *,
                      tm, tiles_per_seq):
    i = pl.program_id(0)
    cu_scr[0:8, :] = jnp.where(i % tiles_per_seq == 0, 0.0, cu_scr[tm:tm + 8, :])
    xn = _rms(x_ref[...], g_ref[...]).astype(BF16)
    q0 = 3 * SC_WIDTH
    k0 = q0 + SWA_Q_WIDTH
    c_gate = _dot(xn, w_ref[:, SC_WIDTH:2 * SC_WIDTH])
    u = _dot(xn, w_ref[:, 2 * SC_WIDTH:3 * SC_WIDTH])
    q = _dot(xn, w_ref[:, q0:k0])
    k = _dot(xn, w_ref[:, k0:])
    vt = _dot_nt(wvt_ref[...], xn)
    b_gate = _dot(xn, w_ref[:, 0:SC_WIDTH])
    cu = c_gate * u
    cu_scr[8:8 + tm, :] = cu
    y = (cw_ref[0:1, :] * cu_scr[6:6 + tm, :] + cw_ref[1:2, :] * cu_scr[7:7 + tm, :]
         + cw_ref[2:3, :] * cu)
    q_ref[...] = (q * (SWA_HEAD_DIM ** -0.5 * LOG2E)).astype(BF16)
    k_ref[...] = k.astype(BF16)
    vt_ref[...] = vt.astype(BF16)
    a_ref[...] = (b_gate * y).astype(BF16)


def _even_proj(x, g, w, wvt, conv_w, seq, tm=512):
    n, d = x.shape
    row = lambda width: pl.BlockSpec((tm, width), lambda i: (i, 0))
    return pl.pallas_call(
        functools.partial(_even_proj_kernel, tm=tm, tiles_per_seq=seq // tm),
        grid=(n // tm,),
        in_specs=[row(d), _resident((1, d)), _resident(w.shape), _resident(wvt.shape),
                  _resident((CONV_WIDTH, SC_WIDTH))],
        out_specs=[row(SC_WIDTH), row(SWA_Q_WIDTH), row(SWA_KV_WIDTH),
                   pl.BlockSpec((SWA_KV_WIDTH, tm), lambda i: (0, i))],
        out_shape=[jax.ShapeDtypeStruct((n, SC_WIDTH), BF16),
                   jax.ShapeDtypeStruct((n, SWA_Q_WIDTH), BF16),
                   jax.ShapeDtypeStruct((n, SWA_KV_WIDTH), BF16),
                   jax.ShapeDtypeStruct((SWA_KV_WIDTH, n), BF16)],
        scratch_shapes=[pltpu.VMEM((tm + 8, SC_WIDTH), F32)],
        compiler_params=_params("arbitrary"),
        name="even_proj_conv",
    )(x, g, w, wvt, conv_w)


def _swa_kernel(sink_ref, q_ref, k_ref, kp_ref, vt_ref, vtp_ref, tab_ref, out_ref, *, nq):
    i = pl.program_id(1)
    blk = SWA_BLOCK
    hd = SWA_HEAD_DIM
    half = N_HEADS // SWA_KV_HEADS
    lane = lax.broadcasted_iota(jnp.int32, (blk, 2 * hd), 1)
    k_all = jnp.concatenate([kp_ref[...], k_ref[...]], axis=0)
    vt_all = jnp.concatenate([vtp_ref[...], vt_ref[...]], axis=1)

    scores = {}
    for s in range(nq):
        keys = k_all[s * blk:(s + 2) * blk]
        for j in range(half):
            q_pair = q_ref[s * blk:(s + 1) * blk, j * 2 * hd:(j + 1) * 2 * hd]
            scores[s, j] = _dot_nt(keys, jnp.where(lane < hd, q_pair, 0))
            scores[s, half + j] = _dot_nt(keys, jnp.where(lane >= hd, q_pair, 0))

    probs, denoms = {}, {}
    for s in range(nq):
        for h in range(N_HEADS):
            tab = tab_ref[h]
            if s == 0:
                tab = jnp.concatenate([jnp.where(i > 0, tab[:blk], NEG_INF), tab[blk:]], axis=0)
            sc = scores[s, h] + tab
            sink = sink_ref[h] * LOG2E
            m = jnp.maximum(jnp.max(sc, axis=0, keepdims=True), sink)
            p = jnp.exp2(sc - m)
            denoms[s, h] = jnp.sum(p, axis=0, keepdims=True) + jnp.exp2(sink - m)
            probs[s, h] = p.astype(BF16)

    for s in range(nq):
        outs = []
        for h in range(N_HEADS):
            g = h // half
            vt = vt_all[g * hd:(g + 1) * hd, s * blk:(s + 2) * blk]
            outs.append(_dot(vt, probs[s, h]) / denoms[s, h])
        out_ref[s * blk:(s + 1) * blk, :] = jnp.concatenate(outs, axis=0).T.astype(BF16)


def _swa(q, k, vt, sinks, tab, bsz, seq, nq=4):
    n = q.shape[0]
    nb = seq // SWA_BLOCK
    steps = nb // nq
    prev = lambda b, i: b * nb + jnp.maximum(i * nq - 1, 0)
    return pl.pallas_call(
        functools.partial(_swa_kernel, nq=nq),
        grid=(bsz, steps),
        in_specs=[pl.BlockSpec(memory_space=pltpu.SMEM),
                  pl.BlockSpec((nq * SWA_BLOCK, SWA_Q_WIDTH), lambda b, i: (b * steps + i, 0)),
                  pl.BlockSpec((nq * SWA_BLOCK, SWA_KV_WIDTH), lambda b, i: (b * steps + i, 0)),
                  pl.BlockSpec((SWA_BLOCK, SWA_KV_WIDTH), lambda b, i: (prev(b, i), 0)),
                  pl.BlockSpec((SWA_KV_WIDTH, nq * SWA_BLOCK), lambda b, i: (0, b * steps + i)),
                  pl.BlockSpec((SWA_KV_WIDTH, SWA_BLOCK), lambda b, i: (0, prev(b, i))),
                  _resident(tab.shape)],
        out_specs=pl.BlockSpec((nq * SWA_BLOCK, SWA_Q_WIDTH), lambda b, i: (b * steps + i, 0)),
        out_shape=jax.ShapeDtypeStruct((n, SWA_Q_WIDTH), BF16),
        compiler_params=_params("arbitrary", "arbitrary"),
        name="swa",
    )(sinks, q, k, k, vt, vt, tab)


def _ple(h, p, g, w_gate, w_proj):
    gate = jax.nn.sigmoid(_dot(_rms(h, g).astype(BF16), w_gate))
    return h + gate * _dot(p.astype(BF16), w_proj)


def _cast_weights_once(pairs, buf, sem):
    chunks = [(w, s, r0) for w, s in pairs for r0 in range(0, w.shape[0], CAST_ROWS)]

    def copy(k):
        w, _, r0 = chunks[k]
        return pltpu.make_async_copy(w.at[pl.ds(r0, CAST_ROWS)], buf.at[k % 2, :, 0:w.shape[1]],
                                     sem.at[k % 2])

    @pl.when(pl.program_id(0) == 0)
    def _():
        copy(0).start()
        for k, (w, s, r0) in enumerate(chunks):
            if k + 1 < len(chunks):
                copy(k + 1).start()
            copy(k).wait()
            s[r0:r0 + CAST_ROWS, :] = buf[k % 2, :, 0:w.shape[1]].astype(BF16)


def _even_ffn_kernel(h_ref, a_ref, b_ref, p_ref, gf_ref, gp_ref, wo_hbm, wg_hbm, wu_hbm, wd_hbm,
                     wpg_hbm, wpp_hbm, out_ref, wo_ref, wg_ref, wu_ref, wd_ref, wpg_ref, wpp_ref,
                     cast_buf, cast_sem, *, tf):
    _cast_weights_once([(wo_hbm, wo_ref), (wg_hbm, wg_ref), (wu_hbm, wu_ref), (wd_hbm, wd_ref),
                        (wpg_hbm, wpg_ref), (wpp_hbm, wpp_ref)], cast_buf, cast_sem)
    h1 = (h_ref[...] + _dot(a_ref[...], wo_ref[0:SC_WIDTH, :])
          + _dot(b_ref[...], wo_ref[SC_WIDTH:, :]))
    xn = _rms(h1, gf_ref[...]).astype(BF16)
    h2 = h1 + _swiglu(xn, wg_ref, wu_ref, wd_ref, tf, jnp.zeros(h1.shape, F32))
    out_ref[...] = _ple(h2, p_ref[...], gp_ref[...], wpg_ref[...], wpp_ref[...])


def _layer_rows(p, layer, tm):
    return pl.BlockSpec((None, tm, p.shape[2]), lambda i: (layer, i, 0))


def _even_ffn(h, a, b, p, layer, wo, gf, wg, wu, wd, gp, wpg, wpp, tm=512, tf=256):
    n, d = h.shape
    row = lambda w: pl.BlockSpec((tm, w), lambda i: (i, 0))
    weights = [wo, wg, wu, wd, wpg, wpp]
    hbm = pl.BlockSpec(memory_space=pl.ANY)
    return pl.pallas_call(
        functools.partial(_even_ffn_kernel, tf=tf),
        grid=(n // tm,),
        in_specs=[row(d), row(a.shape[1]), row(b.shape[1]), _layer_rows(p, layer, tm),
                  _resident(gf.shape), _resident(gp.shape)] + [hbm] * len(weights),
        out_specs=row(d),
        out_shape=jax.ShapeDtypeStruct((n, d), F32),
        scratch_shapes=[pltpu.VMEM(w.shape, BF16) for w in weights] + [
            pltpu.VMEM((2, CAST_ROWS, max(w.shape[1] for w in weights)), F32),
            pltpu.SemaphoreType.DMA((2,))],
        compiler_params=_params("arbitrary"),
        name="even_out_ffn_ple",
    )(h, a, b, p, gf, gp, *weights)


def _odd_proj_kernel(h_ref, g_ref, wq_ref, wk_ref, wvt_ref, q_ref, k_ref, vt_ref, kmean_ref, *, tm):
    xn = _rms(h_ref[...], g_ref[...]).astype(BF16)
    q_ref[...] = (_dot(xn, wq_ref[...]) * (MOBA_HEAD_DIM ** -0.5 * LOG2E)).astype(BF16)
    k = _dot(xn, wk_ref[...])
    k_ref[...] = k.astype(BF16)
    vt = _dot_nt(wvt_ref[...], xn).astype(BF16)
    ones = jnp.ones((MOBA_V_ROWS - MOBA_HEAD_DIM, tm), BF16)
    for hh in range(N_HEADS):
        vt_ref[hh * MOBA_V_ROWS:hh * MOBA_V_ROWS + MOBA_HEAD_DIM, :] = (
            vt[hh * MOBA_HEAD_DIM:(hh + 1) * MOBA_HEAD_DIM])
        vt_ref[hh * MOBA_V_ROWS + MOBA_HEAD_DIM:(hh + 1) * MOBA_V_ROWS, :] = ones
    for r in range(tm // MOBA_BLOCK):
        kmean_ref[r] = jnp.mean(k[r * MOBA_BLOCK:(r + 1) * MOBA_BLOCK, :], axis=0, keepdims=True)


def _odd_proj(h, g, wq, wk, wvt, tm=512):
    n, d = h.shape
    bpt = tm // MOBA_BLOCK
    row = pl.BlockSpec((tm, d), lambda i: (i, 0))
    return pl.pallas_call(
        functools.partial(_odd_proj_kernel, tm=tm),
        grid=(n // tm,),
        in_specs=[row, _resident(g.shape), _resident(wq.shape), _resident(wk.shape),
                  _resident(wvt.shape)],
        out_specs=[row, row, pl.BlockSpec((N_HEADS * MOBA_V_ROWS, tm), lambda i: (0, i)),
                   pl.BlockSpec((bpt, 1, d), lambda i: (i, 0, 0))],
        out_shape=[jax.ShapeDtypeStruct((n, d), BF16), jax.ShapeDtypeStruct((n, d), BF16),
                   jax.ShapeDtypeStruct((N_HEADS * MOBA_V_ROWS, n), BF16),
                   jax.ShapeDtypeStruct((n // MOBA_BLOCK, 1, d), F32)],
        compiler_params=_params("arbitrary"),
        name="odd_qkv",
    )(h, g, wq, wk, wvt)


def _moba_kernel(rb_ref, q_ref, k_ref, vt_ref, km_ref, tab_ref, out_ref,
                 sel_scr, adj_scr, m_scr, acc_scr, far_a_scr, far_b_scr, gmax_scr, *, nblk, hg):
    head0 = pl.program_id(1) * hg
    qi = pl.program_id(2)
    bs = MOBA_BLOCK
    hd = MOBA_HEAD_DIM
    heads = [slice(hh * hd, (hh + 1) * hd) for hh in range(hg)]
    head_values = [slice(hh * MOBA_V_ROWS, (hh + 1) * MOBA_V_ROWS) for hh in range(hg)]

    def select_blocks():
        blk = lax.broadcasted_iota(jnp.int32, (nblk, bs), 0)
        past = blk < qi
        for hh in range(hg):
            km = km_ref[:, heads[hh]]
            km_hi = km.astype(BF16)
            km_mid = (km - km_hi.astype(F32)).astype(BF16)
            km_lo = (km - km_hi.astype(F32) - km_mid.astype(F32)).astype(BF16)
            q = q_ref[:, heads[hh]]
            gate = _dot_nt(km_hi, q) + (_dot_nt(km_mid, q) + _dot_nt(km_lo, q))
            g = jnp.where(past, gate, NEG_INF)
            sel = jnp.zeros(gate.shape, jnp.bool_)
            for _ in range(MOBA_TOPK):
                top = jnp.max(g, axis=0, keepdims=True)
                idx = jnp.min(jnp.where(g == top, blk, nblk), axis=0, keepdims=True)
                hit = blk == idx
                sel = sel | hit
                g = jnp.where(hit, NEG_INF, g)
            far_bias = rb_ref[N_BUCKETS - 1, head0 + hh] * LOG2E
            sel_scr[hh] = jnp.where(sel & (blk < qi - 1), far_bias, NEG_INF)
            adj_scr[hh] = jnp.max(jnp.where(sel & (blk == qi - 1), 0.0, NEG_INF), axis=0,
                                  keepdims=True)

    def head_chunks():
        return [range(h0, min(h0 + MOBA_HEAD_CHUNK, hg)) for h0 in range(0, hg, MOBA_HEAD_CHUNK)]

    def near_update(start, nb, tables, query_biases, between=lambda hh: None):
        def bias(hh, r):
            return query_biases(hh, r)

        for hh in range(hg):
            s = _dot_nt(k_ref[pl.ds(start, nb * bs), heads[hh]], q_ref[:, heads[hh]])
            between(hh)
            m_new = None
            for r in range(nb):
                s_r = s[r * bs:(r + 1) * bs] + tables(hh, r)
                far_b_scr[hh, r * bs:(r + 1) * bs, :] = s_r
                col_max = jnp.max(s_r, axis=0, keepdims=True)
                if bias(hh, r) is not None:
                    col_max = col_max + bias(hh, r)
                m_new = col_max if m_new is None else jnp.maximum(m_new, col_max)
            m_scr[hh] = m_new
        for hh in range(hg):
            m_new = m_scr[hh]
            p = jnp.concatenate(
                [jnp.exp2(far_b_scr[hh, r * bs:(r + 1) * bs, :]
                          - (m_new if bias(hh, r) is None else m_new - bias(hh, r)))
                 for r in range(nb)], axis=0)
            acc_scr[hh] = _dot(vt_ref[head_values[hh], pl.ds(start, nb * bs)], p.astype(BF16))

    @pl.when(qi == 0)
    def _():
        near_update(0, 1, lambda hh, r: tab_ref[hh, :, bs:2 * bs], lambda hh, r: None)

    far_rows = MOBA_FAR_GROUP * bs
    n_far = (qi - 1 + MOBA_FAR_GROUP - 1) // MOBA_FAR_GROUP
    far_bufs = (far_a_scr, far_b_scr)

    def far_start(g):
        return pl.multiple_of(g * far_rows, far_rows)

    def far_bias(hh, g, r):
        return sel_scr[hh, pl.ds(g * MOBA_FAR_GROUP + r, 1), :]

    def far_scores(g, buf, head_ids):
        keys = k_ref.at[pl.ds(far_start(g), far_rows)]
        scores = {hh: _dot_nt(keys[:, heads[hh]], q_ref[:, heads[hh]]) for hh in head_ids}
        for hh in head_ids:
            far_bufs[buf][hh] = scores[hh]
            group_max = None
            for r in range(MOBA_FAR_GROUP):
                col_max = (jnp.max(scores[hh][r * bs:(r + 1) * bs], axis=0, keepdims=True)
                           + far_bias(hh, g, r))
                group_max = col_max if group_max is None else jnp.maximum(group_max, col_max)
            gmax_scr[buf, hh] = group_max

    def far_softmax(g, buf, head_ids):
        probs, alphas = {}, {}
        for hh in head_ids:
            m_old = m_scr[hh]
            m_new = jnp.maximum(m_old, gmax_scr[buf, hh])
            alphas[hh] = jnp.exp2(m_old - m_new)
            p = jnp.concatenate(
                [jnp.exp2(far_bufs[buf][hh, r * bs:(r + 1) * bs, :] - (m_new - far_bias(hh, g, r)))
                 for r in range(MOBA_FAR_GROUP)], axis=0)
            m_scr[hh] = m_new
            probs[hh] = p.astype(BF16)
        for hh in head_ids:
            acc_scr[hh] = (alphas[hh] * acc_scr[hh]
                           + _dot(vt_ref[head_values[hh], pl.ds(far_start(g), far_rows)], probs[hh]))

    def far_step(g, buf, prefetch):
        for head_ids in head_chunks():
            if prefetch:
                far_scores(g + 1, 1 - buf, head_ids)
            far_softmax(g, buf, head_ids)

    @pl.when(qi > 0)
    def _():
        select_blocks()
        start = pl.multiple_of((qi - 1) * bs, bs)
        near_update(start, 2, lambda hh, r: tab_ref[hh, :, r * bs:(r + 1) * bs],
                    lambda hh, r: adj_scr[hh] if r == 0 else None,
                    between=lambda hh: far_scores(0, 0, [hh]))

    def far_pair(i, carry):
        far_step(2 * i, 0, True)
        far_step(2 * i + 1, 1, True)
        return carry

    lax.fori_loop(0, (n_far - 1) // 2, far_pair, 0)

    @pl.when((n_far > 0) & (n_far % 2 == 0))
    def _():
        far_step(n_far - 2, 0, True)
        far_step(n_far - 1, 1, False)

    @pl.when(n_far % 2 == 1)
    def _():
        far_step(n_far - 1, 0, False)

    for hh in range(hg):
        out_ref[:, heads[hh]] = (acc_scr[hh, 0:hd, :] / acc_scr[hh, hd:hd + 1, :]).T.astype(BF16)


def _moba(q, k, vt, kmean, tab, rel_bias, bsz, seq, hg=N_HEADS):
    n, d = q.shape
    nblk = seq // MOBA_BLOCK
    assert nblk % MOBA_FAR_GROUP == 0, "far-block groups must not run past the sequence"
    assert MOBA_FAR_GROUP >= 2, "the far score buffer also parks the two near blocks"
    bs = MOBA_BLOCK
    w = hg * MOBA_HEAD_DIM
    return pl.pallas_call(
        functools.partial(_moba_kernel, nblk=nblk, hg=hg),
        grid=(bsz, N_HEADS // hg, nblk),
        in_specs=[pl.BlockSpec(memory_space=pltpu.SMEM),
                  pl.BlockSpec((bs, w), lambda b, g, i: (b * nblk + i, g)),
                  pl.BlockSpec((seq, w), lambda b, g, i: (b, g), pipeline_mode=pl.Buffered(1)),
                  pl.BlockSpec((hg * MOBA_V_ROWS, seq), lambda b, g, i: (g, b),
                               pipeline_mode=pl.Buffered(1)),
                  pl.BlockSpec((None, nblk, w), lambda b, g, i: (b, 0, g)),
                  pl.BlockSpec((hg, bs, 2 * bs), lambda b, g, i: (g, 0, 0),
                               pipeline_mode=pl.Buffered(1))],
        out_specs=pl.BlockSpec((bs, w), lambda b, g, i: (b * nblk + i, g)),
        out_shape=jax.ShapeDtypeStruct((n, d), BF16),
        scratch_shapes=[pltpu.VMEM((hg, nblk, bs), F32), pltpu.VMEM((hg, 1, bs), F32),
                        pltpu.VMEM((hg, 1, bs), F32), pltpu.VMEM((hg, MOBA_V_ROWS, bs), F32),
                        pltpu.VMEM((hg, MOBA_FAR_GROUP * bs, bs), F32),
                        pltpu.VMEM((hg, MOBA_FAR_GROUP * bs, bs), F32),
                        pltpu.VMEM((2, hg, 1, bs), F32)],
        compiler_params=_params("arbitrary", "arbitrary", "arbitrary"),
        name="moba",
    )(rel_bias, q, k, vt, kmean.reshape(bsz, nblk, d), tab)


def _odd_router_kernel(h_ref, c_ref, wo_ref, g_ref, wr_ref, h_out_ref, xn_ref, gates_ref, sel_ref,
                       cnt_ref):
    wr = wr_ref[...]
    wr_hi = wr.astype(BF16)
    wr_lo = (wr - wr_hi.astype(F32)).astype(BF16)
    n_parts = 2
    part = h_ref.shape[0] // n_parts
    rows = [slice(k * part, (k + 1) * part) for k in range(n_parts)]
    h1 = [h_ref[r, :] + _dot(c_ref[r, :], wo_ref[...]) for r in rows]
    xn = [_rms(h, g_ref[...]) for h in h1]
    xn_hi = [x.astype(BF16) for x in xn]
    xn_lo = [(x - hi.astype(F32)).astype(BF16) for x, hi in zip(xn, xn_hi)]
    logits = [_dot(hi, wr_hi) + (_dot(lo, wr_hi) + _dot(hi, wr_lo)) for hi, lo in zip(xn_hi, xn_lo)]
    count = jnp.zeros((1, N_EXPERTS), F32)
    for k, r in enumerate(rows):
        h_out_ref[r, :] = h1[k]
        xn_ref[r, :] = xn_hi[k]
        lane = lax.broadcasted_iota(jnp.int32, logits[k].shape, 1)
        v1 = jnp.max(logits[k], axis=-1, keepdims=True)
        i1 = jnp.min(jnp.where(logits[k] == v1, lane, N_EXPERTS), axis=-1, keepdims=True)
        rest = jnp.where(lane == i1, NEG_INF, logits[k])
        v2 = jnp.max(rest, axis=-1, keepdims=True)
        i2 = jnp.min(jnp.where(rest == v2, lane, N_EXPERTS), axis=-1, keepdims=True)
        e2 = jnp.exp(v2 - v1)
        w1 = 1.0 / (1.0 + e2)
        w2 = e2 / (1.0 + e2)
        gates_ref[r, :] = jnp.where(lane == i1, w1, 0.0) + jnp.where(lane == i2, w2, 0.0)
        sel = jnp.where((lane == i1) | (lane == i2), 1.0, 0.0)
        sel_ref[r, :] = sel
        count = count + jnp.sum(sel, axis=0, keepdims=True)
    cnt_ref[0] = count


def _odd_router(h, c, wo, g, wr, tm=MOE_TOKEN_TILE):
    n, d = h.shape
    row = lambda w: pl.BlockSpec((tm, w), lambda i: (i, 0))
    return pl.pallas_call(
        _odd_router_kernel,
        grid=(n // tm,),
        in_specs=[row(d), row(d), _resident(wo.shape), _resident(g.shape), _resident(wr.shape)],
        out_specs=[row(d), row(d), row(N_EXPERTS), row(N_EXPERTS),
                   pl.BlockSpec((1, 1, N_EXPERTS), lambda i: (i, 0, 0))],
        out_shape=[jax.ShapeDtypeStruct((n, d), F32), jax.ShapeDtypeStruct((n, d), BF16),
                   jax.ShapeDtypeStruct((n, N_EXPERTS), F32),
                   jax.ShapeDtypeStruct((n, N_EXPERTS), F32),
                   jax.ShapeDtypeStruct((n // tm, 1, N_EXPERTS), F32)],
        compiler_params=_params("arbitrary"),
        name="odd_out_router",
    )(h, c, wo, g, wr)


def _moe_plan(counts, n_slot_tiles):
    cnt = counts.reshape(-1, N_EXPERTS).astype(jnp.int32)
    grp = (cnt + MOE_ALIGN - 1) // MOE_ALIGN * MOE_ALIGN
    local = jnp.cumsum(grp, axis=1) - grp
    tot = jnp.sum(grp, axis=0)
    region = (tot + MOE_SLOT_TILE - 1) // MOE_SLOT_TILE * MOE_SLOT_TILE
    region_end = jnp.cumsum(region)
    region_start = region_end - region
    start = region_start[None, :] + jnp.cumsum(grp, axis=0) - grp
    n_used = region_end[-1] // MOE_SLOT_TILE
    tile_row = jnp.arange(n_slot_tiles, dtype=jnp.int32) * MOE_SLOT_TILE
    tile_expert = jnp.sum((tile_row[:, None] >= region_end[None, :]).astype(jnp.int32), axis=1)
    tile_expert = jnp.minimum(tile_expert, N_EXPERTS - 1)
    tile_expert = jnp.where(tile_row < region_end[-1], tile_expert, tile_expert[n_used - 1])
    tile_end = tile_row + MOE_SLOT_TILE
    tile_half = jnp.any((tile_end[:, None] == region_end[None, :])
                        & ((region - tot)[None, :] >= MOE_SLOT_TILE // 2)
                        & (region[None, :] > 0), axis=1).astype(jnp.int32)
    return dict(cnt=cnt, local=local.astype(jnp.int32), start=start.astype(jnp.int32),
                tile_half=tile_half,
                tail_start=(region_start + tot).astype(jnp.int32),
                tail_len=(region - tot).astype(jnp.int32), tile_expert=tile_expert,
                n_used=n_used.reshape(1).astype(jnp.int32))


def _group_copies(cnt_ref, local_ref, start_ref, t, local_buf, slot_array, sem, to_slots):
    copies = []
    for e in range(N_EXPERTS):
        rows = (cnt_ref[t, e] + MOE_ALIGN - 1) // MOE_ALIGN * MOE_ALIGN
        size = MOE_TOKEN_TILE
        while size >= MOE_ALIGN:
            off = rows & ~(2 * size - 1)
            local = local_buf.at[pl.ds(pl.multiple_of(local_ref[t, e] + off, MOE_ALIGN), size)]
            slots = slot_array.at[pl.ds(pl.multiple_of(start_ref[t, e] + off, MOE_ALIGN), size)]
            src, dst = (local, slots) if to_slots else (slots, local)
            copies.append(((rows & size) != 0, pltpu.make_async_copy(src, dst, sem)))
            size //= 2
    return copies


def _start_all(copies):
    for cond, cp in copies:
        pl.when(cond)(cp.start)


def _wait_all(copies):
    for cond, cp in copies:
        pl.when(cond)(cp.wait)


def _moe_dispatch_kernel(cnt_ref, local_ref, start_ref, tail_start_ref, tail_len_ref, x_ref, sel_ref,
                         upper_ref, xs_ref, stage_ref, zero_ref, sem, tail_sem):
    t = pl.program_id(0)
    last = pl.num_programs(0) - 1
    tile = x_ref.shape[0]
    slot = t % 2
    eye = jnp.where(lax.broadcasted_iota(jnp.int32, (N_EXPERTS, N_EXPERTS), 0)
                    == lax.broadcasted_iota(jnp.int32, (N_EXPERTS, N_EXPERTS), 1), 1.0, 0.0)
    sel_t = _dot_nt(eye.astype(BF16), sel_ref[...].astype(BF16))
    rank_t = _dot(sel_t.astype(BF16), upper_ref[...])
    expert = lax.broadcasted_iota(jnp.int32, (N_EXPERTS, 1), 0)
    base = jnp.zeros((N_EXPERTS, 1), F32)
    for e in range(N_EXPERTS):
        base = jnp.where(expert == e, local_ref[t, e].astype(F32), base)
    pos = base + rank_t
    pos_lo = jnp.min(jnp.where(sel_t > 0.5, pos, float(MOE_LOCAL_ROWS)), axis=0, keepdims=True)
    pos_hi = jnp.max(jnp.where(sel_t > 0.5, pos, -1.0), axis=0, keepdims=True)
    row = lax.broadcasted_iota(jnp.int32, (MOE_LOCAL_ROWS, tile), 0).astype(F32)
    onehot = jnp.where(row == pos_lo, 1.0, jnp.where(row == pos_hi, 1.0, 0.0)).astype(BF16)
    stage_ref[slot] = _dot(onehot, x_ref[...]).astype(BF16)

    def copies(step):
        return _group_copies(cnt_ref, local_ref, start_ref, step, stage_ref.at[step % 2], xs_ref,
                             sem.at[step % 2], to_slots=True)

    _start_all(copies(t))

    @pl.when(t > 0)
    def _():
        _wait_all(copies(t - 1))

    @pl.when(t == last)
    def _():
        _wait_all(copies(t))
        zero_ref[...] = jnp.zeros(zero_ref.shape, BF16)
        tails = []
        for e in range(N_EXPERTS):
            size = MOE_SLOT_TILE // 2
            while size >= MOE_ALIGN:
                off = tail_len_ref[e] & ~(2 * size - 1)
                dst = pl.multiple_of(tail_start_ref[e] + off, MOE_ALIGN)
                tails.append(((tail_len_ref[e] & size) != 0, pltpu.make_async_copy(
                    zero_ref.at[pl.ds(0, size)], xs_ref.at[pl.ds(dst, size)], tail_sem)))
                size //= 2
        _start_all(tails)
        _wait_all(tails)

        used_rows = tail_start_ref[N_EXPERTS - 1] + tail_len_ref[N_EXPERTS - 1]
        piece = zero_ref.shape[0]

        def unused_copy(k):
            dst = pl.multiple_of(used_rows + k * piece, piece)
            return pltpu.make_async_copy(zero_ref, xs_ref.at[pl.ds(dst, piece)], tail_sem)

        n_pieces = (xs_ref.shape[0] - used_rows) // piece
        lax.fori_loop(0, n_pieces, lambda k, c: (unused_copy(k).start(), c)[1], 0)
        lax.fori_loop(0, n_pieces, lambda k, c: (unused_copy(k).wait(), c)[1], 0)


def _moe_dispatch(xn, sel, plan, n_slot_tiles, tm=MOE_TOKEN_TILE):
    n, d = xn.shape
    upper = jnp.asarray(np.triu(np.ones((tm, tm), np.float32), 1), BF16)
    smem = pl.BlockSpec(memory_space=pltpu.SMEM)
    return pl.pallas_call(
        _moe_dispatch_kernel,
        grid=(n // tm,),
        in_specs=[smem, smem, smem, smem, smem,
                  pl.BlockSpec((tm, d), lambda i: (i, 0)),
                  pl.BlockSpec((tm, N_EXPERTS), lambda i: (i, 0)),
                  _resident((tm, tm))],
        out_specs=pl.BlockSpec(memory_space=pl.ANY),
        out_shape=jax.ShapeDtypeStruct((n_slot_tiles * MOE_SLOT_TILE, d), BF16),
        scratch_shapes=[pltpu.VMEM((2, MOE_LOCAL_ROWS, d), BF16),
                        pltpu.VMEM((MOE_SLOT_TILE // 2, d), BF16),
                        pltpu.SemaphoreType.DMA((2,)), pltpu.SemaphoreType.DMA(())],
        compiler_params=_params("arbitrary"),
        name="moe_dispatch",
    )(plan["cnt"], plan["local"], plan["start"], plan["tail_start"], plan["tail_len"], xn, sel, upper)


def _moe_experts_kernel(te_ref, nu_ref, half_ref, x_ref, wg_ref, wu_ref, wd_ref, y_ref, acc_ref):
    i = pl.program_id(0)
    f = pl.program_id(1)
    last = pl.num_programs(1) - 1
    used = i < nu_ref[0]
    half = half_ref[i] == 1
    hrows = x_ref.shape[0] // 2

    @pl.when(used & (f == 0))
    def _():
        acc_ref[...] = jnp.zeros(acc_ref.shape, F32)

    @pl.when(used & jnp.logical_not(half))
    def _():
        acc_ref[...] = _swiglu(x_ref[...], wg_ref, wu_ref, wd_ref, MOE_FF_CHUNK, acc_ref[...])

    @pl.when(used & half)
    def _():
        acc_ref[0:hrows, :] = _swiglu(x_ref[0:hrows, :], wg_ref, wu_ref, wd_ref, MOE_FF_CHUNK,
                                      acc_ref[0:hrows, :])

    @pl.when(used & (f == last))
    def _():
        y_ref[...] = acc_ref[...].astype(BF16)

    @pl.when(jnp.logical_not(used) & (f == last))
    def _():
        y_ref[...] = jnp.zeros(y_ref.shape, BF16)


def _moe_experts(xs, plan, wg, wu, wd, tf=1792):
    rows, d = xs.shape
    d_ff = wg.shape[2]
    nf = d_ff // tf
    tm = MOE_SLOT_TILE
    f_idx = lambda i, f, nu: jnp.where(i < nu[0], f, nf - 1)
    return pl.pallas_call(
        _moe_experts_kernel,
        grid_spec=pltpu.PrefetchScalarGridSpec(
            num_scalar_prefetch=3,
            grid=(rows // tm, nf),
            in_specs=[pl.BlockSpec((tm, d), lambda i, f, te, nu, hf: (jnp.minimum(i, nu[0] - 1), 0)),
                      pl.BlockSpec((None, d, tf), lambda i, f, te, nu, hf: (te[i], 0, f_idx(i, f, nu))),
                      pl.BlockSpec((None, d, tf), lambda i, f, te, nu, hf: (te[i], 0, f_idx(i, f, nu))),
                      pl.BlockSpec((None, tf, d), lambda i, f, te, nu, hf: (te[i], f_idx(i, f, nu), 0))],
            out_specs=pl.BlockSpec((tm, d), lambda i, f, te, nu, hf: (i, 0)),
            scratch_shapes=[pltpu.VMEM((tm, d), F32)]),
        out_shape=jax.ShapeDtypeStruct((rows, d), BF16),
        compiler_params=_params("arbitrary", "arbitrary"),
        name="moe_experts",
    )(plan["tile_expert"], plan["n_used"], plan["tile_half"], xs, wg, wu, wd)


def _moe_combine_kernel(cnt_ref, local_ref, start_ref, h_ref, gates_ref, sel_ref, lower_ref, p_ref,
                        gp_ref, wpg_ref, wpp_ref, gfin_ref, ys_ref, out_ref, ybuf_ref, sem):
    t = pl.program_id(0)
    tile = h_ref.shape[0]

    def copies(step):
        return _group_copies(cnt_ref, local_ref, start_ref, step, ybuf_ref.at[step % 2], ys_ref,
                             sem.at[step % 2], to_slots=False)

    @pl.when(t == 0)
    def _():
        ybuf_ref[...] = jnp.zeros(ybuf_ref.shape, BF16)
        _start_all(copies(t))

    @pl.when(t + 1 < pl.num_programs(0))
    def _():
        _start_all(copies(t + 1))

    sel = sel_ref[...] > 0.5
    gates = gates_ref[...]
    rank = _dot(lower_ref[...], sel_ref[...].astype(BF16))
    expert = lax.broadcasted_iota(jnp.int32, (1, N_EXPERTS), 1)
    base = jnp.zeros((1, N_EXPERTS), F32)
    for e in range(N_EXPERTS):
        base = jnp.where(expert == e, local_ref[t, e].astype(F32), base)
    pos = base + rank
    pos_lo = jnp.min(jnp.where(sel, pos, float(MOE_LOCAL_ROWS)), axis=1, keepdims=True)
    pos_hi = jnp.max(jnp.where(sel, pos, -1.0), axis=1, keepdims=True)
    gate_lo = jnp.sum(jnp.where(sel & (pos == pos_lo), gates, 0.0), axis=1, keepdims=True)
    gate_hi = jnp.sum(jnp.where(sel & (pos == pos_hi), gates, 0.0), axis=1, keepdims=True)
    col = lax.broadcasted_iota(jnp.int32, (tile, MOE_LOCAL_ROWS), 1).astype(F32)
    onehot_lo = jnp.where(col == pos_lo, 1.0, 0.0).astype(BF16)
    onehot_hi = jnp.where(col == pos_hi, 1.0, 0.0).astype(BF16)

    _wait_all(copies(t))
    y = ybuf_ref[t % 2]
    h = h_ref[...] + gate_lo * _dot(onehot_lo, y) + gate_hi * _dot(onehot_hi, y)
    h = _ple(h, p_ref[...], gp_ref[...], wpg_ref[...], wpp_ref[...])
    out_ref[...] = _rms(h, gfin_ref[...])


def _moe_combine(h, gates, sel, ys, plan, p, layer, gp, wpg, wpp, gfin, tm=MOE_TOKEN_TILE):
    n, d = h.shape
    lower = jnp.asarray(np.tril(np.ones((tm, tm), np.float32), -1), BF16)
    smem = pl.BlockSpec(memory_space=pltpu.SMEM)
    row = lambda w: pl.BlockSpec((tm, w), lambda i: (i, 0))
    return pl.pallas_call(
        _moe_combine_kernel,
        grid=(n // tm,),
        in_specs=[smem, smem, smem, row(d), row(N_EXPERTS), row(N_EXPERTS), _resident((tm, tm)),
                  _layer_rows(p, layer, tm), _resident(gp.shape), _resident(wpg.shape),
                  _resident(wpp.shape), _resident(gfin.shape), pl.BlockSpec(memory_space=pl.ANY)],
        out_specs=row(d),
        out_shape=jax.ShapeDtypeStruct((n, d), F32),
        scratch_shapes=[pltpu.VMEM((2, MOE_LOCAL_ROWS, d), BF16), pltpu.SemaphoreType.DMA((2,))],
        compiler_params=_params("arbitrary"),
        name="moe_combine_ple_norm",
    )(plan["cnt"], plan["local"], plan["start"], h, gates, sel, lower, p, gp, wpg, wpp, gfin, ys)


def kernel(x, p, rel_bias, final_norm, e_norm_mix, e_w_in, e_conv_w, e_sinks, e_w_out, e_norm_ffn, e_ffn_gate, e_ffn_up, e_ffn_down, e_norm_ple, e_ple_gate, e_ple_proj, o_norm_mix, o_w_qkv, o_w_o, o_norm_ffn, o_router, o_exp_gate, o_exp_up, o_exp_down, o_norm_ple, o_ple_gate, o_ple_proj):
    bsz, seq, d = x.shape
    n = bsz * seq
    bf = lambda w: w.astype(BF16)
    h = x.reshape(n, d)
    p2 = p.reshape(p.shape[0], n, p.shape[-1])

    kk = np.arange(2 * SWA_BLOCK)[:, None]
    qq = np.arange(SWA_BLOCK)[None, :]
    dist = qq - kk + SWA_BLOCK
    swa_bucket = np.where((dist >= 0) & (dist < SWA_BLOCK), _t5_bucket_np(dist), -1)
    swa_tab = _bias_table(rel_bias, swa_bucket, mult=LOG2E)
    kk = np.arange(MOBA_BLOCK)[:, None]
    qq = np.arange(MOBA_BLOCK)[None, :]
    own_bucket = np.where(qq >= kk, _t5_bucket_np(qq - kk), -1)
    moba_bucket = np.concatenate([_t5_bucket_np(qq - kk + MOBA_BLOCK), own_bucket], axis=1)
    moba_tab = _bias_table(rel_bias, moba_bucket, mult=LOG2E)

    w_in = e_w_in[0]
    q0 = 3 * SC_WIDTH
    k0 = q0 + SWA_Q_WIDTH
    v0 = k0 + SWA_KV_WIDTH
    half = N_HEADS // SWA_KV_HEADS
    q_cols = np.concatenate([np.arange(h * SWA_HEAD_DIM, (h + 1) * SWA_HEAD_DIM)
                             for j in range(half) for h in (j, half + j)])
    w_main = jnp.concatenate([w_in[:, :q0], w_in[:, q0:k0][:, q_cols], w_in[:, k0:v0]], axis=1)
    a_out, q0_, k0_, vt0 = _even_proj(h, e_norm_mix[0:1], bf(w_main), bf(w_in[:, v0:].T),
                                      e_conv_w[0], seq)
    b_out = _swa(q0_, k0_, vt0, e_sinks[0], swa_tab, bsz, seq)
    h = _even_ffn(h, a_out, b_out, p2, 0, e_w_out[0], e_norm_ffn[0:1], e_ffn_gate[0], e_ffn_up[0],
                  e_ffn_down[0], e_norm_ple[0:1], e_ple_gate[0], e_ple_proj[0])

    w_qkv = o_w_qkv[0]
    q1, k1, vt1, kmean = _odd_proj(h, o_norm_mix[0:1], bf(w_qkv[:, :d]), bf(w_qkv[:, d:2 * d]),
                                   bf(w_qkv[:, 2 * d:].T))
    c_out = _moba(q1, k1, vt1, kmean, moba_tab, rel_bias, bsz, seq)
    h, xn, gates, sel, counts = _odd_router(h, c_out, bf(o_w_o[0]), o_norm_ffn[0:1], o_router[0])
    n_groups = (n // MOE_TOKEN_TILE) * N_EXPERTS
    max_rows = 2 * n + n_groups * (MOE_ALIGN - 1) + N_EXPERTS * (MOE_SLOT_TILE - 1)
    n_slot_tiles = -(-max_rows // MOE_SLOT_TILE) + 1
    plan = _moe_plan(counts, n_slot_tiles)
    xs = _moe_dispatch(xn, sel, plan, n_slot_tiles)
    ys = _moe_experts(xs, plan, bf(o_exp_gate[0]), bf(o_exp_up[0]), bf(o_exp_down[0]))
    out = _moe_combine(h, gates, sel, ys, plan, p2, 1, o_norm_ple[0:1], bf(o_ple_gate[0]),
                       bf(o_ple_proj[0]), final_norm.reshape(1, d))
    return out.reshape(bsz, seq, d)
```
